```python
import jax
import jax.numpy as jnp
from jax import lax
import numpy as np

D_MODEL = 1024
BATCH = 8
SEQ = 4096
DEPTH = 1

EPS = 1e-6
ATTN_GROUPS = ((128, 1), (512, 4), (2048, 16))
N_GROUPS = 3
HEADS_PER_GROUP = 8
HEAD_DIM = 64
ATTN_WIDTH = HEADS_PER_GROUP * HEAD_DIM
N_ATTN_HEADS = N_GROUPS * HEADS_PER_GROUP
ATTN_QKV_COLS = N_GROUPS * 3 * ATTN_WIDTH
NUM_BUCKETS = 32
REL_MAX_DISTANCE = 1024
NEG_INF = -1e30
HG_HEADS = 8
HG_DK = D_MODEL // HG_HEADS
HG_DV = D_MODEL // HG_HEADS
HG_WIDTH = HG_HEADS * HG_DV
HG_CHUNK = 32
SPLIT_SIZES = (ATTN_QKV_COLS, ATTN_WIDTH, HG_WIDTH, HG_WIDTH, HG_WIDTH, HG_WIDTH, HG_WIDTH, D_MODEL, D_MODEL)
IN_COLS = ATTN_QKV_COLS + ATTN_WIDTH + 5 * HG_WIDTH + 2 * D_MODEL

kernel_name = "hybrid_dilated_attn_hgrn2_block"


def rms_norm(t, w):
    t32 = t.astype(jnp.float32)
    return t32 * lax.rsqrt(jnp.mean(t32 * t32, axis=-1, keepdims=True) + EPS) * w.astype(jnp.float32)


def t5_bucket(rel):
    half = NUM_BUCKETS // 2
    max_exact = half // 2
    n = np.abs(rel)
    large = max_exact + (np.log(np.maximum(n, 1) / max_exact)
                         / np.log(REL_MAX_DISTANCE / max_exact) * (half - max_exact)).astype(np.int32)
    large = np.minimum(large, half - 1)
    return np.where(rel > 0, half, 0) + np.where(n < max_exact, n, large)


def dilated_window_attention(q, k, v, bias_table, window, dilation):
    B, S, H, Dh = q.shape
    side = (window // 2) // dilation
    blk = side
    L = S // dilation
    nb = -(-L // blk)
    Lp = nb * blk

    def to_blocks(t):
        t = t.reshape(B, L, dilation, H, Dh)
        t = jnp.pad(t, ((0, 0), (0, Lp - L), (0, 0), (0, 0), (0, 0)))
        return t.reshape(B, nb, blk, dilation, H, Dh)

    def with_halo(t):
        tp = jnp.pad(t, ((0, 0), (1, 1), (0, 0), (0, 0), (0, 0), (0, 0)))
        return jnp.concatenate([tp[:, :-2], tp[:, 1:-1], tp[:, 2:]], axis=2)

    def from_blocks(t):
        t = t.reshape((B, Lp) + t.shape[3:])[:, :L]
        return t.reshape((B, S) + t.shape[3:])

    qb = to_blocks(q) * (HEAD_DIM ** -0.5)
    kh = with_halo(to_blocks(k))
    vh = with_halo(to_blocks(v))

    i_idx = np.arange(blk)[:, None]
    j_idx = np.arange(3 * blk)[None, :]
    rel = j_idx - blk - i_idx
    bias = jnp.transpose(bias_table[t5_bucket(rel * dilation)], (2, 0, 1))
    t_key = np.arange(nb)[:, None, None] * blk - blk + j_idx[None]
    valid = (np.abs(rel)[None] <= side) & (t_key >= 0) & (t_key < L)

    s = jnp.einsum('bnqrhd,bnkrhd->bnrhqk', qb, kh) + bias
    s = jnp.where(valid[None, :, None, None], s, NEG_INF)
    m = jnp.max(s, axis=-1, keepdims=True)
    p = jnp.exp(s - m)
    den = jnp.transpose(jnp.sum(p, axis=-1), (0, 1, 4, 2, 3))
    o = jnp.einsum('bnrhqk,bnkrhd->bnqrhd', p, vh) / den[..., None]
    m = jnp.transpose(m[..., 0], (0, 1, 4, 2, 3))
    return from_blocks(o), from_blocks(m), from_blocks(den)


def gla_chunk_scan(q, k, v, log_f):
    B, T, H, DK = q.shape
    DV = v.shape[-1]
    N = T // HG_CHUNK

    def chunks(t):
        return t.reshape(B, N, HG_CHUNK, H, t.shape[-1]).transpose(1, 0, 3, 2, 4)

    qc, kc, vc, gc = chunks(q), chunks(k), chunks(v), chunks(log_f)
    b = jnp.cumsum(gc, axis=3)
    b_last = b[:, :, :, -1, :]
    q_t = qc * jnp.exp(b)
    k_t = kc * jnp.exp(-b)
    k_s = kc * jnp.exp(b_last[:, :, :, None, :] - b)
    lower = jnp.tril(jnp.ones((HG_CHUNK, HG_CHUNK), dtype=bool))
    a = jnp.where(lower, jnp.einsum('nbhik,nbhjk->nbhij', q_t, k_t), 0.0)
    o_intra = jnp.einsum('nbhij,nbhjv->nbhiv', a, vc)

    def step(state, xs):
        q_n, k_n, v_n, dec_n = xs
        o_n = jnp.einsum('bhik,bhkv->bhiv', q_n, state)
        state = dec_n[..., None] * state + jnp.einsum('bhik,bhiv->bhkv', k_n, v_n)
        return state, o_n

    s0 = jnp.zeros((B, H, DK, DV), jnp.float32)
    _, o_inter = lax.scan(step, s0, (q_t, k_s, vc, jnp.exp(b_last)))
    o = o_intra + o_inter
    return o.transpose(1, 0, 3, 2, 4).reshape(B, T, H, DV)


def hgrn2_scan(q, i, z_f, lb):
    f = lb + (1.0 - lb) * jax.nn.sigmoid(z_f)
    return gla_chunk_scan(q, 1.0 - f, i, jnp.log(f))


def setup_inputs(seed: int = 0) -> dict:
    key = jax.random.key(seed)
    ks = jax.random.split(key, 12)
    f32 = jnp.float32
    nrm = jax.random.normal
    return {
        "x": nrm(ks[0], (BATCH, SEQ, D_MODEL), f32),
        "norm_w": 1.0 + 0.02 * nrm(ks[1], (DEPTH, D_MODEL), f32),
        "w_in": nrm(ks[2], (DEPTH, D_MODEL, IN_COLS), f32) * D_MODEL ** -0.5,
        "q_norm_w": 1.0 + 0.02 * nrm(ks[3], (DEPTH, N_GROUPS, HEAD_DIM), f32),
        "k_norm_w": 1.0 + 0.02 * nrm(ks[4], (DEPTH, N_GROUPS, HEAD_DIM), f32),
        "rel_bias": 0.5 * nrm(ks[5], (NUM_BUCKETS, N_ATTN_HEADS), f32),
        "lb_fwd": 0.1 * nrm(ks[6], (DEPTH + 1, HG_HEADS * HG_DK), f32),
        "lb_bwd": 0.1 * nrm(ks[7], (DEPTH + 1, HG_HEADS * HG_DK), f32),
        "hg_norm_w": 1.0 + 0.02 * nrm(ks[8], (DEPTH, HG_DV), f32),
        "w_proj_a": nrm(ks[9], (DEPTH, ATTN_WIDTH, D_MODEL), f32) * ATTN_WIDTH ** -0.5,
        "w_proj_b": nrm(ks[10], (DEPTH, HG_WIDTH, D_MODEL), f32) * HG_WIDTH ** -0.5,
        "w_out": nrm(ks[11], (DEPTH, D_MODEL, D_MODEL), f32) * D_MODEL ** -0.5,
    }


def reference(x, norm_w, w_in, q_norm_w, k_norm_w, rel_bias, lb_fwd, lb_bwd, hg_norm_w,
              w_proj_a, w_proj_b, w_out):
    f32 = jnp.float32
    B, S, _ = x.shape
    split_points = np.cumsum(SPLIT_SIZES)[:-1].tolist()
    lb_f_all = jnp.cumsum(jax.nn.softmax(lb_fwd.astype(f32), axis=0), axis=0)
    lb_b_all = jnp.cumsum(jax.nn.softmax(lb_bwd.astype(f32), axis=0), axis=0)
    bias_tab = rel_bias.astype(f32)
    h_res = x.astype(f32)
    for layer in range(DEPTH):
        h = rms_norm(h_res, norm_w[layer])
        z = jnp.einsum('bsd,dc->bsc', h, w_in[layer].astype(f32))
        qkv_a, g_a, q_b, zf_fwd, zf_bwd, i_b, g_b, zgate_a, zgate_b = jnp.split(z, split_points, axis=-1)

        qkv_a = qkv_a.reshape(B, S, N_GROUPS, 3, HEADS_PER_GROUP, HEAD_DIM)
        outs, maxes, dens = [], [], []
        for g, (window, dilation) in enumerate(ATTN_GROUPS):
            q = rms_norm(qkv_a[:, :, g, 0], q_norm_w[layer, g])
            k = rms_norm(qkv_a[:, :, g, 1], k_norm_w[layer, g])
            v = qkv_a[:, :, g, 2]
            o_g, m_g, d_g = dilated_window_attention(
                q, k, v, bias_tab[:, g * HEADS_PER_GROUP:(g + 1) * HEADS_PER_GROUP], window, dilation)
            outs.append(o_g)
            maxes.append(m_g)
            dens.append(d_g)
        m_all = jnp.stack(maxes)
        wts = jnp.stack(dens) * jnp.exp(m_all - jnp.max(m_all, axis=0, keepdims=True))
        o_a = jnp.sum(wts[..., None] * jnp.stack(outs), axis=0) / jnp.sum(wts, axis=0)[..., None]
        y_a = jnp.einsum('bsc,cd->bsd', o_a.reshape(B, S, ATTN_WIDTH) * jax.nn.silu(g_a),
                         w_proj_a[layer].astype(f32))

        q_b = q_b.reshape(B, S, HG_HEADS, HG_DK)
        i_b = i_b.reshape(B, S, HG_HEADS, HG_DV)
        zf_fwd = zf_fwd.reshape(B, S, HG_HEADS, HG_DK)
        zf_bwd = zf_bwd.reshape(B, S, HG_HEADS, HG_DK)
        lb_f = lb_f_all[layer].reshape(HG_HEADS, HG_DK)
        lb_b = lb_b_all[layer].reshape(HG_HEADS, HG_DK)
        o_fwd = hgrn2_scan(q_b, i_b, zf_fwd, lb_f)
        o_bwd = jnp.flip(hgrn2_scan(jnp.flip(q_b, 1), jnp.flip(i_b, 1), jnp.flip(zf_bwd, 1), lb_b), 1)
        o_b = rms_norm(o_fwd + o_bwd, hg_norm_w[layer]) * jax.nn.silu(g_b.reshape(B, S, HG_HEADS, HG_DV))
        y_b = jnp.einsum('bsc,cd->bsd', o_b.reshape(B, S, HG_WIDTH), w_proj_b[layer].astype(f32))

        merged = jax.nn.sigmoid(zgate_a) * y_a + jax.nn.sigmoid(zgate_b) * y_b
        h_res = h_res + jnp.einsum('bsd,de->bse', merged, w_out[layer].astype(f32))
    return h_res.astype(x.dtype)
```

```python
import functools

import numpy as np
import jax
import jax.numpy as jnp
from jax import lax
from jax.experimental import pallas as pl
from jax.experimental.pallas import tpu as pltpu

F32 = jnp.float32
BF16 = jnp.bfloat16

EPS = 1e-6
NEG_INF = -1e30
ATTN_GROUPS = ((128, 1), (512, 4), (2048, 16))
HEAD_DIM = 64
HEADS_PER_GROUP = 8
ATTN_WIDTH = HEADS_PER_GROUP * HEAD_DIM
NUM_BUCKETS = 32
REL_MAX_DISTANCE = 1024
HG_HEADS = 8
HG_DK = 128

LANES = 128
VMEM_LIMIT = 52 * 1024 * 1024

ATTN_TQ = 128
ATTN_SIDE = 64
ATTN_TK = ATTN_TQ + 2 * ATTN_SIDE

HG_CHUNK = 64
HG_TILE = 256


def _cparams(sem):
    return pltpu.CompilerParams(dimension_semantics=sem, vmem_limit_bytes=VMEM_LIMIT)


def _norm_kernel(x_ref, w_ref, o_ref):
    x = x_ref[...]
    ms = jnp.mean(x * x, axis=-1, keepdims=True)
    o_ref[...] = (x * lax.rsqrt(ms + EPS) * w_ref[...]).astype(BF16)


def _norm(x2, w):
    n, d = x2.shape
    tm = 1024
    return pl.pallas_call(
        _norm_kernel,
        out_shape=jax.ShapeDtypeStruct((n, d), BF16),
        grid=(n // tm,),
        in_specs=[pl.BlockSpec((tm, d), lambda i: (i, 0)),
                  pl.BlockSpec((1, d), lambda i: (0, 0))],
        out_specs=pl.BlockSpec((tm, d), lambda i: (i, 0)),
        compiler_params=_cparams(("parallel",)),
        name="rmsnorm",
    )(x2, w)


def _proj_kernel(x_ref, w_ref, o_ref):
    o_ref[...] = jnp.dot(x_ref[...], w_ref[...], preferred_element_type=F32).astype(o_ref.dtype)


def _proj_rest(xn, w):
    n, d = xn.shape
    c = w.shape[1]
    tm, tn = 1024, 1536
    return pl.pallas_call(
        _proj_kernel,
        out_shape=jax.ShapeDtypeStruct((n, c), BF16),
        grid=(n // tm, c // tn),
        in_specs=[pl.BlockSpec((tm, d), lambda i, j: (i, 0)),
                  pl.BlockSpec((d, tn), lambda i, j: (0, j))],
        out_specs=pl.BlockSpec((tm, tn), lambda i, j: (i, j)),
        compiler_params=_cparams(("parallel", "arbitrary")),
        name="proj_rest",
    )(xn, w)


def _proj_qkv_kernel(x_ref, w_ref, nw_ref, o_ref, *, n_res, tl, d_model):
    j = pl.program_id(3)
    for rr in range(n_res):
        z = jnp.dot(x_ref[:, rr * d_model:(rr + 1) * d_model], w_ref[...],
                    preferred_element_type=F32)
        rows = slice(rr * tl, (rr + 1) * tl)

        @pl.when(j < 2)
        def _():
            lane = lax.broadcasted_iota(jnp.int32, (tl, LANES), 1)
            first = lane < HEAD_DIM
            for c in range(ATTN_WIDTH // LANES):
                cols = slice(c * LANES, (c + 1) * LANES)
                zc = z[:, cols]
                zz = zc * zc
                s_tot = jnp.sum(zz, axis=-1, keepdims=True)
                s_a = jnp.sum(jnp.where(first, zz, 0.0), axis=-1, keepdims=True)
                ms = jnp.where(first, s_a, s_tot - s_a) * (1.0 / HEAD_DIM)
                o_ref[rows, cols] = (zc * lax.rsqrt(ms + EPS) * nw_ref[:, cols]).astype(BF16)

        @pl.when(j == 2)
        def _():
            o_ref[rows, :] = z.astype(BF16)


def _proj_qkv(xn3, w, nw, dilation):
    b, l, dd = xn3.shape
    d_model = dd // dilation
    tl = min(l, 1024)
    n_res = 1024 // tl if tl < 1024 else 1
    n_res = min(n_res, dilation)
    lt = l // tl
    tn = ATTN_WIDTH
    kern = functools.partial(_proj_qkv_kernel, n_res=n_res, tl=tl, d_model=d_model)
    return pl.pallas_call(
        kern,
        out_shape=jax.ShapeDtypeStruct((b, dilation * l, 3 * tn), BF16),
        grid=(b, dilation // n_res, lt, 3),
        in_specs=[pl.BlockSpec((None, tl, n_res * d_model), lambda bi, rb, li, j: (bi, li, rb)),
                  pl.BlockSpec((d_model, tn), lambda bi, rb, li, j: (0, j)),
                  pl.BlockSpec((None, 1, tn), lambda bi, rb, li, j: (jnp.minimum(j, 1), 0, 0))],
        out_specs=pl.BlockSpec((None, n_res * tl, tn),
                               lambda bi, rb, li, j: (bi, rb * lt + li, j)),
        compiler_params=_cparams(("parallel", "parallel", "parallel", "arbitrary")),
        name=f"proj_qkv_d{dilation}",
    )(xn3, w, nw)


def _t5_bucket(rel):
    half = NUM_BUCKETS // 2
    max_exact = half // 2
    n = np.abs(rel)
    large = max_exact + (np.log(np.maximum(n, 1) / max_exact)
                         / np.log(REL_MAX_DISTANCE / max_exact) * (half - max_exact)).astype(np.int32)
    large = np.minimum(large, half - 1)
    return np.where(rel > 0, half, 0) + np.where(n < max_exact, n, large)


def _attn_bias(bias_tab, dilation):
    t = np.arange(ATTN_TQ)[:, None]
    j = np.arange(ATTN_TK)[None, :]
    tiles = []
    for off in (0, ATTN_SIDE, 2 * ATTN_SIDE):
        rel = j - off - t
        valid = np.abs(rel) <= ATTN_SIDE
        bucket = _t5_bucket(rel * dilation)
        vals = bias_tab[bucket]
        vals = jnp.where(valid[..., None], vals, NEG_INF)
        tiles.append(jnp.transpose(vals, (2, 0, 1)))
    tiles = jnp.stack(tiles, axis=1)
    tiles = tiles.reshape(4, 2, 3, ATTN_TQ, ATTN_TK).transpose(0, 2, 1, 3, 4)
    return tiles.reshape(4, 3, 2 * ATTN_TQ, ATTN_TK)


def _attn_kernel(q_ref, k_ref, v_ref, bias_ref, o_ref, lse_ref, *, seq, n_sub, n_pairs):
    nb = seq // ATTN_TQ
    lane = lax.broadcasted_iota(jnp.int32, (ATTN_TQ, LANES), 1)
    first = lane < HEAD_DIM
    for sub in range(n_sub):
        base = sub * seq
        for pr in range(n_pairs):
            cin = slice(pr * LANES, (pr + 1) * LANES)
            cout = slice((sub * n_pairs + pr) * LANES, (sub * n_pairs + pr + 1) * LANES)

            def body(i, carry, base=base, pr=pr, cin=cin, cout=cout):
                q0 = pl.multiple_of(i * ATTN_TQ, ATTN_TQ)
                ks = pl.multiple_of(jnp.clip(i * ATTN_TQ - ATTN_SIDE, 0, seq - ATTN_TK), ATTN_SIDE)
                var = jnp.where(i == 0, 0, jnp.where(i == nb - 1, 2, 1))
                q2 = q_ref[pl.ds(base + q0, ATTN_TQ), cin]
                k2 = k_ref[pl.ds(base + ks, ATTN_TK), cin]
                v2 = v_ref[pl.ds(base + ks, ATTN_TK), cin]
                zero = jnp.zeros_like(q2)
                qq = jnp.concatenate([jnp.where(first, q2, zero), jnp.where(first, zero, q2)], axis=0)
                s = lax.dot_general(qq, k2, (((1,), (1,)), ((), ())),
                                    preferred_element_type=F32)
                s = s + bias_ref[pr, var]
                m = jnp.max(s, axis=-1, keepdims=True)
                p = jnp.exp(s - m)
                den = jnp.sum(p, axis=-1, keepdims=True)
                pv = jnp.dot(p.astype(BF16), v2, preferred_element_type=F32)
                on = pv / den
                lse = m + jnp.log(den)
                o_ref[pl.ds(q0, ATTN_TQ), cout] = jnp.where(
                    first, on[:ATTN_TQ], on[ATTN_TQ:]).astype(BF16)
                lse_ref[pl.ds(q0, ATTN_TQ), cout] = jnp.where(
                    first, jnp.broadcast_to(lse[:ATTN_TQ], (ATTN_TQ, LANES)),
                    jnp.broadcast_to(lse[ATTN_TQ:], (ATTN_TQ, LANES)))
                return carry

            lax.fori_loop(0, nb, body, 0)


def _attention(qkv, bias, dilation):
    b, rows, _ = qkv.shape
    seq = rows // dilation
    assert seq >= 2 * ATTN_TQ and seq % ATTN_TQ == 0
    if seq >= 4096:
        n_sub, n_pairs = 1, 1
    else:
        n_pairs = ATTN_WIDTH // LANES
        n_sub = max(1, min(dilation, 1024 // seq))
    cw = n_pairs * LANES
    pb = ATTN_WIDTH // cw
    kern = functools.partial(_attn_kernel, seq=seq, n_sub=n_sub, n_pairs=n_pairs)
    out_w = dilation * ATTN_WIDTH
    return pl.pallas_call(
        kern,
        out_shape=(jax.ShapeDtypeStruct((b, seq, out_w), BF16),
                   jax.ShapeDtypeStruct((b, seq, out_w), F32)),
        grid=(b, dilation // n_sub, pb),
        in_specs=[pl.BlockSpec((None, n_sub * seq, cw), lambda bi, r, p: (bi, r, p)),
                  pl.BlockSpec((None, n_sub * seq, cw), lambda bi, r, p: (bi, r, pb + p)),
                  pl.BlockSpec((None, n_sub * seq, cw), lambda bi, r, p: (bi, r, 2 * pb + p)),
                  pl.BlockSpec((n_pairs, 3, 2 * ATTN_TQ, ATTN_TK), lambda bi, r, p: (p, 0, 0, 0))],
        out_specs=(pl.BlockSpec((None, seq, n_sub * cw), lambda bi, r, p: (bi, 0, r * pb + p)),
                   pl.BlockSpec((None, seq, n_sub * cw), lambda bi, r, p: (bi, 0, r * pb + p))),
        compiler_params=_cparams(("parallel", "parallel", "parallel")),
        name=f"attn_d{dilation}",
    )(qkv, qkv, qkv, bias)


def _hgrn_mats():
    t = np.arange(HG_TILE)[:, None]
    u = np.arange(HG_TILE)[None, :]
    same = (t // HG_CHUNK) == (u // HG_CHUNK)
    mid = HG_CHUNK // 2
    pos_u = u % HG_CHUNK
    fwd = np.concatenate([same & (u <= t), same, same & (pos_u <= mid - 1)], axis=0)
    bwd = np.concatenate([same & (u >= t), same, same & (pos_u >= mid)], axis=0)
    return (jnp.asarray(fwd, dtype=BF16), jnp.asarray(bwd, dtype=BF16))


def _hgrn_kernel(q_ref, zf_ref, zb_ref, i_ref, g_ref, lbf_ref, lbb_ref, nw_ref, mf_ref, mb_ref,
                 o_ref, of_ref, ob_ref, qdf_ref, qdb_ref, kvf_ref, kvb_ref, decf_ref, decb_ref,
                 *, seq):
    n_tiles = seq // HG_TILE
    n_chunks = seq // HG_CHUNK
    cpt = HG_TILE // HG_CHUNK
    tt = lax.broadcasted_iota(jnp.int32, (HG_TILE, HG_TILE), 0)
    uu = lax.broadcasted_iota(jnp.int32, (HG_TILE, HG_TILE), 1)
    same = (tt // HG_CHUNK) == (uu // HG_CHUNK)

    def intra(t_idx, z_ref, lb_ref, m_ref, acc_ref, qd_ref, kv_ref, dec_ref, reverse):
        r0 = pl.multiple_of(t_idx * HG_TILE, HG_TILE)
        rows = pl.ds(r0, HG_TILE)
        lb = lb_ref[...]
        f = lb + (1.0 - lb) * jax.nn.sigmoid(z_ref[rows, :].astype(F32))
        g = jnp.log(f)
        g_hi = g.astype(BF16)
        g_lo = (g - g_hi.astype(F32)).astype(BF16)
        sums = jnp.dot(m_ref[...], jnp.concatenate([g_hi, g_lo], axis=1),
                       preferred_element_type=F32)
        sums = sums[:, :HG_DK] + sums[:, HG_DK:]
        beta = sums[:HG_TILE]
        tot = sums[HG_TILE:2 * HG_TILE]
        cen = sums[2 * HG_TILE:]
        q = q_ref[rows, :].astype(F32)
        v = i_ref[rows, :]
        kk = 1.0 - f
        q_c = (q * jnp.exp(beta - cen)).astype(BF16)
        k_c = (kk * jnp.exp(cen - beta)).astype(BF16)
        a = lax.dot_general(q_c, k_c, (((1,), (1,)), ((), ())), preferred_element_type=F32)
        keep = same & ((uu >= tt) if reverse else (uu <= tt))
        a = jnp.where(keep, a, 0.0)
        acc_ref[rows, :] = jnp.dot(a.astype(BF16), v, preferred_element_type=F32)
        qd_ref[rows, :] = (q * jnp.exp(beta)).astype(BF16)
        k_s = (kk * jnp.exp(tot - beta)).astype(BF16)
        dec = jnp.exp(tot)
        for c in range(cpt):
            cr = slice(c * HG_CHUNK, (c + 1) * HG_CHUNK)
            ci = t_idx * cpt + c
            kv_ref[ci] = lax.dot_general(v[cr], k_s[cr], (((0,), (0,)), ((), ())),
                                         preferred_element_type=F32)
            dec_ref[ci] = dec[c * HG_CHUNK:c * HG_CHUNK + 8]

    def phase1(t_idx, carry):
        intra(t_idx, zf_ref, lbf_ref, mf_ref, of_ref, qdf_ref, kvf_ref, decf_ref, False)
        intra(t_idx, zb_ref, lbb_ref, mb_ref, ob_ref, qdb_ref, kvb_ref, decb_ref, True)
        return carry

    lax.fori_loop(0, n_tiles, phase1, 0)

    def inter(n, st, acc_ref, qd_ref, kv_ref, dec_ref):
        rows = pl.ds(pl.multiple_of(n * HG_CHUNK, HG_CHUNK), HG_CHUNK)
        oi = lax.dot_general(qd_ref[rows, :], st.astype(BF16), (((1,), (1,)), ((), ())),
                             preferred_element_type=F32)
        acc_ref[rows, :] += oi
        dec = dec_ref[n][0:1, :]
        return dec * st + kv_ref[n]

    def phase2(n, carry):
        sf, sb = carry
        sf = inter(n, sf, of_ref, qdf_ref, kvf_ref, decf_ref)
        sb = inter(n_chunks - 1 - n, sb, ob_ref, qdb_ref, kvb_ref, decb_ref)
        return sf, sb

    s0 = jnp.zeros((HG_DK, HG_DK), F32)
    lax.fori_loop(0, n_chunks, phase2, (s0, s0))

    def phase3(t_idx, carry):
        rows = pl.ds(pl.multiple_of(t_idx * HG_TILE, HG_TILE), HG_TILE)
        o = of_ref[rows, :] + ob_ref[rows, :]
        ms = jnp.mean(o * o, axis=-1, keepdims=True)
        gate = g_ref[rows, :].astype(F32)
        o_ref[rows, :] = (o * lax.rsqrt(ms + EPS) * nw_ref[...] * (gate * jax.nn.sigmoid(gate))
                          ).astype(BF16)
        return carry

    lax.fori_loop(0, n_tiles, phase3, 0)


def _hgrn(z_rest3, lb_f, lb_b, nw, col_q, col_zf, col_zb, col_i, col_g):
    b, seq, _ = z_rest3.shape
    mf, mb = _hgrn_mats()
    n_chunks = seq // HG_CHUNK

    def zspec(col):
        return pl.BlockSpec((None, seq, HG_DK), lambda bi, h, col=col: (bi, 0, col + h))

    vec = pl.BlockSpec((1, HG_DK), lambda bi, h: (0, h))
    const = pl.BlockSpec((3 * HG_TILE, HG_TILE), lambda bi, h: (0, 0))
    return pl.pallas_call(
        functools.partial(_hgrn_kernel, seq=seq),
        out_shape=jax.ShapeDtypeStruct((b, seq, HG_HEADS * HG_DK), BF16),
        grid=(b, HG_HEADS),
        in_specs=[zspec(col_q), zspec(col_zf), zspec(col_zb), zspec(col_i), zspec(col_g),
                  vec, vec, pl.BlockSpec((1, HG_DK), lambda bi, h: (0, 0)), const, const],
        out_specs=pl.BlockSpec((None, seq, HG_DK), lambda bi, h: (bi, 0, h)),
        scratch_shapes=[pltpu.VMEM((seq, HG_DK), F32), pltpu.VMEM((seq, HG_DK), F32),
                        pltpu.VMEM((seq, HG_DK), BF16), pltpu.VMEM((seq, HG_DK), BF16),
                        pltpu.VMEM((n_chunks, HG_DK, HG_DK), F32),
                        pltpu.VMEM((n_chunks, HG_DK, HG_DK), F32),
                        pltpu.VMEM((n_chunks, 8, HG_DK), F32),
                        pltpu.VMEM((n_chunks, 8, HG_DK), F32)],
        compiler_params=_cparams(("parallel", "parallel")),
        name="hgrn2",
    )(z_rest3, z_rest3, z_rest3, z_rest3, z_rest3, lb_f, lb_b, nw, mf, mb)


def _final_kernel(o0_ref, o1_ref, o2_ref, l0_ref, l1_ref, l2_ref, ga_ref, ob_ref, za_ref, zb_ref,
                  x_ref, wa_ref, wb_ref, wo_ref, out_ref):
    l0, l1, l2 = l0_ref[...], l1_ref[...], l2_ref[...]
    mx = jnp.maximum(jnp.maximum(l0, l1), l2)
    w0, w1, w2 = jnp.exp(l0 - mx), jnp.exp(l1 - mx), jnp.exp(l2 - mx)
    num = (w0 * o0_ref[...].astype(F32) + w1 * o1_ref[...].astype(F32)
           + w2 * o2_ref[...].astype(F32))
    o_a = num / (w0 + w1 + w2)
    ga = ga_ref[...].astype(F32)
    a = (o_a * (ga * jax.nn.sigmoid(ga))).astype(BF16)
    y_a = jnp.dot(a, wa_ref[...], preferred_element_type=F32)
    y_b = jnp.dot(ob_ref[...], wb_ref[...], preferred_element_type=F32)
    merged = (jax.nn.sigmoid(za_ref[...].astype(F32)) * y_a
              + jax.nn.sigmoid(zb_ref[...].astype(F32)) * y_b)
    out_ref[...] = x_ref[...] + jnp.dot(merged.astype(BF16), wo_ref[...],
                                        preferred_element_type=F32)


def _final(o_g, lse_g, z_rest, o_b, x2, wa, wb, wo, col_ga, col_za, col_zb):
    n, d = x2.shape
    tm = 512
    aw = ATTN_WIDTH

    def rows(width, col=0):
        return pl.BlockSpec((tm, width), lambda i, col=col: (i, col))

    def full(shape):
        return pl.BlockSpec(shape, lambda i: (0, 0))

    return pl.pallas_call(
        _final_kernel,
        out_shape=jax.ShapeDtypeStruct((n, d), F32),
        grid=(n // tm,),
        in_specs=[rows(aw), rows(aw), rows(aw), rows(aw), rows(aw), rows(aw),
                  rows(aw, col_ga), rows(d), rows(d, col_za), rows(d, col_zb), rows(d),
                  full(wa.shape), full(wb.shape), full(wo.shape)],
        out_specs=rows(d),
        compiler_params=_cparams(("parallel",)),
        name="merge_out",
    )(*o_g, *lse_g, z_rest, o_b, z_rest, z_rest, x2, wa, wb, wo)


def kernel(x, norm_w, w_in, q_norm_w, k_norm_w, rel_bias, lb_fwd, lb_bwd, hg_norm_w,
           w_proj_a, w_proj_b, w_out):
    b, seq, d = x.shape
    n = b * seq
    layer = 0
    x2 = x.reshape(n, d).astype(F32)
    w = w_in[layer].astype(BF16)
    qkv_cols = 3 * len(ATTN_GROUPS) * ATTN_WIDTH

    xn = _norm(x2, norm_w[layer].reshape(1, d).astype(F32))

    w_rest = jnp.concatenate([w[:, qkv_cols + ATTN_WIDTH:], w[:, qkv_cols:qkv_cols + ATTN_WIDTH]],
                             axis=1)
    z_rest = _proj_rest(xn, w_rest)
    hw = HG_HEADS * HG_DK

    o_g, lse_g = [], []
    for g, (_, dilation) in enumerate(ATTN_GROUPS):
        sub_len = seq // dilation
        wg = w[:, g * 3 * ATTN_WIDTH:(g + 1) * 3 * ATTN_WIDTH]
        nw = jnp.stack([jnp.tile(q_norm_w[layer, g].astype(F32), HEADS_PER_GROUP) * HEAD_DIM ** -0.5,
                        jnp.tile(k_norm_w[layer, g].astype(F32), HEADS_PER_GROUP)]
                       ).reshape(2, 1, ATTN_WIDTH)
        qkv = _proj_qkv(xn.reshape(b, sub_len, dilation * d), wg, nw, dilation)
        bias = _attn_bias(rel_bias.astype(F32)[:, g * HEADS_PER_GROUP:(g + 1) * HEADS_PER_GROUP],
                          dilation)
        o, lse = _attention(qkv, bias, dilation)
        o_g.append(o.reshape(n, ATTN_WIDTH))
        lse_g.append(lse.reshape(n, ATTN_WIDTH))

    lb_f = jnp.cumsum(jax.nn.softmax(lb_fwd.astype(F32), axis=0), axis=0)[layer].reshape(1, hw)
    lb_b = jnp.cumsum(jax.nn.softmax(lb_bwd.astype(F32), axis=0), axis=0)[layer].reshape(1, hw)
    hb = hw // LANES
    o_b = _hgrn(z_rest.reshape(b, seq, -1), lb_f, lb_b,
                hg_norm_w[layer].reshape(1, HG_DK).astype(F32),
                col_q=0, col_zf=hb, col_zb=2 * hb, col_i=3 * hb, col_g=4 * hb)

    out = _final(o_g, lse_g, z_rest, o_b.reshape(n, hw), x2,
                 w_proj_a[layer].astype(BF16), w_proj_b[layer].astype(BF16),
                 w_out[layer].astype(BF16),
                 col_ga=7 * hw // ATTN_WIDTH, col_za=5, col_zb=6)
    return out.reshape(b, seq, d).astype(x.dtype)
```

```python
import functools

import numpy as np
import jax
import jax.numpy as jnp
from jax import lax
from jax.experimental import pallas as pl
from jax.experimental.pallas import tpu as pltpu

F32 = jnp.float32
BF16 = jnp.bfloat16

EPS = 1e-6
NEG_INF = -1e30
ATTN_GROUPS = ((128, 1), (512, 4), (2048, 16))
HEAD_DIM = 64
HEADS_PER_GROUP = 8
ATTN_WIDTH = HEADS_PER_GROUP * HEAD_DIM
NUM_BUCKETS = 32
REL_MAX_DISTANCE = 1024
HG_HEADS = 8
HG_DK = 128

LANES = 128
VMEM_LIMIT = 52 * 1024 * 1024

ATTN_TQ = 128
ATTN_SIDE = 64
ATTN_TK = ATTN_TQ + 2 * ATTN_SIDE

HG_CHUNK = 64
HG_TILE = 256


def _cparams(sem):
    return pltpu.CompilerParams(dimension_semantics=sem, vmem_limit_bytes=VMEM_LIMIT)


def _norm_kernel(x_ref, w_ref, o_ref, *rest, tm, dilations):
    perm_refs, y_ref = rest[:-1], rest[-1]
    x = x_ref[...]
    ms = jnp.mean(x * x, axis=-1, keepdims=True)
    y = x * lax.rsqrt(ms + EPS) * w_ref[...]
    o_ref[...] = y.astype(BF16)
    n_lane_blocks = y.shape[1] // LANES
    for c in range(n_lane_blocks):
        y_ref[c] = y[:, c * LANES:(c + 1) * LANES]
    for p_ref, dil in zip(perm_refs, dilations):
        for r in range(dil):
            for c in range(n_lane_blocks):
                p_ref[r, :, c * LANES:(c + 1) * LANES] = (
                    y_ref[c, pl.ds(r, tm // dil, stride=dil), :].astype(BF16))


def _norm(x3, w, dilations):
    b, seq, d = x3.shape
    tm = 1024
    out_shape = [jax.ShapeDtypeStruct((b, seq, d), BF16)]
    out_specs = [pl.BlockSpec((None, tm, d), lambda bi, i: (bi, i, 0))]
    for dil in dilations:
        out_shape.append(jax.ShapeDtypeStruct((b, dil, seq // dil, d), BF16))
        out_specs.append(pl.BlockSpec((None, dil, tm // dil, d), lambda bi, i: (bi, 0, i, 0)))
    return pl.pallas_call(
        functools.partial(_norm_kernel, tm=tm, dilations=dilations),
        out_shape=out_shape,
        grid=(b, seq // tm),
        in_specs=[pl.BlockSpec((None, tm, d), lambda bi, i: (bi, i, 0)),
                  pl.BlockSpec((1, d), lambda bi, i: (0, 0))],
        out_specs=out_specs,
        scratch_shapes=[pltpu.VMEM((d // LANES, tm, LANES), F32)],
        compiler_params=_cparams(("parallel", "parallel")),
        name="rmsnorm",
    )(x3, w)


def _proj_kernel(x_ref, w_ref, o_ref):
    o_ref[...] = jnp.dot(x_ref[...], w_ref[...], preferred_element_type=F32).astype(o_ref.dtype)


def _proj_rest(xn, w):
    n, d = xn.shape
    c = w.shape[1]
    tm, tn = 1024, 1536
    return pl.pallas_call(
        _proj_kernel,
        out_shape=jax.ShapeDtypeStruct((n, c), BF16),
        grid=(n // tm, c // tn),
        in_specs=[pl.BlockSpec((tm, d), lambda i, j: (i, 0)),
                  pl.BlockSpec((d, tn), lambda i, j: (0, j))],
        out_specs=pl.BlockSpec((tm, tn), lambda i, j: (i, j)),
        compiler_params=_cparams(("parallel", "arbitrary")),
        name="proj_rest",
    )(xn, w)


def _proj_qkv_kernel(x_ref, w_ref, nw_ref, o_ref):
    j = pl.program_id(1)
    z = jnp.dot(x_ref[...], w_ref[...], preferred_element_type=F32)

    @pl.when(j < 2)
    def _():
        lane = lax.broadcasted_iota(jnp.int32, (z.shape[0], LANES), 1)
        first = lane < HEAD_DIM
        for c in range(ATTN_WIDTH // LANES):
            cols = slice(c * LANES, (c + 1) * LANES)
            zc = z[:, cols]
            zz = zc * zc
            s_tot = jnp.sum(zz, axis=-1, keepdims=True)
            s_a = jnp.sum(jnp.where(first, zz, 0.0), axis=-1, keepdims=True)
            ms = jnp.where(first, s_a, s_tot - s_a) * (1.0 / HEAD_DIM)
            o_ref[:, cols] = (zc * lax.rsqrt(ms + EPS) * nw_ref[:, cols]).astype(BF16)

    @pl.when(j == 2)
    def _():
        o_ref[...] = z.astype(BF16)


def _proj_qkv(xn2, w, nw, name):
    n, d = xn2.shape
    tm, tn = 1024, ATTN_WIDTH
    return pl.pallas_call(
        _proj_qkv_kernel,
        out_shape=jax.ShapeDtypeStruct((n, 3 * tn), BF16),
        grid=(n // tm, 3),
        in_specs=[pl.BlockSpec((tm, d), lambda i, j: (i, 0)),
                  pl.BlockSpec((d, tn), lambda i, j: (0, j)),
                  pl.BlockSpec((None, 1, tn), lambda i, j: (jnp.minimum(j, 1), 0, 0))],
        out_specs=pl.BlockSpec((tm, tn), lambda i, j: (i, j)),
        compiler_params=_cparams(("parallel", "arbitrary")),
        name=name,
    )(xn2, w, nw)


def _t5_bucket(rel):
    half = NUM_BUCKETS // 2
    max_exact = half // 2
    n = np.abs(rel)
    large = max_exact + (np.log(np.maximum(n, 1) / max_exact)
                         / np.log(REL_MAX_DISTANCE / max_exact) * (half - max_exact)).astype(np.int32)
    large = np.minimum(large, half - 1)
    return np.where(rel > 0, half, 0) + np.where(n < max_exact, n, large)


def _attn_bias(bias_tab, dilation):
    t = np.arange(ATTN_TQ)[:, None]
    j = np.arange(ATTN_TK)[None, :]
    rel = np.stack([j - off - t for off in (0, ATTN_SIDE, 2 * ATTN_SIDE)])
    valid = (np.abs(rel) <= ATTN_SIDE).reshape(-1)
    bucket = _t5_bucket(rel * dilation).reshape(-1).astype(np.int32)
    onehot = (jnp.asarray(bucket)[None, :] == jnp.arange(NUM_BUCKETS, dtype=jnp.int32)[:, None])
    vals = jnp.dot(bias_tab.T, onehot.astype(F32), precision=lax.Precision.HIGHEST)
    vals = jnp.where(jnp.asarray(valid)[None, :], vals, NEG_INF)
    return vals.reshape(HEADS_PER_GROUP, 3, ATTN_TQ, ATTN_TK)


def _attn_kernel(q_ref, k_ref, v_ref, bias_ref, o_ref, lse_ref, *, seq, n_sub, n_pairs):
    nb = seq // ATTN_TQ
    lane = lax.broadcasted_iota(jnp.int32, (ATTN_TQ, LANES), 1)
    first = lane < HEAD_DIM
    for sub in range(n_sub):
        for pr in range(n_pairs):
            cols = slice(pr * LANES, (pr + 1) * LANES)

            def body(i, carry, sub=sub, pr=pr, cols=cols):
                q0 = pl.multiple_of(i * ATTN_TQ, ATTN_TQ)
                ks = pl.multiple_of(jnp.clip(i * ATTN_TQ - ATTN_SIDE, 0, seq - ATTN_TK), ATTN_SIDE)
                var = jnp.where(i == 0, 0, jnp.where(i == nb - 1, 2, 1))
                q2 = q_ref[sub, pl.ds(q0, ATTN_TQ), cols]
                k2 = k_ref[sub, pl.ds(ks, ATTN_TK), cols]
                v2 = v_ref[sub, pl.ds(ks, ATTN_TK), cols]
                zero = jnp.zeros_like(q2)
                qq = jnp.concatenate([jnp.where(first, q2, zero), jnp.where(first, zero, q2)], axis=0)
                s = lax.dot_general(qq, k2, (((1,), (1,)), ((), ())),
                                    preferred_element_type=F32)
                s = s + jnp.concatenate([bias_ref[2 * pr, var], bias_ref[2 * pr + 1, var]], axis=0)
                m = jnp.max(s, axis=-1, keepdims=True)
                p = jnp.exp(s - m)
                den = jnp.sum(p, axis=-1, keepdims=True)
                pv = jnp.dot(p.astype(BF16), v2, preferred_element_type=F32)
                on = pv / den
                lse = m + jnp.log(den)
                o_ref[sub, pl.ds(q0, ATTN_TQ), cols] = jnp.where(
                    first, on[:ATTN_TQ], on[ATTN_TQ:]).astype(BF16)
                lse_ref[sub, pl.ds(q0, ATTN_TQ), cols] = jnp.where(
                    first, jnp.broadcast_to(lse[:ATTN_TQ], (ATTN_TQ, LANES)),
                    jnp.broadcast_to(lse[ATTN_TQ:], (ATTN_TQ, LANES)))
                return carry

            lax.fori_loop(0, nb, body, 0)


def _attention(qkv, bias, name):
    n_seq, seq, _ = qkv.shape
    assert seq >= 2 * ATTN_TQ and seq % ATTN_TQ == 0
    if seq >= 4096:
        n_sub, n_pairs = 1, 1
    else:
        n_pairs = ATTN_WIDTH // LANES
        n_sub = max(1, min(n_seq, 1024 // seq))
    cw = n_pairs * LANES
    pb = ATTN_WIDTH // cw
    kern = functools.partial(_attn_kernel, seq=seq, n_sub=n_sub, n_pairs=n_pairs)
    blk = (n_sub, seq, cw)
    return pl.pallas_call(
        kern,
        out_shape=(jax.ShapeDtypeStruct((n_seq, seq, ATTN_WIDTH), BF16),
                   jax.ShapeDtypeStruct((n_seq, seq, ATTN_WIDTH), F32)),
        grid=(n_seq // n_sub, pb),
        in_specs=[pl.BlockSpec(blk, lambda r, p: (r, 0, p)),
                  pl.BlockSpec(blk, lambda r, p: (r, 0, pb + p)),
                  pl.BlockSpec(blk, lambda r, p: (r, 0, 2 * pb + p)),
                  pl.BlockSpec((2 * n_pairs, 3, ATTN_TQ, ATTN_TK), lambda r, p: (p, 0, 0, 0))],
        out_specs=(pl.BlockSpec(blk, lambda r, p: (r, 0, p)),
                   pl.BlockSpec(blk, lambda r, p: (r, 0, p))),
        compiler_params=_cparams(("parallel", "parallel")),
        name=name,
    )(qkv, qkv, qkv, bias)


def _hgrn_mats():
    t = np.arange(HG_TILE)[:, None]
    u = np.arange(HG_TILE)[None, :]
    same = (t // HG_CHUNK) == (u // HG_CHUNK)
    mid = HG_CHUNK // 2
    pos_u = u % HG_CHUNK
    fwd = np.concatenate([same & (u <= t), same, same & (pos_u <= mid - 1)], axis=0)
    bwd = np.concatenate([same & (u >= t), same, same & (pos_u >= mid)], axis=0)
    return (jnp.asarray(fwd, dtype=BF16), jnp.asarray(bwd, dtype=BF16))


def _hgrn_kernel(q_ref, zf_ref, zb_ref, i_ref, g_ref, lbf_ref, lbb_ref, nw_ref, mf_ref, mb_ref,
                 o_ref, of_ref, ob_ref, qdf_ref, qdb_ref, kvf_ref, kvb_ref, decf_ref, decb_ref,
                 *, seq):
    n_tiles = seq // HG_TILE
    n_chunks = seq // HG_CHUNK
    cpt = HG_TILE // HG_CHUNK
    tt = lax.broadcasted_iota(jnp.int32, (HG_TILE, HG_TILE), 0)
    uu = lax.broadcasted_iota(jnp.int32, (HG_TILE, HG_TILE), 1)
    same = (tt // HG_CHUNK) == (uu // HG_CHUNK)

    def intra(t_idx, z_ref, lb_ref, m_ref, acc_ref, qd_ref, kv_ref, dec_ref, reverse):
        r0 = pl.multiple_of(t_idx * HG_TILE, HG_TILE)
        rows = pl.ds(r0, HG_TILE)
        lb = lb_ref[...]
        f = lb + (1.0 - lb) * jax.nn.sigmoid(z_ref[rows, :].astype(F32))
        g = jnp.log(f)
        g_hi = g.astype(BF16)
        g_lo = (g - g_hi.astype(F32)).astype(BF16)
        sums = jnp.dot(m_ref[...], jnp.concatenate([g_hi, g_lo], axis=1),
                       preferred_element_type=F32)
        sums = sums[:, :HG_DK] + sums[:, HG_DK:]
        beta = sums[:HG_TILE]
        tot = sums[HG_TILE:2 * HG_TILE]
        cen = sums[2 * HG_TILE:]
        q = q_ref[rows, :].astype(F32)
        v = i_ref[rows, :]
        kk = 1.0 - f
        q_c = (q * jnp.exp(beta - cen)).astype(BF16)
        k_c = (kk * jnp.exp(cen - beta)).astype(BF16)
        a = lax.dot_general(q_c, k_c, (((1,), (1,)), ((), ())), preferred_element_type=F32)
        keep = same & ((uu >= tt) if reverse else (uu <= tt))
        a = jnp.where(keep, a, 0.0)
        acc_ref[rows, :] = jnp.dot(a.astype(BF16), v, preferred_element_type=F32)
        qd_ref[rows, :] = (q * jnp.exp(beta)).astype(BF16)
        k_s = (kk * jnp.exp(tot - beta)).astype(BF16)
        dec = jnp.exp(tot)
        for c in range(cpt):
            cr = slice(c * HG_CHUNK, (c + 1) * HG_CHUNK)
            ci = t_idx * cpt + c
            kv_ref[ci] = lax.dot_general(v[cr], k_s[cr], (((0,), (0,)), ((), ())),
                                         preferred_element_type=F32)
            dec_ref[ci] = dec[c * HG_CHUNK:c * HG_CHUNK + 8]

    def phase1(t_idx, carry):
        intra(t_idx, zf_ref, lbf_ref, mf_ref, of_ref, qdf_ref, kvf_ref, decf_ref, False)
        intra(t_idx, zb_ref, lbb_ref, mb_ref, ob_ref, qdb_ref, kvb_ref, decb_ref, True)
        return carry

    lax.fori_loop(0, n_tiles, phase1, 0, unroll=2)

    def inter(n, st, acc_ref, qd_ref, kv_ref, dec_ref):
        rows = pl.ds(pl.multiple_of(n * HG_CHUNK, HG_CHUNK), HG_CHUNK)
        oi = lax.dot_general(qd_ref[rows, :], st.astype(BF16), (((1,), (1,)), ((), ())),
                             preferred_element_type=F32)
        acc_ref[rows, :] += oi
        dec = dec_ref[n][0:1, :]
        return dec * st + kv_ref[n]

    def phase2(n, carry):
        sf, sb = carry
        sf = inter(n, sf, of_ref, qdf_ref, kvf_ref, decf_ref)
        sb = inter(n_chunks - 1 - n, sb, ob_ref, qdb_ref, kvb_ref, decb_ref)
        return sf, sb

    s0 = jnp.zeros((HG_DK, HG_DK), F32)
    lax.fori_loop(0, n_chunks, phase2, (s0, s0), unroll=8)

    def phase3(t_idx, carry):
        rows = pl.ds(pl.multiple_of(t_idx * HG_TILE, HG_TILE), HG_TILE)
        o = of_ref[rows, :] + ob_ref[rows, :]
        ms = jnp.mean(o * o, axis=-1, keepdims=True)
        gate = g_ref[rows, :].astype(F32)
        o_ref[rows, :] = (o * lax.rsqrt(ms + EPS) * nw_ref[...] * (gate * jax.nn.sigmoid(gate))
                          ).astype(BF16)
        return carry

    lax.fori_loop(0, n_tiles, phase3, 0)


def _hgrn(z_rest3, lb_f, lb_b, nw, col_q, col_zf, col_zb, col_i, col_g):
    b, seq, _ = z_rest3.shape
    mf, mb = _hgrn_mats()
    n_chunks = seq // HG_CHUNK

    def zspec(col):
        return pl.BlockSpec((None, seq, HG_DK), lambda bi, h, col=col: (bi, 0, col + h))

    vec = pl.BlockSpec((1, HG_DK), lambda bi, h: (0, h))
    const = pl.BlockSpec((3 * HG_TILE, HG_TILE), lambda bi, h: (0, 0))
    return pl.pallas_call(
        functools.partial(_hgrn_kernel, seq=seq),
        out_shape=jax.ShapeDtypeStruct((b, seq, HG_HEADS * HG_DK), BF16),
        grid=(b, HG_HEADS),
        in_specs=[zspec(col_q), zspec(col_zf), zspec(col_zb), zspec(col_i), zspec(col_g),
                  vec, vec, pl.BlockSpec((1, HG_DK), lambda bi, h: (0, 0)), const, const],
        out_specs=pl.BlockSpec((None, seq, HG_DK), lambda bi, h: (bi, 0, h)),
        scratch_shapes=[pltpu.VMEM((seq, HG_DK), F32), pltpu.VMEM((seq, HG_DK), F32),
                        pltpu.VMEM((seq, HG_DK), BF16), pltpu.VMEM((seq, HG_DK), BF16),
                        pltpu.VMEM((n_chunks, HG_DK, HG_DK), F32),
                        pltpu.VMEM((n_chunks, HG_DK, HG_DK), F32),
                        pltpu.VMEM((n_chunks, 8, HG_DK), F32),
                        pltpu.VMEM((n_chunks, 8, HG_DK), F32)],
        compiler_params=_cparams(("parallel", "parallel")),
        name="hgrn2",
    )(z_rest3, z_rest3, z_rest3, z_rest3, z_rest3, lb_f, lb_b, nw, mf, mb)


def _final_kernel(o0_ref, l0_ref, o1_ref, l1_ref, o2_ref, l2_ref, ga_ref, ob_ref, za_ref, zb_ref,
                  x_ref, wa_ref, wb_ref, wo_ref, out_ref, so1_ref, sl1_ref, so2_ref, sl2_ref,
                  *, tm, dilations):
    for src, dst, dil in ((o1_ref, so1_ref, dilations[0]), (l1_ref, sl1_ref, dilations[0]),
                          (o2_ref, so2_ref, dilations[1]), (l2_ref, sl2_ref, dilations[1])):
        for r in range(dil):
            val = src[r].astype(F32)
            for c in range(ATTN_WIDTH // LANES):
                dst[c, pl.ds(r, tm // dil, stride=dil), :] = val[:, c * LANES:(c + 1) * LANES]

    def natural(ref):
        return jnp.concatenate([ref[c] for c in range(ATTN_WIDTH // LANES)], axis=1)

    l0, l1, l2 = l0_ref[...], natural(sl1_ref), natural(sl2_ref)
    mx = jnp.maximum(jnp.maximum(l0, l1), l2)
    w0, w1, w2 = jnp.exp(l0 - mx), jnp.exp(l1 - mx), jnp.exp(l2 - mx)
    num = w0 * o0_ref[...].astype(F32) + w1 * natural(so1_ref) + w2 * natural(so2_ref)
    o_a = num / (w0 + w1 + w2)
    ga = ga_ref[...].astype(F32)
    a = (o_a * (ga * jax.nn.sigmoid(ga))).astype(BF16)
    y_a = jnp.dot(a, wa_ref[...], preferred_element_type=F32)
    y_b = jnp.dot(ob_ref[...], wb_ref[...], preferred_element_type=F32)
    merged = (jax.nn.sigmoid(za_ref[...].astype(F32)) * y_a
              + jax.nn.sigmoid(zb_ref[...].astype(F32)) * y_b)
    out_ref[...] = x_ref[...] + jnp.dot(merged.astype(BF16), wo_ref[...],
                                        preferred_element_type=F32)


def _final(o_g, lse_g, z_rest, o_b, x3, wa, wb, wo, dilations, col_ga, col_za, col_zb):
    b, seq, d = x3.shape
    tm = 512
    aw = ATTN_WIDTH

    def rows(width, col=0):
        return pl.BlockSpec((None, tm, width), lambda bi, i, col=col: (bi, i, col))

    def perm(dil):
        return pl.BlockSpec((None, dil, tm // dil, aw), lambda bi, i: (bi, 0, i, 0))

    def full(shape):
        return pl.BlockSpec(shape, lambda bi, i: (0, 0))

    d1, d2 = dilations
    return pl.pallas_call(
        functools.partial(_final_kernel, tm=tm, dilations=dilations),
        out_shape=jax.ShapeDtypeStruct((b, seq, d), F32),
        grid=(b, seq // tm),
        in_specs=[rows(aw), rows(aw), perm(d1), perm(d1), perm(d2), perm(d2),
                  rows(aw, col_ga), rows(d), rows(d, col_za), rows(d, col_zb), rows(d),
                  full(wa.shape), full(wb.shape), full(wo.shape)],
        out_specs=rows(d),
        scratch_shapes=[pltpu.VMEM((aw // LANES, tm, LANES), F32)] * 4,
        compiler_params=_cparams(("parallel", "parallel")),
        name="merge_out",
    )(o_g[0], lse_g[0], o_g[1], lse_g[1], o_g[2], lse_g[2], z_rest, o_b, z_rest, z_rest, x3,
      wa, wb, wo)


def kernel(x, norm_w, w_in, q_norm_w, k_norm_w, rel_bias, lb_fwd, lb_bwd, hg_norm_w,
           w_proj_a, w_proj_b, w_out):
    b, seq, d = x.shape
    n = b * seq
    layer = 0
    w = w_in[layer].astype(BF16)
    qkv_cols = 3 * len(ATTN_GROUPS) * ATTN_WIDTH
    dilations = tuple(dil for _, dil in ATTN_GROUPS)
    assert dilations[0] == 1

    xn_all = _norm(x.astype(F32), norm_w[layer].reshape(1, d).astype(F32), dilations[1:])
    xn = xn_all[0].reshape(n, d)

    w_rest = jnp.concatenate([w[:, qkv_cols + ATTN_WIDTH:], w[:, qkv_cols:qkv_cols + ATTN_WIDTH]],
                             axis=1)
    z_rest = _proj_rest(xn, w_rest).reshape(b, seq, -1)
    hw = HG_HEADS * HG_DK

    o_g, lse_g = [], []
    for g, dilation in enumerate(dilations):
        sub_len = seq // dilation
        wg = w[:, g * 3 * ATTN_WIDTH:(g + 1) * 3 * ATTN_WIDTH]
        nw = jnp.stack([jnp.tile(q_norm_w[layer, g].astype(F32), HEADS_PER_GROUP) * HEAD_DIM ** -0.5,
                        jnp.tile(k_norm_w[layer, g].astype(F32), HEADS_PER_GROUP)]
                       ).reshape(2, 1, ATTN_WIDTH)
        qkv = _proj_qkv(xn_all[g].reshape(n, d), wg, nw, f"proj_qkv_d{dilation}")
        bias = _attn_bias(rel_bias.astype(F32)[:, g * HEADS_PER_GROUP:(g + 1) * HEADS_PER_GROUP],
                          dilation)
        o, lse = _attention(qkv.reshape(b * dilation, sub_len, 3 * ATTN_WIDTH), bias,
                            f"attn_d{dilation}")
        shape = (b, seq, ATTN_WIDTH) if dilation == 1 else (b, dilation, sub_len, ATTN_WIDTH)
        o_g.append(o.reshape(shape))
        lse_g.append(lse.reshape(shape))

    lb_f = jnp.cumsum(jax.nn.softmax(lb_fwd.astype(F32), axis=0), axis=0)[layer].reshape(1, hw)
    lb_b = jnp.cumsum(jax.nn.softmax(lb_bwd.astype(F32), axis=0), axis=0)[layer].reshape(1, hw)
    hb = hw // LANES
    o_b = _hgrn(z_rest, lb_f, lb_b, hg_norm_w[layer].reshape(1, HG_DK).astype(F32),
                col_q=0, col_zf=hb, col_zb=2 * hb, col_i=3 * hb, col_g=4 * hb)

    out = _final(o_g, lse_g, z_rest, o_b, x.astype(F32),
                 w_proj_a[layer].astype(BF16), w_proj_b[layer].astype(BF16),
                 w_out[layer].astype(BF16), dilations[1:],
                 col_ga=7 * hw // ATTN_WIDTH, col_za=5, col_zb=6)
    return out.astype(x.dtype)
```

```python
import functools

import numpy as np
import jax
import jax.numpy as jnp
from jax import lax
from jax.experimental import pallas as pl
from jax.experimental.pallas import tpu as pltpu

F32 = jnp.float32
BF16 = jnp.bfloat16

EPS = 1e-6
NEG_INF = -1e30
ATTN_GROUPS = ((128, 1), (512, 4), (2048, 16))
HEAD_DIM = 64
HEADS_PER_GROUP = 8
ATTN_WIDTH = HEADS_PER_GROUP * HEAD_DIM
NUM_BUCKETS = 32
REL_MAX_DISTANCE = 1024
HG_HEADS = 8
HG_DK = 128

LANES = 128
VMEM_LIMIT = 52 * 1024 * 1024

ATTN_TQ = 128
ATTN_SIDE = 64
ATTN_TK = ATTN_TQ + 2 * ATTN_SIDE

HG_CHUNK = 64
HG_TILE = 256


def _cparams(sem):
    return pltpu.CompilerParams(dimension_semantics=sem, vmem_limit_bytes=VMEM_LIMIT)


def _norm_kernel(x_ref, w_ref, o_ref, *rest, tm, dilations):
    perm_refs, y_ref = rest[:-1], rest[-1]
    x = x_ref[...]
    ms = jnp.mean(x * x, axis=-1, keepdims=True)
    y = x * lax.rsqrt(ms + EPS) * w_ref[...]
    o_ref[...] = y.astype(BF16)
    n_lane_blocks = y.shape[1] // LANES
    for c in range(n_lane_blocks):
        y_ref[c] = y[:, c * LANES:(c + 1) * LANES]
    for p_ref, dil in zip(perm_refs, dilations):
        for r in range(dil):
            for c in range(n_lane_blocks):
                p_ref[r, :, c * LANES:(c + 1) * LANES] = (
                    y_ref[c, pl.ds(r, tm // dil, stride=dil), :].astype(BF16))


def _norm(x3, w, dilations):
    b, seq, d = x3.shape
    tm = 1024
    out_shape = [jax.ShapeDtypeStruct((b, seq, d), BF16)]
    out_specs = [pl.BlockSpec((None, tm, d), lambda bi, i: (bi, i, 0))]
    for dil in dilations:
        out_shape.append(jax.ShapeDtypeStruct((b, dil, seq // dil, d), BF16))
        out_specs.append(pl.BlockSpec((None, dil, tm // dil, d), lambda bi, i: (bi, 0, i, 0)))
    return pl.pallas_call(
        functools.partial(_norm_kernel, tm=tm, dilations=dilations),
        out_shape=out_shape,
        grid=(b, seq // tm),
        in_specs=[pl.BlockSpec((None, tm, d), lambda bi, i: (bi, i, 0)),
                  pl.BlockSpec((1, d), lambda bi, i: (0, 0))],
        out_specs=out_specs,
        scratch_shapes=[pltpu.VMEM((d // LANES, tm, LANES), F32)],
        compiler_params=_cparams(("parallel", "parallel")),
        name="rmsnorm",
    )(x3, w)


def _proj_kernel(x_ref, w_ref, o_ref):
    o_ref[...] = jnp.dot(x_ref[...], w_ref[...], preferred_element_type=F32).astype(o_ref.dtype)


def _proj_rest(xn, w):
    n, d = xn.shape
    c = w.shape[1]
    tm, tn = 1024, 1536
    return pl.pallas_call(
        _proj_kernel,
        out_shape=jax.ShapeDtypeStruct((n, c), BF16),
        grid=(n // tm, c // tn),
        in_specs=[pl.BlockSpec((tm, d), lambda i, j: (i, 0)),
                  pl.BlockSpec((d, tn), lambda i, j: (0, j))],
        out_specs=pl.BlockSpec((tm, tn), lambda i, j: (i, j)),
        compiler_params=_cparams(("parallel", "arbitrary")),
        name="proj_rest",
    )(xn, w)


def _proj_qkv_kernel(x_ref, w_ref, nw_ref, o_ref):
    j = pl.program_id(1)
    z = jnp.dot(x_ref[...], w_ref[...], preferred_element_type=F32)

    @pl.when(j < 2)
    def _():
        lane = lax.broadcasted_iota(jnp.int32, (z.shape[0], LANES), 1)
        first = lane < HEAD_DIM
        for c in range(ATTN_WIDTH // LANES):
            cols = slice(c * LANES, (c + 1) * LANES)
            zc = z[:, cols]
            zz = zc * zc
            s_tot = jnp.sum(zz, axis=-1, keepdims=True)
            s_a = jnp.sum(jnp.where(first, zz, 0.0), axis=-1, keepdims=True)
            ms = jnp.where(first, s_a, s_tot - s_a) * (1.0 / HEAD_DIM)
            o_ref[:, cols] = (zc * lax.rsqrt(ms + EPS) * nw_ref[:, cols]).astype(BF16)

    @pl.when(j == 2)
    def _():
        o_ref[...] = z.astype(BF16)


def _proj_qkv(xn2, w, nw, name):
    n, d = xn2.shape
    tm, tn = 1024, ATTN_WIDTH
    return pl.pallas_call(
        _proj_qkv_kernel,
        out_shape=jax.ShapeDtypeStruct((n, 3 * tn), BF16),
        grid=(n // tm, 3),
        in_specs=[pl.BlockSpec((tm, d), lambda i, j: (i, 0)),
                  pl.BlockSpec((d, tn), lambda i, j: (0, j)),
                  pl.BlockSpec((None, 1, tn), lambda i, j: (jnp.minimum(j, 1), 0, 0))],
        out_specs=pl.BlockSpec((tm, tn), lambda i, j: (i, j)),
        compiler_params=_cparams(("parallel", "arbitrary")),
        name=name,
    )(xn2, w, nw)


def _t5_bucket(rel):
    half = NUM_BUCKETS // 2
    max_exact = half // 2
    n = np.abs(rel)
    large = max_exact + (np.log(np.maximum(n, 1) / max_exact)
                         / np.log(REL_MAX_DISTANCE / max_exact) * (half - max_exact)).astype(np.int32)
    large = np.minimum(large, half - 1)
    return np.where(rel > 0, half, 0) + np.where(n < max_exact, n, large)


def _attn_bias(bias_tab, dilation):
    t = np.arange(ATTN_TQ)[:, None]
    j = np.arange(ATTN_TK)[None, :]
    rel = np.stack([j - off - t for off in (0, ATTN_SIDE, 2 * ATTN_SIDE)])
    valid = (np.abs(rel) <= ATTN_SIDE).reshape(-1)
    bucket = _t5_bucket(rel * dilation).reshape(-1).astype(np.int32)
    onehot = (jnp.asarray(bucket)[None, :] == jnp.arange(NUM_BUCKETS, dtype=jnp.int32)[:, None])
    vals = jnp.dot(bias_tab.T, onehot.astype(F32), precision=lax.Precision.HIGHEST)
    vals = jnp.where(jnp.asarray(valid)[None, :], vals, NEG_INF)
    return vals.reshape(HEADS_PER_GROUP, 3, ATTN_TQ, ATTN_TK)


def _attn_kernel(q_ref, k_ref, v_ref, bias_ref, o_ref, lse_ref, *, seq, n_sub, n_pairs):
    nb = seq // ATTN_TQ
    lane = lax.broadcasted_iota(jnp.int32, (ATTN_TQ, LANES), 1)
    first = lane < HEAD_DIM
    for sub in range(n_sub):
        for pr in range(n_pairs):
            cols = slice(pr * LANES, (pr + 1) * LANES)

            def body(i, carry, sub=sub, pr=pr, cols=cols):
                q0 = pl.multiple_of(i * ATTN_TQ, ATTN_TQ)
                ks = pl.multiple_of(jnp.clip(i * ATTN_TQ - ATTN_SIDE, 0, seq - ATTN_TK), ATTN_SIDE)
                var = jnp.where(i == 0, 0, jnp.where(i == nb - 1, 2, 1))
                q2 = q_ref[sub, pl.ds(q0, ATTN_TQ), cols]
                k2 = k_ref[sub, pl.ds(ks, ATTN_TK), cols]
                v2 = v_ref[sub, pl.ds(ks, ATTN_TK), cols]
                zero = jnp.zeros_like(q2)
                qq = jnp.concatenate([jnp.where(first, q2, zero), jnp.where(first, zero, q2)], axis=0)
                s = lax.dot_general(qq, k2, (((1,), (1,)), ((), ())),
                                    preferred_element_type=F32)
                s = s + jnp.concatenate([bias_ref[2 * pr, var], bias_ref[2 * pr + 1, var]], axis=0)
                m = jnp.max(s, axis=-1, keepdims=True)
                p = jnp.exp(s - m)
                den = jnp.sum(p, axis=-1, keepdims=True)
                pv = jnp.dot(p.astype(BF16), v2, preferred_element_type=F32)
                on = pv / den
                lse = m + jnp.log(den)
                o_ref[sub, pl.ds(q0, ATTN_TQ), cols] = jnp.where(
                    first, on[:ATTN_TQ], on[ATTN_TQ:]).astype(BF16)
                lse_ref[sub, pl.ds(q0, ATTN_TQ), cols] = jnp.where(
                    first, jnp.broadcast_to(lse[:ATTN_TQ], (ATTN_TQ, LANES)),
                    jnp.broadcast_to(lse[ATTN_TQ:], (ATTN_TQ, LANES)))
                return carry

            lax.fori_loop(0, nb, body, 0)


def _attention(qkv, bias, name):
    n_seq, seq, _ = qkv.shape
    assert seq >= 2 * ATTN_TQ and seq % ATTN_TQ == 0
    if seq >= 4096:
        n_sub, n_pairs = 1, 1
    else:
        n_pairs = ATTN_WIDTH // LANES
        n_sub = max(1, min(n_seq, 1024 // seq))
    cw = n_pairs * LANES
    pb = ATTN_WIDTH // cw
    kern = functools.partial(_attn_kernel, seq=seq, n_sub=n_sub, n_pairs=n_pairs)
    blk = (n_sub, seq, cw)
    return pl.pallas_call(
        kern,
        out_shape=(jax.ShapeDtypeStruct((n_seq, seq, ATTN_WIDTH), BF16),
                   jax.ShapeDtypeStruct((n_seq, seq, ATTN_WIDTH), F32)),
        grid=(n_seq // n_sub, pb),
        in_specs=[pl.BlockSpec(blk, lambda r, p: (r, 0, p)),
                  pl.BlockSpec(blk, lambda r, p: (r, 0, pb + p)),
                  pl.BlockSpec(blk, lambda r, p: (r, 0, 2 * pb + p)),
                  pl.BlockSpec((2 * n_pairs, 3, ATTN_TQ, ATTN_TK), lambda r, p: (p, 0, 0, 0))],
        out_specs=(pl.BlockSpec(blk, lambda r, p: (r, 0, p)),
                   pl.BlockSpec(blk, lambda r, p: (r, 0, p))),
        compiler_params=_cparams(("parallel", "parallel")),
        name=name,
    )(qkv, qkv, qkv, bias)


def _hgrn_mats():
    t = np.arange(HG_TILE)[:, None]
    u = np.arange(HG_TILE)[None, :]
    same = (t // HG_CHUNK) == (u // HG_CHUNK)
    fwd = same & (u <= t)
    bwd = same & (u >= t)
    return (jnp.asarray(np.concatenate([fwd, fwd], axis=1), dtype=BF16),
            jnp.asarray(np.concatenate([bwd, bwd], axis=1), dtype=BF16),
            jnp.asarray(fwd, dtype=F32), jnp.asarray(bwd, dtype=F32))


def _hgrn_kernel(q_ref, zf_ref, zb_ref, i_ref, g_ref, lbf_ref, lbb_ref, nw_ref, mf_ref, mb_ref,
                 lmask_ref, umask_ref, o_ref, acc_ref, qk_ref, qd_ref, ks_ref, kv_ref, dec_ref,
                 st_ref, *, seq):
    n_tiles = seq // HG_TILE
    n_chunks = seq // HG_CHUNK
    cpt = HG_TILE // HG_CHUNK
    mid = HG_CHUNK // 2
    nt = (((1,), (1,)), ((), ()))

    def per_chunk_rows(rows):
        return jnp.concatenate([jnp.broadcast_to(r, (HG_CHUNK, HG_DK)) for r in rows], axis=0)

    def gates(z_ref, lb_ref, m_ref, rows, q, reverse):
        lb = lb_ref[...]
        f = lb + (1.0 - lb) * jax.nn.sigmoid(z_ref[rows, :].astype(F32))
        g = jnp.log(f) * (1.0 / np.log(2.0))
        g_hi = g.astype(BF16)
        g_lo = (g - g_hi.astype(F32)).astype(BF16)
        beta = jnp.dot(m_ref[...], jnp.concatenate([g_hi, g_lo], axis=0),
                       preferred_element_type=F32)
        tot_row = 0 if reverse else HG_CHUNK - 1
        cen_row = mid if reverse else mid - 1
        tot = [beta[c * HG_CHUNK + tot_row:c * HG_CHUNK + tot_row + 1] for c in range(cpt)]
        cen = [beta[c * HG_CHUNK + cen_row:c * HG_CHUNK + cen_row + 1] for c in range(cpt)]
        d = beta - per_chunk_rows(cen)
        q_c = q * jnp.exp2(d)
        k_c = (1.0 - f) * jnp.exp2(-d)
        q_d = q_c * per_chunk_rows([jnp.exp2(c_) for c_ in cen])
        k_s = k_c * per_chunk_rows([jnp.exp2(t_ - c_) for t_, c_ in zip(tot, cen)])
        dec = [jnp.exp2(t_) for t_ in tot]
        return q_c.astype(BF16), k_c.astype(BF16), q_d.astype(BF16), k_s.astype(BF16), dec

    def phase_a1(t_idx, carry):
        rows = pl.ds(pl.multiple_of(t_idx * HG_TILE, HG_TILE), HG_TILE)
        q = q_ref[rows, :].astype(F32)
        qcf, kcf, qdf, ksf, decf = gates(zf_ref, lbf_ref, mf_ref, rows, q, False)
        qcb, kcb, qdb, ksb, decb = gates(zb_ref, lbb_ref, mb_ref, rows, q, True)
        qk_ref[rows, :] = jnp.concatenate([qcf, kcf, qcb, kcb], axis=1)
        qd_ref[rows, :] = jnp.concatenate([qdf, qdb], axis=1)
        ks_ref[rows, :] = jnp.concatenate([ksf, ksb], axis=1)
        for c in range(cpt):
            dec_ref[t_idx * cpt + c] = jnp.concatenate(
                [jnp.broadcast_to(decf[c], (8, HG_DK)), jnp.broadcast_to(decb[c], (8, HG_DK))],
                axis=1)
        return carry

    lax.fori_loop(0, n_tiles, phase_a1, 0, unroll=2)

    def phase_a2(t_idx, carry):
        rows = pl.ds(pl.multiple_of(t_idx * HG_TILE, HG_TILE), HG_TILE)
        v = i_ref[rows, :]
        a_f = lax.dot_general(qk_ref[rows, 0:HG_DK], qk_ref[rows, HG_DK:2 * HG_DK], nt,
                              preferred_element_type=F32)
        a_b = lax.dot_general(qk_ref[rows, 2 * HG_DK:3 * HG_DK], qk_ref[rows, 3 * HG_DK:], nt,
                              preferred_element_type=F32)
        a = (jnp.where(lmask_ref[...] > 0.0, a_f, 0.0) + jnp.where(umask_ref[...] > 0.0, a_b, 0.0))
        acc_ref[rows, :] = jnp.dot(a.astype(BF16), v, preferred_element_type=F32)
        ks = ks_ref[rows, :]
        for c in range(cpt):
            cr = slice(c * HG_CHUNK, (c + 1) * HG_CHUNK)
            kv_ref[t_idx * cpt + c] = lax.dot_general(v[cr], ks[cr], (((0,), (0,)), ((), ())),
                                                      preferred_element_type=F32)
        return carry

    lax.fori_loop(0, n_tiles, phase_a2, 0, unroll=2)

    fw = slice(0, HG_DK)
    bw = slice(HG_DK, 2 * HG_DK)

    def phase_b(n, carry):
        sf, sb = carry
        m = n_chunks - 1 - n
        st_ref[n, :, fw] = sf.astype(BF16)
        sf = dec_ref[n][0:1, fw] * sf + kv_ref[n, :, fw]
        st_ref[m, :, bw] = sb.astype(BF16)
        sb = dec_ref[m][0:1, bw] * sb + kv_ref[m, :, bw]
        return sf, sb

    s0 = jnp.zeros((HG_DK, HG_DK), F32)
    lax.fori_loop(0, n_chunks, phase_b, (s0, s0), unroll=4)

    def phase_c(t_idx, carry):
        r0 = pl.multiple_of(t_idx * HG_TILE, HG_TILE)
        rows = pl.ds(r0, HG_TILE)
        inter = [lax.dot_general(qd_ref[pl.ds(r0 + c * HG_CHUNK, HG_CHUNK), :],
                                 st_ref[t_idx * cpt + c], nt, preferred_element_type=F32)
                 for c in range(cpt)]
        o = acc_ref[rows, :] + jnp.concatenate(inter, axis=0)
        ms = jnp.mean(o * o, axis=-1, keepdims=True)
        gate = g_ref[rows, :].astype(F32)
        o_ref[rows, :] = (o * lax.rsqrt(ms + EPS) * nw_ref[...] * (gate * jax.nn.sigmoid(gate))
                          ).astype(BF16)
        return carry

    lax.fori_loop(0, n_tiles, phase_c, 0, unroll=4)


def _hgrn(z_rest3, lb_f, lb_b, nw, col_q, col_zf, col_zb, col_i, col_g):
    b, seq, _ = z_rest3.shape
    mf, mb, lmask, umask = _hgrn_mats()
    n_chunks = seq // HG_CHUNK

    def zspec(col):
        return pl.BlockSpec((None, seq, HG_DK), lambda bi, h, col=col: (bi, 0, col + h))

    vec = pl.BlockSpec((1, HG_DK), lambda bi, h: (0, h))
    const = pl.BlockSpec((HG_TILE, 2 * HG_TILE), lambda bi, h: (0, 0))
    mask = pl.BlockSpec((HG_TILE, HG_TILE), lambda bi, h: (0, 0))
    return pl.pallas_call(
        functools.partial(_hgrn_kernel, seq=seq),
        out_shape=jax.ShapeDtypeStruct((b, seq, HG_HEADS * HG_DK), BF16),
        grid=(b, HG_HEADS),
        in_specs=[zspec(col_q), zspec(col_zf), zspec(col_zb), zspec(col_i), zspec(col_g),
                  vec, vec, pl.BlockSpec((1, HG_DK), lambda bi, h: (0, 0)), const, const,
                  mask, mask],
        out_specs=pl.BlockSpec((None, seq, HG_DK), lambda bi, h: (bi, 0, h)),
        scratch_shapes=[pltpu.VMEM((seq, HG_DK), F32),
                        pltpu.VMEM((seq, 4 * HG_DK), BF16),
                        pltpu.VMEM((seq, 2 * HG_DK), BF16),
                        pltpu.VMEM((seq, 2 * HG_DK), BF16),
                        pltpu.VMEM((n_chunks, HG_DK, 2 * HG_DK), F32),
                        pltpu.VMEM((n_chunks, 8, 2 * HG_DK), F32),
                        pltpu.VMEM((n_chunks, HG_DK, 2 * HG_DK), BF16)],
        compiler_params=_cparams(("parallel", "parallel")),
        name="hgrn2",
    )(z_rest3, z_rest3, z_rest3, z_rest3, z_rest3, lb_f, lb_b, nw, mf, mb, lmask, umask)


def _final_kernel(o0_ref, l0_ref, o1_ref, l1_ref, o2_ref, l2_ref, ga_ref, ob_ref, za_ref, zb_ref,
                  x_ref, wa_ref, wb_ref, wo_ref, out_ref, so1_ref, sl1_ref, so2_ref, sl2_ref,
                  *, tm, dilations):
    for src, dst, dil in ((o1_ref, so1_ref, dilations[0]), (l1_ref, sl1_ref, dilations[0]),
                          (o2_ref, so2_ref, dilations[1]), (l2_ref, sl2_ref, dilations[1])):
        for r in range(dil):
            val = src[r].astype(F32)
            for c in range(ATTN_WIDTH // LANES):
                dst[c, pl.ds(r, tm // dil, stride=dil), :] = val[:, c * LANES:(c + 1) * LANES]

    def natural(ref):
        return jnp.concatenate([ref[c] for c in range(ATTN_WIDTH // LANES)], axis=1)

    l0, l1, l2 = l0_ref[...], natural(sl1_ref), natural(sl2_ref)
    mx = jnp.maximum(jnp.maximum(l0, l1), l2)
    w0, w1, w2 = jnp.exp(l0 - mx), jnp.exp(l1 - mx), jnp.exp(l2 - mx)
    num = w0 * o0_ref[...].astype(F32) + w1 * natural(so1_ref) + w2 * natural(so2_ref)
    o_a = num / (w0 + w1 + w2)
    ga = ga_ref[...].astype(F32)
    a = (o_a * (ga * jax.nn.sigmoid(ga))).astype(BF16)
    y_a = jnp.dot(a, wa_ref[...], preferred_element_type=F32)
    y_b = jnp.dot(ob_ref[...], wb_ref[...], preferred_element_type=F32)
    merged = (jax.nn.sigmoid(za_ref[...].astype(F32)) * y_a
              + jax.nn.sigmoid(zb_ref[...].astype(F32)) * y_b)
    out_ref[...] = x_ref[...] + jnp.dot(merged.astype(BF16), wo_ref[...],
                                        preferred_element_type=F32)


def _final(o_g, lse_g, z_rest, o_b, x3, wa, wb, wo, dilations, col_ga, col_za, col_zb):
    b, seq, d = x3.shape
    tm = 512
    aw = ATTN_WIDTH

    def rows(width, col=0):
        return pl.BlockSpec((None, tm, width), lambda bi, i, col=col: (bi, i, col))

    def perm(dil):
        return pl.BlockSpec((None, dil, tm // dil, aw), lambda bi, i: (bi, 0, i, 0))

    def full(shape):
        return pl.BlockSpec(shape, lambda bi, i: (0, 0))

    d1, d2 = dilations
    return pl.pallas_call(
        functools.partial(_final_kernel, tm=tm, dilations=dilations),
        out_shape=jax.ShapeDtypeStruct((b, seq, d), F32),
        grid=(b, seq // tm),
        in_specs=[rows(aw), rows(aw), perm(d1), perm(d1), perm(d2), perm(d2),
                  rows(aw, col_ga), rows(d), rows(d, col_za), rows(d, col_zb), rows(d),
                  full(wa.shape), full(wb.shape), full(wo.shape)],
        out_specs=rows(d),
        scratch_shapes=[pltpu.VMEM((aw // LANES, tm, LANES), F32)] * 4,
        compiler_params=_cparams(("parallel", "parallel")),
        name="merge_out",
    )(o_g[0], lse_g[0], o_g[1], lse_g[1], o_g[2], lse_g[2], z_rest, o_b, z_rest, z_rest, x3,
      wa, wb, wo)


def kernel(x, norm_w, w_in, q_norm_w, k_norm_w, rel_bias, lb_fwd, lb_bwd, hg_norm_w,
           w_proj_a, w_proj_b, w_out):
    b, seq, d = x.shape
    n = b * seq
    layer = 0
    w = w_in[layer].astype(BF16)
    qkv_cols = 3 * len(ATTN_GROUPS) * ATTN_WIDTH
    dilations = tuple(dil for _, dil in ATTN_GROUPS)
    assert dilations[0] == 1

    xn_all = _norm(x.astype(F32), norm_w[layer].reshape(1, d).astype(F32), dilations[1:])
    xn = xn_all[0].reshape(n, d)

    w_rest = jnp.concatenate([w[:, qkv_cols + ATTN_WIDTH:], w[:, qkv_cols:qkv_cols + ATTN_WIDTH]],
                             axis=1)
    z_rest = _proj_rest(xn, w_rest).reshape(b, seq, -1)
    hw = HG_HEADS * HG_DK

    o_g, lse_g = [], []
    for g, dilation in enumerate(dilations):
        sub_len = seq // dilation
        wg = w[:, g * 3 * ATTN_WIDTH:(g + 1) * 3 * ATTN_WIDTH]
        nw = jnp.stack([jnp.tile(q_norm_w[layer, g].astype(F32), HEADS_PER_GROUP) * HEAD_DIM ** -0.5,
                        jnp.tile(k_norm_w[layer, g].astype(F32), HEADS_PER_GROUP)]
                       ).reshape(2, 1, ATTN_WIDTH)
        qkv = _proj_qkv(xn_all[g].reshape(n, d), wg, nw, f"proj_qkv_d{dilation}")
        bias = _attn_bias(rel_bias.astype(F32)[:, g * HEADS_PER_GROUP:(g + 1) * HEADS_PER_GROUP],
                          dilation)
        o, lse = _attention(qkv.reshape(b * dilation, sub_len, 3 * ATTN_WIDTH), bias,
                            f"attn_d{dilation}")
        shape = (b, seq, ATTN_WIDTH) if dilation == 1 else (b, dilation, sub_len, ATTN_WIDTH)
        o_g.append(o.reshape(shape))
        lse_g.append(lse.reshape(shape))

    lb_f = jnp.cumsum(jax.nn.softmax(lb_fwd.astype(F32), axis=0), axis=0)[layer].reshape(1, hw)
    lb_b = jnp.cumsum(jax.nn.softmax(lb_bwd.astype(F32), axis=0), axis=0)[layer].reshape(1, hw)
    hb = hw // LANES
    o_b = _hgrn(z_rest, lb_f, lb_b, hg_norm_w[layer].reshape(1, HG_DK).astype(F32),
                col_q=0, col_zf=hb, col_zb=2 * hb, col_i=3 * hb, col_g=4 * hb)

    out = _final(o_g, lse_g, z_rest, o_b, x.astype(F32),
                 w_proj_a[layer].astype(BF16), w_proj_b[layer].astype(BF16),
                 w_out[layer].astype(BF16), dilations[1:],
                 col_ga=7 * hw // ATTN_WIDTH, col_za=5, col_zb=6)
    return out.astype(x.dtype)
```

```python
import functools

import numpy as np
import jax
import jax.numpy as jnp
from jax import lax
from jax.experimental import pallas as pl
from jax.experimental.pallas import tpu as pltpu

F32 = jnp.float32
BF16 = jnp.bfloat16

EPS = 1e-6
LOG2E = float(np.log2(np.e))
NEG_INF = -1e30
ATTN_GROUPS = ((128, 1), (512, 4), (2048, 16))
HEAD_DIM = 64
HEADS_PER_GROUP = 8
ATTN_WIDTH = HEADS_PER_GROUP * HEAD_DIM
NUM_BUCKETS = 32
REL_MAX_DISTANCE = 1024
HG_HEADS = 8
HG_DK = 128

LANES = 128
VMEM_LIMIT = 52 * 1024 * 1024

ATTN_TQ = 128
ATTN_SIDE = 64
ATTN_TK = ATTN_TQ + 2 * ATTN_SIDE

HG_CHUNK = 64
HG_TILE = 256


def _cparams(sem):
    return pltpu.CompilerParams(dimension_semantics=sem, vmem_limit_bytes=VMEM_LIMIT)


def _norm_kernel(x_ref, w_ref, o_ref, *rest, tm, dilations):
    perm_refs, y_ref = rest[:-1], rest[-1]
    x = x_ref[...]
    ms = jnp.mean(x * x, axis=-1, keepdims=True)
    y = x * lax.rsqrt(ms + EPS) * w_ref[...]
    o_ref[...] = y.astype(BF16)
    n_lane_blocks = y.shape[1] // LANES
    for c in range(n_lane_blocks):
        y_ref[c] = y[:, c * LANES:(c + 1) * LANES]
    for p_ref, dil in zip(perm_refs, dilations):
        for r in range(dil):
            for c in range(n_lane_blocks):
                p_ref[r, :, c * LANES:(c + 1) * LANES] = (
                    y_ref[c, pl.ds(r, tm // dil, stride=dil), :].astype(BF16))


def _norm(x3, w, dilations):
    b, seq, d = x3.shape
    tm = 1024
    out_shape = [jax.ShapeDtypeStruct((b, seq, d), BF16)]
    out_specs = [pl.BlockSpec((None, tm, d), lambda bi, i: (bi, i, 0))]
    for dil in dilations:
        out_shape.append(jax.ShapeDtypeStruct((b, dil, seq // dil, d), BF16))
        out_specs.append(pl.BlockSpec((None, dil, tm // dil, d), lambda bi, i: (bi, 0, i, 0)))
    return pl.pallas_call(
        functools.partial(_norm_kernel, tm=tm, dilations=dilations),
        out_shape=out_shape,
        grid=(b, seq // tm),
        in_specs=[pl.BlockSpec((None, tm, d), lambda bi, i: (bi, i, 0)),
                  pl.BlockSpec((1, d), lambda bi, i: (0, 0))],
        out_specs=out_specs,
        scratch_shapes=[pltpu.VMEM((d // LANES, tm, LANES), F32)],
        compiler_params=_cparams(("parallel", "parallel")),
        name="rmsnorm",
    )(x3, w)


def _proj_kernel(x_ref, w_ref, o_ref):
    o_ref[...] = jnp.dot(x_ref[...], w_ref[...], preferred_element_type=F32).astype(o_ref.dtype)


def _proj_rest(xn, w):
    n, d = xn.shape
    c = w.shape[1]
    tm, tn = 1024, 1536
    return pl.pallas_call(
        _proj_kernel,
        out_shape=jax.ShapeDtypeStruct((n, c), BF16),
        grid=(n // tm, c // tn),
        in_specs=[pl.BlockSpec((tm, d), lambda i, j: (i, 0)),
                  pl.BlockSpec((d, tn), lambda i, j: (0, j))],
        out_specs=pl.BlockSpec((tm, tn), lambda i, j: (i, j)),
        compiler_params=_cparams(("parallel", "arbitrary")),
        name="proj_rest",
    )(xn, w)


def _proj_qkv_kernel(x_ref, w_ref, nw_ref, seg_ref, o_ref):
    j = pl.program_id(1)
    z = jnp.dot(x_ref[...], w_ref[...], preferred_element_type=F32)

    @pl.when(j < 2)
    def _():
        ms = jnp.dot((z * z).astype(BF16), seg_ref[...], preferred_element_type=F32)
        o_ref[...] = (z * lax.rsqrt(ms + EPS) * nw_ref[...]).astype(BF16)

    @pl.when(j == 2)
    def _():
        o_ref[...] = z.astype(BF16)


def _proj_qkv(xn2, w, nw, name):
    n, d = xn2.shape
    tm, tn = 1024, ATTN_WIDTH
    head = np.arange(tn) // HEAD_DIM
    seg = jnp.asarray((head[:, None] == head[None, :]) / HEAD_DIM, dtype=BF16)
    return pl.pallas_call(
        _proj_qkv_kernel,
        out_shape=jax.ShapeDtypeStruct((n, 3 * tn), BF16),
        grid=(n // tm, 3),
        in_specs=[pl.BlockSpec((tm, d), lambda i, j: (i, 0)),
                  pl.BlockSpec((d, tn), lambda i, j: (0, j)),
                  pl.BlockSpec((None, 1, tn), lambda i, j: (jnp.minimum(j, 1), 0, 0)),
                  pl.BlockSpec((tn, tn), lambda i, j: (0, 0))],
        out_specs=pl.BlockSpec((tm, tn), lambda i, j: (i, j)),
        compiler_params=_cparams(("parallel", "arbitrary")),
        name=name,
    )(xn2, w, nw, seg)


def _t5_bucket(rel):
    half = NUM_BUCKETS // 2
    max_exact = half // 2
    n = np.abs(rel)
    large = max_exact + (np.log(np.maximum(n, 1) / max_exact)
                         / np.log(REL_MAX_DISTANCE / max_exact) * (half - max_exact)).astype(np.int32)
    large = np.minimum(large, half - 1)
    return np.where(rel > 0, half, 0) + np.where(n < max_exact, n, large)


def _attn_bias(bias_tab, dilation):
    t = np.arange(ATTN_TQ)[:, None]
    j = np.arange(ATTN_TK)[None, :]
    rel = np.stack([j - off - t for off in (0, ATTN_SIDE, 2 * ATTN_SIDE)])
    valid = (np.abs(rel) <= ATTN_SIDE).reshape(-1)
    bucket = _t5_bucket(rel * dilation).reshape(-1).astype(np.int32)
    onehot = (jnp.asarray(bucket)[None, :] == jnp.arange(NUM_BUCKETS, dtype=jnp.int32)[:, None])
    vals = jnp.dot(bias_tab.T, onehot.astype(F32), precision=lax.Precision.HIGHEST)
    vals = jnp.where(jnp.asarray(valid)[None, :], vals, NEG_INF)
    return vals.reshape(HEADS_PER_GROUP, 3, ATTN_TQ, ATTN_TK)


def _attn_kernel(q_ref, k_ref, v_ref, bias_ref, o_ref, lse_ref, *, seq, n_sub, n_pairs, unroll):
    nb = seq // ATTN_TQ
    lane = lax.broadcasted_iota(jnp.int32, (ATTN_TQ, LANES), 1)
    first = lane < HEAD_DIM
    nt = (((1,), (1,)), ((), ()))
    for sub in range(n_sub):

        def body(i, carry, sub=sub):
            q0 = pl.multiple_of(i * ATTN_TQ, ATTN_TQ)
            ks = pl.multiple_of(jnp.clip(i * ATTN_TQ - ATTN_SIDE, 0, seq - ATTN_TK), ATTN_SIDE)
            var = jnp.where(i == 0, 0, jnp.where(i == nb - 1, 2, 1))
            for pr in range(n_pairs):
                cols = slice(pr * LANES, (pr + 1) * LANES)
                q2 = q_ref[sub, pl.ds(q0, ATTN_TQ), cols]
                k2 = k_ref[sub, pl.ds(ks, ATTN_TK), cols]
                v2 = v_ref[sub, pl.ds(ks, ATTN_TK), cols]
                zero = jnp.zeros_like(q2)
                qq = jnp.concatenate([jnp.where(first, q2, zero), jnp.where(first, zero, q2)], axis=0)
                s = lax.dot_general(qq, k2, nt, preferred_element_type=F32)
                s = s + jnp.concatenate([bias_ref[2 * pr, var], bias_ref[2 * pr + 1, var]], axis=0)
                m = jnp.max(s, axis=-1, keepdims=True)
                p = jnp.exp2(s - m)
                den = jnp.sum(p, axis=-1, keepdims=True)
                pv = jnp.dot(p.astype(BF16), v2, preferred_element_type=F32)
                on = pv * (1.0 / den)
                lse = m + jnp.log(den) * (1.0 / np.log(2.0))
                o_ref[sub, pl.ds(q0, ATTN_TQ), cols] = jnp.where(
                    first, on[:ATTN_TQ], on[ATTN_TQ:]).astype(BF16)
                lse_ref[sub, pl.ds(q0, ATTN_TQ), cols] = jnp.where(
                    first, jnp.broadcast_to(lse[:ATTN_TQ], (ATTN_TQ, LANES)),
                    jnp.broadcast_to(lse[ATTN_TQ:], (ATTN_TQ, LANES)))
            return carry

        lax.fori_loop(0, nb, body, 0, unroll=unroll)


def _attention(qkv, bias, name):
    n_seq, seq, _ = qkv.shape
    assert seq >= 2 * ATTN_TQ and seq % ATTN_TQ == 0
    if seq >= 4096:
        n_sub, n_pairs, unroll = 1, 2, 2
    else:
        n_pairs, unroll = ATTN_WIDTH // LANES, 1
        n_sub = max(1, min(n_seq, 1024 // seq))
    cw = n_pairs * LANES
    pb = ATTN_WIDTH // cw
    kern = functools.partial(_attn_kernel, seq=seq, n_sub=n_sub, n_pairs=n_pairs, unroll=unroll)
    blk = (n_sub, seq, cw)
    return pl.pallas_call(
        kern,
        out_shape=(jax.ShapeDtypeStruct((n_seq, seq, ATTN_WIDTH), BF16),
                   jax.ShapeDtypeStruct((n_seq, seq, ATTN_WIDTH), F32)),
        grid=(n_seq // n_sub, pb),
        in_specs=[pl.BlockSpec(blk, lambda r, p: (r, 0, p)),
                  pl.BlockSpec(blk, lambda r, p: (r, 0, pb + p)),
                  pl.BlockSpec(blk, lambda r, p: (r, 0, 2 * pb + p)),
                  pl.BlockSpec((2 * n_pairs, 3, ATTN_TQ, ATTN_TK), lambda r, p: (p, 0, 0, 0))],
        out_specs=(pl.BlockSpec(blk, lambda r, p: (r, 0, p)),
                   pl.BlockSpec(blk, lambda r, p: (r, 0, p))),
        compiler_params=_cparams(("parallel", "parallel")),
        name=name,
    )(qkv, qkv, qkv, bias)


def _hgrn_mats():
    t = np.arange(HG_TILE)[:, None]
    u = np.arange(HG_TILE)[None, :]
    same = (t // HG_CHUNK) == (u // HG_CHUNK)
    fwd = same & (u <= t)
    bwd = same & (u >= t)
    return (jnp.asarray(np.concatenate([fwd, fwd], axis=1), dtype=BF16),
            jnp.asarray(np.concatenate([bwd, bwd], axis=1), dtype=BF16),
            jnp.asarray(fwd, dtype=F32), jnp.asarray(bwd, dtype=F32))


def _hgrn_kernel(q_ref, zf_ref, zb_ref, i_ref, g_ref, lbf_ref, lbb_ref, nw_ref, mf_ref, mb_ref,
                 lmask_ref, umask_ref, o_ref, acc_ref, qk_ref, qd_ref, ks_ref, kv_ref, dec_ref,
                 st_ref, *, seq):
    n_tiles = seq // HG_TILE
    n_chunks = seq // HG_CHUNK
    cpt = HG_TILE // HG_CHUNK
    mid = HG_CHUNK // 2
    nt = (((1,), (1,)), ((), ()))

    def per_chunk_rows(rows):
        return jnp.concatenate([jnp.broadcast_to(r, (HG_CHUNK, HG_DK)) for r in rows], axis=0)

    def gates(z_ref, lb_ref, m_ref, rows, q, reverse):
        lb = lb_ref[...]
        f = lb + (1.0 - lb) * jax.nn.sigmoid(z_ref[rows, :].astype(F32))
        g = jnp.log(f) * (1.0 / np.log(2.0))
        g_hi = g.astype(BF16)
        g_lo = (g - g_hi.astype(F32)).astype(BF16)
        beta = jnp.dot(m_ref[...], jnp.concatenate([g_hi, g_lo], axis=0),
                       preferred_element_type=F32)
        tot_row = 0 if reverse else HG_CHUNK - 1
        cen_row = mid if reverse else mid - 1
        tot = [beta[c * HG_CHUNK + tot_row:c * HG_CHUNK + tot_row + 1] for c in range(cpt)]
        cen = [beta[c * HG_CHUNK + cen_row:c * HG_CHUNK + cen_row + 1] for c in range(cpt)]
        d = beta - per_chunk_rows(cen)
        q_c = q * jnp.exp2(d)
        k_c = (1.0 - f) * jnp.exp2(-d)
        q_d = q_c * per_chunk_rows([jnp.exp2(c_) for c_ in cen])
        k_s = k_c * per_chunk_rows([jnp.exp2(t_ - c_) for t_, c_ in zip(tot, cen)])
        dec = [jnp.exp2(t_) for t_ in tot]
        return q_c.astype(BF16), k_c.astype(BF16), q_d.astype(BF16), k_s.astype(BF16), dec

    def phase_a1(t_idx, carry):
        rows = pl.ds(pl.multiple_of(t_idx * HG_TILE, HG_TILE), HG_TILE)
        q = q_ref[rows, :].astype(F32)
        qcf, kcf, qdf, ksf, decf = gates(zf_ref, lbf_ref, mf_ref, rows, q, False)
        qcb, kcb, qdb, ksb, decb = gates(zb_ref, lbb_ref, mb_ref, rows, q, True)
        qk_ref[rows, :] = jnp.concatenate([qcf, kcf, qcb, kcb], axis=1)
        qd_ref[rows, :] = jnp.concatenate([qdf, qdb], axis=1)
        ks_ref[rows, :] = jnp.concatenate([ksf, ksb], axis=1)
        for c in range(cpt):
            dec_ref[t_idx * cpt + c] = jnp.concatenate(
                [jnp.broadcast_to(decf[c], (8, HG_DK)), jnp.broadcast_to(decb[c], (8, HG_DK))],
                axis=1)
        return carry

    lax.fori_loop(0, n_tiles, phase_a1, 0, unroll=2)

    def phase_a2(t_idx, carry):
        rows = pl.ds(pl.multiple_of(t_idx * HG_TILE, HG_TILE), HG_TILE)
        v = i_ref[rows, :]
        a_f = lax.dot_general(qk_ref[rows, 0:HG_DK], qk_ref[rows, HG_DK:2 * HG_DK], nt,
                              preferred_element_type=F32)
        a_b = lax.dot_general(qk_ref[rows, 2 * HG_DK:3 * HG_DK], qk_ref[rows, 3 * HG_DK:], nt,
                              preferred_element_type=F32)
        a = (jnp.where(lmask_ref[...] > 0.0, a_f, 0.0) + jnp.where(umask_ref[...] > 0.0, a_b, 0.0))
        acc_ref[rows, :] = jnp.dot(a.astype(BF16), v, preferred_element_type=F32)
        ks = ks_ref[rows, :]
        for c in range(cpt):
            cr = slice(c * HG_CHUNK, (c + 1) * HG_CHUNK)
            kv_ref[t_idx * cpt + c] = lax.dot_general(v[cr], ks[cr], (((0,), (0,)), ((), ())),
                                                      preferred_element_type=F32)
        return carry

    lax.fori_loop(0, n_tiles, phase_a2, 0, unroll=2)

    fw = slice(0, HG_DK)
    bw = slice(HG_DK, 2 * HG_DK)

    def phase_b(n, carry):
        sf, sb = carry
        m = n_chunks - 1 - n
        st_ref[n, :, fw] = sf.astype(BF16)
        sf = dec_ref[n][0:1, fw] * sf + kv_ref[n, :, fw]
        st_ref[m, :, bw] = sb.astype(BF16)
        sb = dec_ref[m][0:1, bw] * sb + kv_ref[m, :, bw]
        return sf, sb

    s0 = jnp.zeros((HG_DK, HG_DK), F32)
    lax.fori_loop(0, n_chunks, phase_b, (s0, s0), unroll=4)

    def phase_c(t_idx, carry):
        r0 = pl.multiple_of(t_idx * HG_TILE, HG_TILE)
        rows = pl.ds(r0, HG_TILE)
        inter = [lax.dot_general(qd_ref[pl.ds(r0 + c * HG_CHUNK, HG_CHUNK), :],
                                 st_ref[t_idx * cpt + c], nt, preferred_element_type=F32)
                 for c in range(cpt)]
        o = acc_ref[rows, :] + jnp.concatenate(inter, axis=0)
        ms = jnp.mean(o * o, axis=-1, keepdims=True)
        gate = g_ref[rows, :].astype(F32)
        o_ref[rows, :] = (o * lax.rsqrt(ms + EPS) * nw_ref[...] * (gate * jax.nn.sigmoid(gate))
                          ).astype(BF16)
        return carry

    lax.fori_loop(0, n_tiles, phase_c, 0, unroll=4)


def _hgrn(z_rest3, lb_f, lb_b, nw, col_q, col_zf, col_zb, col_i, col_g):
    b, seq, _ = z_rest3.shape
    mf, mb, lmask, umask = _hgrn_mats()
    n_chunks = seq // HG_CHUNK

    def zspec(col):
        return pl.BlockSpec((None, seq, HG_DK), lambda bi, h, col=col: (bi, 0, col + h))

    vec = pl.BlockSpec((1, HG_DK), lambda bi, h: (0, h))
    const = pl.BlockSpec((HG_TILE, 2 * HG_TILE), lambda bi, h: (0, 0))
    mask = pl.BlockSpec((HG_TILE, HG_TILE), lambda bi, h: (0, 0))
    return pl.pallas_call(
        functools.partial(_hgrn_kernel, seq=seq),
        out_shape=jax.ShapeDtypeStruct((b, seq, HG_HEADS * HG_DK), BF16),
        grid=(b, HG_HEADS),
        in_specs=[zspec(col_q), zspec(col_zf), zspec(col_zb), zspec(col_i), zspec(col_g),
                  vec, vec, pl.BlockSpec((1, HG_DK), lambda bi, h: (0, 0)), const, const,
                  mask, mask],
        out_specs=pl.BlockSpec((None, seq, HG_DK), lambda bi, h: (bi, 0, h)),
        scratch_shapes=[pltpu.VMEM((seq, HG_DK), F32),
                        pltpu.VMEM((seq, 4 * HG_DK), BF16),
                        pltpu.VMEM((seq, 2 * HG_DK), BF16),
                        pltpu.VMEM((seq, 2 * HG_DK), BF16),
                        pltpu.VMEM((n_chunks, HG_DK, 2 * HG_DK), F32),
                        pltpu.VMEM((n_chunks, 8, 2 * HG_DK), F32),
                        pltpu.VMEM((n_chunks, HG_DK, 2 * HG_DK), BF16)],
        compiler_params=_cparams(("parallel", "parallel")),
        name="hgrn2",
    )(z_rest3, z_rest3, z_rest3, z_rest3, z_rest3, lb_f, lb_b, nw, mf, mb, lmask, umask)


def _final_kernel(o0_ref, l0_ref, o1_ref, l1_ref, o2_ref, l2_ref, ga_ref, ob_ref, za_ref, zb_ref,
                  x_ref, wa_ref, wb_ref, wo_ref, out_ref, so1_ref, sl1_ref, so2_ref, sl2_ref,
                  *, tm, dilations):
    for src, dst, dil in ((o1_ref, so1_ref, dilations[0]), (l1_ref, sl1_ref, dilations[0]),
                          (o2_ref, so2_ref, dilations[1]), (l2_ref, sl2_ref, dilations[1])):
        for r in range(dil):
            val = src[r].astype(F32)
            for c in range(ATTN_WIDTH // LANES):
                dst[c, pl.ds(r, tm // dil, stride=dil), :] = val[:, c * LANES:(c + 1) * LANES]

    def natural(ref):
        return jnp.concatenate([ref[c] for c in range(ATTN_WIDTH // LANES)], axis=1)

    l0, l1, l2 = l0_ref[...], natural(sl1_ref), natural(sl2_ref)
    mx = jnp.maximum(jnp.maximum(l0, l1), l2)
    w0, w1, w2 = jnp.exp2(l0 - mx), jnp.exp2(l1 - mx), jnp.exp2(l2 - mx)
    num = w0 * o0_ref[...].astype(F32) + w1 * natural(so1_ref) + w2 * natural(so2_ref)
    o_a = num / (w0 + w1 + w2)
    ga = ga_ref[...].astype(F32)
    a = (o_a * (ga * jax.nn.sigmoid(ga))).astype(BF16)
    y_a = jnp.dot(a, wa_ref[...], preferred_element_type=F32)
    y_b = jnp.dot(ob_ref[...], wb_ref[...], preferred_element_type=F32)
    merged = (jax.nn.sigmoid(za_ref[...].astype(F32)) * y_a
              + jax.nn.sigmoid(zb_ref[...].astype(F32)) * y_b)
    out_ref[...] = x_ref[...] + jnp.dot(merged.astype(BF16), wo_ref[...],
                                        preferred_element_type=F32)


def _final(o_g, lse_g, z_rest, o_b, x3, wa, wb, wo, dilations, col_ga, col_za, col_zb):
    b, seq, d = x3.shape
    tm = 512
    aw = ATTN_WIDTH

    def rows(width, col=0):
        return pl.BlockSpec((None, tm, width), lambda bi, i, col=col: (bi, i, col))

    def perm(dil):
        return pl.BlockSpec((None, dil, tm // dil, aw), lambda bi, i: (bi, 0, i, 0))

    def full(shape):
        return pl.BlockSpec(shape, lambda bi, i: (0, 0))

    d1, d2 = dilations
    return pl.pallas_call(
        functools.partial(_final_kernel, tm=tm, dilations=dilations),
        out_shape=jax.ShapeDtypeStruct((b, seq, d), F32),
        grid=(b, seq // tm),
        in_specs=[rows(aw), rows(aw), perm(d1), perm(d1), perm(d2), perm(d2),
                  rows(aw, col_ga), rows(d), rows(d, col_za), rows(d, col_zb), rows(d),
                  full(wa.shape), full(wb.shape), full(wo.shape)],
        out_specs=rows(d),
        scratch_shapes=[pltpu.VMEM((aw // LANES, tm, LANES), F32)] * 4,
        compiler_params=_cparams(("parallel", "parallel")),
        name="merge_out",
    )(o_g[0], lse_g[0], o_g[1], lse_g[1], o_g[2], lse_g[2], z_rest, o_b, z_rest, z_rest, x3,
      wa, wb, wo)


def kernel(x, norm_w, w_in, q_norm_w, k_norm_w, rel_bias, lb_fwd, lb_bwd, hg_norm_w,
           w_proj_a, w_proj_b, w_out):
    b, seq, d = x.shape
    n = b * seq
    layer = 0
    w = w_in[layer].astype(BF16)
    qkv_cols = 3 * len(ATTN_GROUPS) * ATTN_WIDTH
    dilations = tuple(dil for _, dil in ATTN_GROUPS)
    assert dilations[0] == 1

    xn_all = _norm(x.astype(F32), norm_w[layer].reshape(1, d).astype(F32), dilations[1:])
    xn = xn_all[0].reshape(n, d)

    w_rest = jnp.concatenate([w[:, qkv_cols + ATTN_WIDTH:], w[:, qkv_cols:qkv_cols + ATTN_WIDTH]],
                             axis=1)
    z_rest = _proj_rest(xn, w_rest).reshape(b, seq, -1)
    hw = HG_HEADS * HG_DK

    o_g, lse_g = [], []
    for g, dilation in enumerate(dilations):
        sub_len = seq // dilation
        wg = w[:, g * 3 * ATTN_WIDTH:(g + 1) * 3 * ATTN_WIDTH]
        q_scale = HEAD_DIM ** -0.5 * LOG2E
        nw = jnp.stack([jnp.tile(q_norm_w[layer, g].astype(F32), HEADS_PER_GROUP) * q_scale,
                        jnp.tile(k_norm_w[layer, g].astype(F32), HEADS_PER_GROUP)]
                       ).reshape(2, 1, ATTN_WIDTH)
        qkv = _proj_qkv(xn_all[g].reshape(n, d), wg, nw, f"proj_qkv_d{dilation}")
        heads = slice(g * HEADS_PER_GROUP, (g + 1) * HEADS_PER_GROUP)
        bias = _attn_bias(rel_bias.astype(F32)[:, heads] * LOG2E, dilation)
        o, lse = _attention(qkv.reshape(b * dilation, sub_len, 3 * ATTN_WIDTH), bias,
                            f"attn_d{dilation}")
        shape = (b, seq, ATTN_WIDTH) if dilation == 1 else (b, dilation, sub_len, ATTN_WIDTH)
        o_g.append(o.reshape(shape))
        lse_g.append(lse.reshape(shape))

    lb_f = jnp.cumsum(jax.nn.softmax(lb_fwd.astype(F32), axis=0), axis=0)[layer].reshape(1, hw)
    lb_b = jnp.cumsum(jax.nn.softmax(lb_bwd.astype(F32), axis=0), axis=0)[layer].reshape(1, hw)
    hb = hw // LANES
    o_b = _hgrn(z_rest, lb_f, lb_b, hg_norm_w[layer].reshape(1, HG_DK).astype(F32),
                col_q=0, col_zf=hb, col_zb=2 * hb, col_i=3 * hb, col_g=4 * hb)

    out = _final(o_g, lse_g, z_rest, o_b, x.astype(F32),
                 w_proj_a[layer].astype(BF16), w_proj_b[layer].astype(BF16),
                 w_out[layer].astype(BF16), dilations[1:],
                 col_ga=7 * hw // ATTN_WIDTH, col_za=5, col_zb=6)
    return out.astype(x.dtype)
```

```python
import functools

import numpy as np
import jax
import jax.numpy as jnp
from jax import lax
from jax.experimental import pallas as pl
from jax.experimental.pallas import tpu as pltpu

F32 = jnp.float32
BF16 = jnp.bfloat16

EPS = 1e-6
LOG2E = float(np.log2(np.e))
NEG_INF = -1e30
ATTN_GROUPS = ((128, 1), (512, 4), (2048, 16))
HEAD_DIM = 64
HEADS_PER_GROUP = 8
ATTN_WIDTH = HEADS_PER_GROUP * HEAD_DIM
NUM_BUCKETS = 32
REL_MAX_DISTANCE = 1024
HG_HEADS = 8
HG_DK = 128

LANES = 128
VMEM_LIMIT = 52 * 1024 * 1024

ATTN_TQ = 128
ATTN_SIDE = 64
ATTN_TK = ATTN_TQ + 2 * ATTN_SIDE

HG_CHUNK = 64
HG_TILE = 256


def _cparams(sem):
    return pltpu.CompilerParams(dimension_semantics=sem, vmem_limit_bytes=VMEM_LIMIT)


def _norm_kernel(x_ref, w_ref, o_ref, *rest, tm, dilations):
    perm_refs, y_ref = rest[:-1], rest[-1]
    x = x_ref[...]
    ms = jnp.mean(x * x, axis=-1, keepdims=True)
    y = x * lax.rsqrt(ms + EPS) * w_ref[...]
    o_ref[...] = y.astype(BF16)
    n_lane_blocks = y.shape[1] // LANES
    for c in range(n_lane_blocks):
        y_ref[c] = y[:, c * LANES:(c + 1) * LANES]
    for p_ref, dil in zip(perm_refs, dilations):
        for r in range(dil):
            for c in range(n_lane_blocks):
                p_ref[r, :, c * LANES:(c + 1) * LANES] = (
                    y_ref[c, pl.ds(r, tm // dil, stride=dil), :].astype(BF16))


def _norm(x3, w, dilations):
    b, seq, d = x3.shape
    tm = 1024
    out_shape = [jax.ShapeDtypeStruct((b, seq, d), BF16)]
    out_specs = [pl.BlockSpec((None, tm, d), lambda bi, i: (bi, i, 0))]
    for dil in dilations:
        out_shape.append(jax.ShapeDtypeStruct((b, dil, seq // dil, d), BF16))
        out_specs.append(pl.BlockSpec((None, dil, tm // dil, d), lambda bi, i: (bi, 0, i, 0)))
    return pl.pallas_call(
        functools.partial(_norm_kernel, tm=tm, dilations=dilations),
        out_shape=out_shape,
        grid=(b, seq // tm),
        in_specs=[pl.BlockSpec((None, tm, d), lambda bi, i: (bi, i, 0)),
                  pl.BlockSpec((1, d), lambda bi, i: (0, 0))],
        out_specs=out_specs,
        scratch_shapes=[pltpu.VMEM((d // LANES, tm, LANES), F32)],
        compiler_params=_cparams(("parallel", "parallel")),
        name="rmsnorm",
    )(x3, w)


PROJ_SUB = 256


def _proj_kernel(x_ref, w_ref, o_ref):
    o_ref[...] = jnp.dot(x_ref[...], w_ref[...], preferred_element_type=F32).astype(o_ref.dtype)


def _proj_raw(xn, w):
    n, d = xn.shape
    c = w.shape[1]
    tm, tn = 1024, 2048
    return pl.pallas_call(
        _proj_kernel,
        out_shape=jax.ShapeDtypeStruct((n, c), BF16),
        grid=(n // tm, c // tn),
        in_specs=[pl.BlockSpec((tm, d), lambda i, j: (i, 0)),
                  pl.BlockSpec((d, tn), lambda i, j: (0, j))],
        out_specs=pl.BlockSpec((tm, tn), lambda i, j: (i, j)),
        compiler_params=_cparams(("parallel", "arbitrary")),
        name="proj_raw",
    )(xn, w)


def _proj_act_kernel(silu_ref, x_ref, w_ref, o_ref):
    is_silu = silu_ref[pl.program_id(1)] > 0
    for r in range(x_ref.shape[0] // PROJ_SUB):
        rows = slice(r * PROJ_SUB, (r + 1) * PROJ_SUB)
        z = jnp.dot(x_ref[rows, :], w_ref[...], preferred_element_type=F32)
        o_ref[rows, :] = (jax.nn.sigmoid(z) * jnp.where(is_silu, z, 1.0)).astype(BF16)


def _proj_act(xn, w, silu_tiles):
    n, d = xn.shape
    c = w.shape[1]
    tm, tn = 2048, 512
    assert len(silu_tiles) == c // tn
    return pl.pallas_call(
        _proj_act_kernel,
        out_shape=jax.ShapeDtypeStruct((n, c), BF16),
        grid_spec=pltpu.PrefetchScalarGridSpec(
            num_scalar_prefetch=1,
            grid=(n // tm, c // tn),
            in_specs=[pl.BlockSpec((tm, d), lambda i, j, s: (i, 0)),
                      pl.BlockSpec((d, tn), lambda i, j, s: (0, j))],
            out_specs=pl.BlockSpec((tm, tn), lambda i, j, s: (i, j))),
        compiler_params=_cparams(("parallel", "arbitrary")),
        name="proj_act",
    )(jnp.asarray(silu_tiles, dtype=jnp.int32), xn, w)


def _proj_qkv_kernel(x_ref, w_ref, nw_ref, seg_ref, o_ref):
    j = pl.program_id(1)
    n_sub = x_ref.shape[0] // PROJ_SUB

    @pl.when(j < 2)
    def _():
        for r in range(n_sub):
            rows = slice(r * PROJ_SUB, (r + 1) * PROJ_SUB)
            z = jnp.dot(x_ref[rows, :], w_ref[...], preferred_element_type=F32)
            ms = jnp.dot((z * z).astype(BF16), seg_ref[...], preferred_element_type=F32)
            o_ref[rows, :] = (z * lax.rsqrt(ms + EPS) * nw_ref[...]).astype(BF16)

    @pl.when(j == 2)
    def _():
        o_ref[...] = jnp.dot(x_ref[...], w_ref[...], preferred_element_type=F32).astype(BF16)


def _proj_qkv(xn2, w, nw, name):
    n, d = xn2.shape
    tm, tn = 1024, ATTN_WIDTH
    head = np.arange(tn) // HEAD_DIM
    seg = jnp.asarray((head[:, None] == head[None, :]) / HEAD_DIM, dtype=BF16)
    return pl.pallas_call(
        _proj_qkv_kernel,
        out_shape=jax.ShapeDtypeStruct((n, 3 * tn), BF16),
        grid=(n // tm, 3),
        in_specs=[pl.BlockSpec((tm, d), lambda i, j: (i, 0)),
                  pl.BlockSpec((d, tn), lambda i, j: (0, j)),
                  pl.BlockSpec((None, 1, tn), lambda i, j: (jnp.minimum(j, 1), 0, 0)),
                  pl.BlockSpec((tn, tn), lambda i, j: (0, 0))],
        out_specs=pl.BlockSpec((tm, tn), lambda i, j: (i, j)),
        compiler_params=_cparams(("parallel", "arbitrary")),
        name=name,
    )(xn2, w, nw, seg)


def _t5_bucket(rel):
    half = NUM_BUCKETS // 2
    max_exact = half // 2
    n = np.abs(rel)
    large = max_exact + (np.log(np.maximum(n, 1) / max_exact)
                         / np.log(REL_MAX_DISTANCE / max_exact) * (half - max_exact)).astype(np.int32)
    large = np.minimum(large, half - 1)
    return np.where(rel > 0, half, 0) + np.where(n < max_exact, n, large)


def _attn_bias(bias_tab, dilation):
    t = np.arange(ATTN_TQ)[:, None]
    j = np.arange(ATTN_TK)[None, :]
    rel = np.stack([j - off - t for off in (0, ATTN_SIDE, 2 * ATTN_SIDE)])
    valid = (np.abs(rel) <= ATTN_SIDE).reshape(-1)
    bucket = _t5_bucket(rel * dilation).reshape(-1).astype(np.int32)
    onehot = (jnp.asarray(bucket)[None, :] == jnp.arange(NUM_BUCKETS, dtype=jnp.int32)[:, None])
    vals = jnp.dot(bias_tab.T, onehot.astype(F32), precision=lax.Precision.HIGHEST)
    vals = jnp.where(jnp.asarray(valid)[None, :], vals, NEG_INF)
    return vals.reshape(HEADS_PER_GROUP, 3, ATTN_TQ, ATTN_TK)


def _attn_kernel(q_ref, k_ref, v_ref, bias_ref, o_ref, lse_ref, *, seq, n_sub, n_pairs, unroll):
    nb = seq // ATTN_TQ
    lane = lax.broadcasted_iota(jnp.int32, (ATTN_TQ, LANES), 1)
    first = lane < HEAD_DIM
    nt = (((1,), (1,)), ((), ()))
    for sub in range(n_sub):

        def body(i, carry, sub=sub):
            q0 = pl.multiple_of(i * ATTN_TQ, ATTN_TQ)
            ks = pl.multiple_of(jnp.clip(i * ATTN_TQ - ATTN_SIDE, 0, seq - ATTN_TK), ATTN_SIDE)
            var = jnp.where(i == 0, 0, jnp.where(i == nb - 1, 2, 1))
            for pr in range(n_pairs):
                cols = slice(pr * LANES, (pr + 1) * LANES)
                q2 = q_ref[sub, pl.ds(q0, ATTN_TQ), cols]
                k2 = k_ref[sub, pl.ds(ks, ATTN_TK), cols]
                v2 = v_ref[sub, pl.ds(ks, ATTN_TK), cols]
                zero = jnp.zeros_like(q2)
                qq = jnp.concatenate([jnp.where(first, q2, zero), jnp.where(first, zero, q2)], axis=0)
                s = lax.dot_general(qq, k2, nt, preferred_element_type=F32)
                s = s + jnp.concatenate([bias_ref[2 * pr, var], bias_ref[2 * pr + 1, var]], axis=0)
                m = jnp.max(s, axis=-1, keepdims=True)
                p = jnp.exp2(s - m)
                den = jnp.sum(p, axis=-1, keepdims=True)
                pv = jnp.dot(p.astype(BF16), v2, preferred_element_type=F32)
                on = pv * (1.0 / den)
                lse = m + jnp.log(den) * (1.0 / np.log(2.0))
                o_ref[sub, pl.ds(q0, ATTN_TQ), cols] = jnp.where(
                    first, on[:ATTN_TQ], on[ATTN_TQ:]).astype(BF16)
                lse_ref[sub, pl.ds(q0, ATTN_TQ), cols] = jnp.where(
                    first, jnp.broadcast_to(lse[:ATTN_TQ], (ATTN_TQ, LANES)),
                    jnp.broadcast_to(lse[ATTN_TQ:], (ATTN_TQ, LANES)))
            return carry

        lax.fori_loop(0, nb, body, 0, unroll=unroll)


def _attention(qkv, bias, name):
    n_seq, seq, _ = qkv.shape
    assert seq >= 2 * ATTN_TQ and seq % ATTN_TQ == 0
    if seq >= 4096:
        n_sub, n_pairs, unroll = 1, 2, 2
    else:
        n_pairs, unroll = ATTN_WIDTH // LANES, 1
        n_sub = max(1, min(n_seq, 1024 // seq))
    cw = n_pairs * LANES
    pb = ATTN_WIDTH // cw
    kern = functools.partial(_attn_kernel, seq=seq, n_sub=n_sub, n_pairs=n_pairs, unroll=unroll)
    blk = (n_sub, seq, cw)
    return pl.pallas_call(
        kern,
        out_shape=(jax.ShapeDtypeStruct((n_seq, seq, ATTN_WIDTH), BF16),
                   jax.ShapeDtypeStruct((n_seq, seq, ATTN_WIDTH), F32)),
        grid=(n_seq // n_sub, pb),
        in_specs=[pl.BlockSpec(blk, lambda r, p: (r, 0, p)),
                  pl.BlockSpec(blk, lambda r, p: (r, 0, pb + p)),
                  pl.BlockSpec(blk, lambda r, p: (r, 0, 2 * pb + p)),
                  pl.BlockSpec((2 * n_pairs, 3, ATTN_TQ, ATTN_TK), lambda r, p: (p, 0, 0, 0))],
        out_specs=(pl.BlockSpec(blk, lambda r, p: (r, 0, p)),
                   pl.BlockSpec(blk, lambda r, p: (r, 0, p))),
        compiler_params=_cparams(("parallel", "parallel")),
        name=name,
    )(qkv, qkv, qkv, bias)


def _hgrn_mats():
    t = np.arange(HG_TILE)[:, None]
    u = np.arange(HG_TILE)[None, :]
    same = (t // HG_CHUNK) == (u // HG_CHUNK)
    fwd = same & (u <= t)
    bwd = same & (u >= t)
    return (jnp.asarray(fwd, dtype=BF16), jnp.asarray(bwd, dtype=BF16),
            jnp.asarray(fwd, dtype=F32), jnp.asarray(bwd, dtype=F32))


def _hgrn_kernel(q_ref, zf_ref, zb_ref, i_ref, g_ref, lbf_ref, lbb_ref, nw_ref, mf_ref, mb_ref,
                 lmask_ref, umask_ref, o_ref, acc_ref, qk_ref, qd_ref, ks_ref, kv_ref, dec_ref,
                 st_ref, *, seq):
    n_tiles = seq // HG_TILE
    n_chunks = seq // HG_CHUNK
    cpt = HG_TILE // HG_CHUNK
    mid = HG_CHUNK // 2
    nt = (((1,), (1,)), ((), ()))

    def per_chunk_rows(rows):
        return jnp.concatenate([jnp.broadcast_to(r, (HG_CHUNK, HG_DK)) for r in rows], axis=0)

    def gates(z_ref, lb_ref, m_ref, rows, q, reverse):
        lb = lb_ref[...]
        f = lb + (1.0 - lb) * jax.nn.sigmoid(z_ref[rows, :].astype(F32))
        g = jnp.log(f) * (1.0 / np.log(2.0))
        g_hi = g.astype(BF16)
        g_lo = (g - g_hi.astype(F32)).astype(BF16)
        sums = jnp.dot(m_ref[...], jnp.concatenate([g_hi, g_lo], axis=1),
                       preferred_element_type=F32)
        beta = sums[:, :HG_DK] + sums[:, HG_DK:]
        tot_row = 0 if reverse else HG_CHUNK - 1
        cen_row = mid if reverse else mid - 1
        tot = [beta[c * HG_CHUNK + tot_row:c * HG_CHUNK + tot_row + 1] for c in range(cpt)]
        cen = [beta[c * HG_CHUNK + cen_row:c * HG_CHUNK + cen_row + 1] for c in range(cpt)]
        d = beta - per_chunk_rows(cen)
        q_c = q * jnp.exp2(d)
        k_c = (1.0 - f) * jnp.exp2(-d)
        q_d = q_c * per_chunk_rows([jnp.exp2(c_) for c_ in cen])
        k_s = k_c * per_chunk_rows([jnp.exp2(t_ - c_) for t_, c_ in zip(tot, cen)])
        dec = [jnp.exp2(t_) for t_ in tot]
        return q_c.astype(BF16), k_c.astype(BF16), q_d.astype(BF16), k_s.astype(BF16), dec

    def phase_a1(t_idx, slot):
        rows = pl.ds(pl.multiple_of(t_idx * HG_TILE, HG_TILE), HG_TILE)
        q = q_ref[rows, :].astype(F32)
        qcf, kcf, qdf, ksf, decf = gates(zf_ref, lbf_ref, mf_ref, rows, q, False)
        qcb, kcb, qdb, ksb, decb = gates(zb_ref, lbb_ref, mb_ref, rows, q, True)
        qk_ref[slot] = jnp.concatenate([qcf, kcf, qcb, kcb], axis=1)
        ks_ref[slot] = jnp.concatenate([ksf, ksb], axis=1)
        qd_ref[rows, :] = jnp.concatenate([qdf, qdb], axis=1)
        for c in range(cpt):
            dec_ref[t_idx * cpt + c] = jnp.concatenate(
                [jnp.broadcast_to(decf[c], (8, HG_DK)), jnp.broadcast_to(decb[c], (8, HG_DK))],
                axis=1)

    def phase_a2(t_idx, slot):
        rows = pl.ds(pl.multiple_of(t_idx * HG_TILE, HG_TILE), HG_TILE)
        v = i_ref[rows, :]
        a_f = lax.dot_general(qk_ref[slot, :, 0:HG_DK], qk_ref[slot, :, HG_DK:2 * HG_DK], nt,
                              preferred_element_type=F32)
        a_b = lax.dot_general(qk_ref[slot, :, 2 * HG_DK:3 * HG_DK], qk_ref[slot, :, 3 * HG_DK:], nt,
                              preferred_element_type=F32)
        a = (jnp.where(lmask_ref[...] > 0.0, a_f, 0.0) + jnp.where(umask_ref[...] > 0.0, a_b, 0.0))
        acc_ref[rows, :] = jnp.dot(a.astype(BF16), v, preferred_element_type=F32)
        ks = ks_ref[slot]
        for c in range(cpt):
            cr = slice(c * HG_CHUNK, (c + 1) * HG_CHUNK)
            kv_ref[t_idx * cpt + c] = lax.dot_general(v[cr], ks[cr], (((0,), (0,)), ((), ())),
                                                      preferred_element_type=F32)

    def phase_a(j, carry):
        phase_a1(2 * j + 1, 1)
        phase_a2(2 * j, 0)
        phase_a1(2 * j + 2, 0)
        phase_a2(2 * j + 1, 1)
        return carry

    phase_a1(0, 0)
    lax.fori_loop(0, n_tiles // 2 - 1, phase_a, 0)
    phase_a1(n_tiles - 1, 1)
    phase_a2(n_tiles - 2, 0)
    phase_a2(n_tiles - 1, 1)

    fw = slice(0, HG_DK)
    bw = slice(HG_DK, 2 * HG_DK)

    def phase_b(n, carry):
        sf, sb = carry
        m = n_chunks - 1 - n
        st_ref[n, :, fw] = sf.astype(BF16)
        sf = dec_ref[n][0:1, fw] * sf + kv_ref[n, :, fw]
        st_ref[m, :, bw] = sb.astype(BF16)
        sb = dec_ref[m][0:1, bw] * sb + kv_ref[m, :, bw]
        return sf, sb

    s0 = jnp.zeros((HG_DK, HG_DK), F32)
    lax.fori_loop(0, n_chunks, phase_b, (s0, s0), unroll=4)

    def phase_c(t_idx, carry):
        r0 = pl.multiple_of(t_idx * HG_TILE, HG_TILE)
        rows = pl.ds(r0, HG_TILE)
        inter = [lax.dot_general(qd_ref[pl.ds(r0 + c * HG_CHUNK, HG_CHUNK), :],
                                 st_ref[t_idx * cpt + c], nt, preferred_element_type=F32)
                 for c in range(cpt)]
        o = acc_ref[rows, :] + jnp.concatenate(inter, axis=0)
        ms = jnp.mean(o * o, axis=-1, keepdims=True)
        o_ref[rows, :] = (o * lax.rsqrt(ms + EPS) * nw_ref[...] * g_ref[rows, :].astype(F32)
                          ).astype(BF16)
        return carry

    lax.fori_loop(0, n_tiles, phase_c, 0, unroll=4)


def _hgrn(z_raw, z_act, lb_f, lb_b, nw, col_q, col_zf, col_zb, col_i, col_g):
    b, seq, _ = z_raw.shape
    mf, mb, lmask, umask = _hgrn_mats()
    n_chunks = seq // HG_CHUNK

    def zspec(col):
        return pl.BlockSpec((None, seq, HG_DK), lambda bi, h, col=col: (bi, 0, col + h))

    vec = pl.BlockSpec((1, HG_DK), lambda bi, h: (0, h))
    const = pl.BlockSpec((HG_TILE, HG_TILE), lambda bi, h: (0, 0))
    mask = const
    return pl.pallas_call(
        functools.partial(_hgrn_kernel, seq=seq),
        out_shape=jax.ShapeDtypeStruct((b, seq, HG_HEADS * HG_DK), BF16),
        grid=(b, HG_HEADS),
        in_specs=[zspec(col_q), zspec(col_zf), zspec(col_zb), zspec(col_i), zspec(col_g),
                  vec, vec, pl.BlockSpec((1, HG_DK), lambda bi, h: (0, 0)), const, const,
                  mask, mask],
        out_specs=pl.BlockSpec((None, seq, HG_DK), lambda bi, h: (bi, 0, h)),
        scratch_shapes=[pltpu.VMEM((seq, HG_DK), F32),
                        pltpu.VMEM((2, HG_TILE, 4 * HG_DK), BF16),
                        pltpu.VMEM((seq, 2 * HG_DK), BF16),
                        pltpu.VMEM((2, HG_TILE, 2 * HG_DK), BF16),
                        pltpu.VMEM((n_chunks, HG_DK, 2 * HG_DK), F32),
                        pltpu.VMEM((n_chunks, 8, 2 * HG_DK), F32),
                        pltpu.VMEM((n_chunks, HG_DK, 2 * HG_DK), BF16)],
        compiler_params=_cparams(("parallel", "parallel")),
        name="hgrn2",
    )(z_raw, z_raw, z_raw, z_raw, z_act, lb_f, lb_b, nw, mf, mb, lmask, umask)


def _final_kernel(o0_ref, l0_ref, o1_ref, l1_ref, o2_ref, l2_ref, ga_ref, ob_ref, za_ref, zb_ref,
                  x_ref, wa_ref, wb_ref, wo_ref, out_ref, so1_ref, sl1_ref, so2_ref, sl2_ref,
                  *, tm, dilations):
    for src, dst, dil in ((o1_ref, so1_ref, dilations[0]), (l1_ref, sl1_ref, dilations[0]),
                          (o2_ref, so2_ref, dilations[1]), (l2_ref, sl2_ref, dilations[1])):
        for r in range(dil):
            val = src[r].astype(F32)
            for c in range(ATTN_WIDTH // LANES):
                dst[c, pl.ds(r, tm // dil, stride=dil), :] = val[:, c * LANES:(c + 1) * LANES]

    for r in range(tm // PROJ_SUB):
        rows = slice(r * PROJ_SUB, (r + 1) * PROJ_SUB)

        def natural(ref):
            return jnp.concatenate([ref[c, rows, :] for c in range(ATTN_WIDTH // LANES)], axis=1)

        l0, l1, l2 = l0_ref[rows, :], natural(sl1_ref), natural(sl2_ref)
        mx = jnp.maximum(jnp.maximum(l0, l1), l2)
        w0, w1, w2 = jnp.exp2(l0 - mx), jnp.exp2(l1 - mx), jnp.exp2(l2 - mx)
        num = w0 * o0_ref[rows, :].astype(F32) + w1 * natural(so1_ref) + w2 * natural(so2_ref)
        o_a = num / (w0 + w1 + w2)
        a = (o_a * ga_ref[rows, :].astype(F32)).astype(BF16)
        y_a = jnp.dot(a, wa_ref[...], preferred_element_type=F32)
        y_b = jnp.dot(ob_ref[rows, :], wb_ref[...], preferred_element_type=F32)
        merged = (za_ref[rows, :].astype(F32) * y_a
                  + zb_ref[rows, :].astype(F32) * y_b)
        out_ref[rows, :] = x_ref[rows, :] + jnp.dot(merged.astype(BF16), wo_ref[...],
                                                    preferred_element_type=F32)


def _final(o_g, lse_g, z_rest, o_b, x3, wa, wb, wo, dilations, col_ga, col_za, col_zb):
    b, seq, d = x3.shape
    tm = 512
    aw = ATTN_WIDTH

    def rows(width, col=0):
        return pl.BlockSpec((None, tm, width), lambda bi, i, col=col: (bi, i, col))

    def perm(dil):
        return pl.BlockSpec((None, dil, tm // dil, aw), lambda bi, i: (bi, 0, i, 0))

    def full(shape):
        return pl.BlockSpec(shape, lambda bi, i: (0, 0))

    d1, d2 = dilations
    return pl.pallas_call(
        functools.partial(_final_kernel, tm=tm, dilations=dilations),
        out_shape=jax.ShapeDtypeStruct((b, seq, d), F32),
        grid=(b, seq // tm),
        in_specs=[rows(aw), rows(aw), perm(d1), perm(d1), perm(d2), perm(d2),
                  rows(aw, col_ga), rows(d), rows(d, col_za), rows(d, col_zb), rows(d),
                  full(wa.shape), full(wb.shape), full(wo.shape)],
        out_specs=rows(d),
        scratch_shapes=[pltpu.VMEM((aw // LANES, tm, LANES), F32)] * 4,
        compiler_params=_cparams(("parallel", "parallel")),
        name="merge_out",
    )(o_g[0], lse_g[0], o_g[1], lse_g[1], o_g[2], lse_g[2], z_rest, o_b, z_rest, z_rest, x3,
      wa, wb, wo)


def kernel(x, norm_w, w_in, q_norm_w, k_norm_w, rel_bias, lb_fwd, lb_bwd, hg_norm_w,
           w_proj_a, w_proj_b, w_out):
    b, seq, d = x.shape
    n = b * seq
    layer = 0
    w = w_in[layer].astype(BF16)
    qkv_cols = 3 * len(ATTN_GROUPS) * ATTN_WIDTH
    dilations = tuple(dil for _, dil in ATTN_GROUPS)
    assert dilations[0] == 1

    xn_all = _norm(x.astype(F32), norm_w[layer].reshape(1, d).astype(F32), dilations[1:])
    xn = xn_all[0].reshape(n, d)

    hw = HG_HEADS * HG_DK
    c_ga = qkv_cols
    c_qb = c_ga + ATTN_WIDTH
    c_gb = c_qb + 4 * hw
    z_raw = _proj_raw(xn, w[:, c_qb:c_gb]).reshape(b, seq, 4 * hw)
    w_act = jnp.concatenate([w[:, c_gb:], w[:, c_ga:c_qb]], axis=1)
    tiles = hw // ATTN_WIDTH
    z_act = _proj_act(xn, w_act, (1,) * tiles + (0,) * (2 * tiles) + (1,)).reshape(b, seq, -1)

    o_g, lse_g = [], []
    for g, dilation in enumerate(dilations):
        sub_len = seq // dilation
        wg = w[:, g * 3 * ATTN_WIDTH:(g + 1) * 3 * ATTN_WIDTH]
        q_scale = HEAD_DIM ** -0.5 * LOG2E
        nw = jnp.stack([jnp.tile(q_norm_w[layer, g].astype(F32), HEADS_PER_GROUP) * q_scale,
                        jnp.tile(k_norm_w[layer, g].astype(F32), HEADS_PER_GROUP)]
                       ).reshape(2, 1, ATTN_WIDTH)
        qkv = _proj_qkv(xn_all[g].reshape(n, d), wg, nw, f"proj_qkv_d{dilation}")
        heads = slice(g * HEADS_PER_GROUP, (g + 1) * HEADS_PER_GROUP)
        bias = _attn_bias(rel_bias.astype(F32)[:, heads] * LOG2E, dilation)
        o, lse = _attention(qkv.reshape(b * dilation, sub_len, 3 * ATTN_WIDTH), bias,
                            f"attn_d{dilation}")
        shape = (b, seq, ATTN_WIDTH) if dilation == 1 else (b, dilation, sub_len, ATTN_WIDTH)
        o_g.append(o.reshape(shape))
        lse_g.append(lse.reshape(shape))

    lb_f = jnp.cumsum(jax.nn.softmax(lb_fwd.astype(F32), axis=0), axis=0)[layer].reshape(1, hw)
    lb_b = jnp.cumsum(jax.nn.softmax(lb_bwd.astype(F32), axis=0), axis=0)[layer].reshape(1, hw)
    hb = hw // LANES
    o_b = _hgrn(z_raw, z_act, lb_f, lb_b, hg_norm_w[layer].reshape(1, HG_DK).astype(F32),
                col_q=0, col_zf=hb, col_zb=2 * hb, col_i=3 * hb, col_g=0)

    out = _final(o_g, lse_g, z_act, o_b, x.astype(F32),
                 w_proj_a[layer].astype(BF16), w_proj_b[layer].astype(BF16),
                 w_out[layer].astype(BF16), dilations[1:],
                 col_ga=3 * hw // ATTN_WIDTH, col_za=1, col_zb=2)
    return out.astype(x.dtype)
```

```python
import functools

import numpy as np
import jax
import jax.numpy as jnp
from jax import lax
from jax.experimental import pallas as pl
from jax.experimental.pallas import tpu as pltpu

F32 = jnp.float32
BF16 = jnp.bfloat16

EPS = 1e-6
LOG2E = float(np.log2(np.e))
NEG_INF = -1e30
ATTN_GROUPS = ((128, 1), (512, 4), (2048, 16))
HEAD_DIM = 64
HEADS_PER_GROUP = 8
ATTN_WIDTH = HEADS_PER_GROUP * HEAD_DIM
NUM_BUCKETS = 32
REL_MAX_DISTANCE = 1024
HG_HEADS = 8
HG_DK = 128

LANES = 128
VMEM_LIMIT = 52 * 1024 * 1024

ATTN_TQ = 128
ATTN_SIDE = 64
ATTN_TK = ATTN_TQ + 2 * ATTN_SIDE

HG_CHUNK = 64
HG_TILE = 256


def _cparams(sem):
    return pltpu.CompilerParams(dimension_semantics=sem, vmem_limit_bytes=VMEM_LIMIT)


def _norm_kernel(x_ref, w_ref, o_ref, *rest, tm, dilations):
    perm_refs, y_ref = rest[:-1], rest[-1]
    x = x_ref[...]
    ms = jnp.mean(x * x, axis=-1, keepdims=True)
    y = x * lax.rsqrt(ms + EPS) * w_ref[...]
    o_ref[...] = y.astype(BF16)
    n_lane_blocks = y.shape[1] // LANES
    for c in range(n_lane_blocks):
        y_ref[c] = y[:, c * LANES:(c + 1) * LANES]
    for p_ref, dil in zip(perm_refs, dilations):
        for r in range(dil):
            for c in range(n_lane_blocks):
                p_ref[r, :, c * LANES:(c + 1) * LANES] = (
                    y_ref[c, pl.ds(r, tm // dil, stride=dil), :].astype(BF16))


def _norm(x3, w, dilations):
    b, seq, d = x3.shape
    tm = 1024
    out_shape = [jax.ShapeDtypeStruct((b, seq, d), BF16)]
    out_specs = [pl.BlockSpec((None, tm, d), lambda bi, i: (bi, i, 0))]
    for dil in dilations:
        out_shape.append(jax.ShapeDtypeStruct((b, dil, seq // dil, d), BF16))
        out_specs.append(pl.BlockSpec((None, dil, tm // dil, d), lambda bi, i: (bi, 0, i, 0)))
    return pl.pallas_call(
        functools.partial(_norm_kernel, tm=tm, dilations=dilations),
        out_shape=out_shape,
        grid=(b, seq // tm),
        in_specs=[pl.BlockSpec((None, tm, d), lambda bi, i: (bi, i, 0)),
                  pl.BlockSpec((1, d), lambda bi, i: (0, 0))],
        out_specs=out_specs,
        scratch_shapes=[pltpu.VMEM((d // LANES, tm, LANES), F32)],
        compiler_params=_cparams(("parallel", "parallel")),
        name="rmsnorm",
    )(x3, w)


def _proj_kernel(x_ref, w_ref, o_ref):
    o_ref[...] = jnp.dot(x_ref[...], w_ref[...], preferred_element_type=F32).astype(o_ref.dtype)


def _proj_rest(xn, w):
    n, d = xn.shape
    c = w.shape[1]
    tm, tn = 1024, 1536
    return pl.pallas_call(
        _proj_kernel,
        out_shape=jax.ShapeDtypeStruct((n, c), BF16),
        grid=(n // tm, c // tn),
        in_specs=[pl.BlockSpec((tm, d), lambda i, j: (i, 0)),
                  pl.BlockSpec((d, tn), lambda i, j: (0, j))],
        out_specs=pl.BlockSpec((tm, tn), lambda i, j: (i, j)),
        compiler_params=_cparams(("parallel", "arbitrary")),
        name="proj_rest",
    )(xn, w)


def _proj_qkv_kernel(x_ref, w_ref, nw_ref, seg_ref, o_ref):
    j = pl.program_id(1)
    z = jnp.dot(x_ref[...], w_ref[...], preferred_element_type=F32)

    @pl.when(j < 2)
    def _():
        ms = jnp.dot((z * z).astype(BF16), seg_ref[...], preferred_element_type=F32)
        o_ref[...] = (z * lax.rsqrt(ms + EPS) * nw_ref[...]).astype(BF16)

    @pl.when(j == 2)
    def _():
        o_ref[...] = z.astype(BF16)


def _proj_qkv(xn2, w, nw, name):
    n, d = xn2.shape
    tm, tn = 1024, ATTN_WIDTH
    head = np.arange(tn) // HEAD_DIM
    seg = jnp.asarray((head[:, None] == head[None, :]) / HEAD_DIM, dtype=BF16)
    return pl.pallas_call(
        _proj_qkv_kernel,
        out_shape=jax.ShapeDtypeStruct((n, 3 * tn), BF16),
        grid=(n // tm, 3),
        in_specs=[pl.BlockSpec((tm, d), lambda i, j: (i, 0)),
                  pl.BlockSpec((d, tn), lambda i, j: (0, j)),
                  pl.BlockSpec((None, 1, tn), lambda i, j: (jnp.minimum(j, 1), 0, 0)),
                  pl.BlockSpec((tn, tn), lambda i, j: (0, 0))],
        out_specs=pl.BlockSpec((tm, tn), lambda i, j: (i, j)),
        compiler_params=_cparams(("parallel", "arbitrary")),
        name=name,
    )(xn2, w, nw, seg)


def _t5_bucket(rel):
    half = NUM_BUCKETS // 2
    max_exact = half // 2
    n = np.abs(rel)
    large = max_exact + (np.log(np.maximum(n, 1) / max_exact)
                         / np.log(REL_MAX_DISTANCE / max_exact) * (half - max_exact)).astype(np.int32)
    large = np.minimum(large, half - 1)
    return np.where(rel > 0, half, 0) + np.where(n < max_exact, n, large)


def _attn_bias(bias_tab, dilation):
    t = np.arange(ATTN_TQ)[:, None]
    j = np.arange(ATTN_TK)[None, :]
    rel = np.stack([j - off - t for off in (0, ATTN_SIDE, 2 * ATTN_SIDE)])
    valid = (np.abs(rel) <= ATTN_SIDE).reshape(-1)
    bucket = _t5_bucket(rel * dilation).reshape(-1).astype(np.int32)
    onehot = (jnp.asarray(bucket)[None, :] == jnp.arange(NUM_BUCKETS, dtype=jnp.int32)[:, None])
    vals = jnp.dot(bias_tab.T, onehot.astype(F32), precision=lax.Precision.HIGHEST)
    vals = jnp.where(jnp.asarray(valid)[None, :], vals, NEG_INF)
    return vals.reshape(HEADS_PER_GROUP, 3, ATTN_TQ, ATTN_TK)


def _attn_kernel(q_ref, k_ref, v_ref, bias_ref, o_ref, lse_ref, *, seq, n_sub, n_pairs, blocks):
    nb = seq // ATTN_TQ
    lane = lax.broadcasted_iota(jnp.int32, (ATTN_TQ, LANES), 1)
    first = lane < HEAD_DIM
    nt = (((1,), (1,)), ((), ()))

    def both_heads(col):
        return jnp.where(first, jnp.broadcast_to(col[:ATTN_TQ], (ATTN_TQ, LANES)),
                         jnp.broadcast_to(col[ATTN_TQ:], (ATTN_TQ, LANES)))

    def run(sub, i0):
        units = []
        for bo in range(blocks):
            i = i0 + bo
            q0 = pl.multiple_of(i * ATTN_TQ, ATTN_TQ)
            ks = pl.multiple_of(jnp.clip(i * ATTN_TQ - ATTN_SIDE, 0, seq - ATTN_TK), ATTN_SIDE)
            var = jnp.where(i == 0, 0, jnp.where(i == nb - 1, 2, 1))
            for pr in range(n_pairs):
                units.append(dict(q0=q0, ks=ks, var=var, pr=pr,
                                  cols=slice(pr * LANES, (pr + 1) * LANES)))

        def scores(u):
            q2 = q_ref[sub, pl.ds(u["q0"], ATTN_TQ), u["cols"]]
            k2 = k_ref[sub, pl.ds(u["ks"], ATTN_TK), u["cols"]]
            zero = jnp.zeros_like(q2)
            qq = jnp.concatenate([jnp.where(first, q2, zero), jnp.where(first, zero, q2)], axis=0)
            u["s"] = lax.dot_general(qq, k2, nt, preferred_element_type=F32)

        def row_max(u):
            pr, var = u["pr"], u["var"]
            u["s"] = u["s"] + jnp.concatenate([bias_ref[2 * pr, var], bias_ref[2 * pr + 1, var]],
                                              axis=0)
            u["m"] = jnp.max(u["s"], axis=-1, keepdims=True)

        def probs(u):
            p = jnp.exp2(u.pop("s") - u["m"])
            u["den"] = jnp.sum(p, axis=-1, keepdims=True)
            u["p"] = p.astype(BF16)

        def values(u):
            v2 = v_ref[sub, pl.ds(u["ks"], ATTN_TK), u["cols"]]
            u["pv"] = jnp.dot(u.pop("p"), v2, preferred_element_type=F32)

        def finish(u):
            on = u.pop("pv") * (1.0 / u["den"])
            lse = u["m"] + jnp.log(u["den"]) * (1.0 / np.log(2.0))
            rows = pl.ds(u["q0"], ATTN_TQ)
            o_ref[sub, rows, u["cols"]] = jnp.where(first, on[:ATTN_TQ], on[ATTN_TQ:]).astype(BF16)
            lse_ref[sub, rows, u["cols"]] = both_heads(lse)

        stages = (scores, row_max, probs, values, finish)
        for t in range(len(units) + len(stages) - 1):
            for k, stage in enumerate(stages):
                if 0 <= t - k < len(units):
                    stage(units[t - k])

    for sub in range(n_sub):
        if nb == blocks:
            run(sub, 0)
        else:
            def body(j, carry, sub=sub):
                run(sub, j * blocks)
                return carry

            lax.fori_loop(0, nb // blocks, body, 0)


def _attention(qkv, bias, name):
    n_seq, seq, _ = qkv.shape
    assert seq >= 2 * ATTN_TQ and seq % ATTN_TQ == 0
    if seq >= 4096:
        n_sub, n_pairs, blocks = 1, 2, 2
    else:
        n_pairs, blocks = ATTN_WIDTH // LANES, 2
        n_sub = max(1, min(n_seq, 1024 // seq))
    assert (seq // ATTN_TQ) % blocks == 0
    cw = n_pairs * LANES
    pb = ATTN_WIDTH // cw
    kern = functools.partial(_attn_kernel, seq=seq, n_sub=n_sub, n_pairs=n_pairs, blocks=blocks)
    blk = (n_sub, seq, cw)
    return pl.pallas_call(
        kern,
        out_shape=(jax.ShapeDtypeStruct((n_seq, seq, ATTN_WIDTH), BF16),
                   jax.ShapeDtypeStruct((n_seq, seq, ATTN_WIDTH), F32)),
        grid=(n_seq // n_sub, pb),
        in_specs=[pl.BlockSpec(blk, lambda r, p: (r, 0, p)),
                  pl.BlockSpec(blk, lambda r, p: (r, 0, pb + p)),
                  pl.BlockSpec(blk, lambda r, p: (r, 0, 2 * pb + p)),
                  pl.BlockSpec((2 * n_pairs, 3, ATTN_TQ, ATTN_TK), lambda r, p: (p, 0, 0, 0))],
        out_specs=(pl.BlockSpec(blk, lambda r, p: (r, 0, p)),
                   pl.BlockSpec(blk, lambda r, p: (r, 0, p))),
        compiler_params=_cparams(("parallel", "parallel")),
        name=name,
    )(qkv, qkv, qkv, bias)


def _hgrn_mats():
    t = np.arange(HG_TILE)[:, None]
    u = np.arange(HG_TILE)[None, :]
    same = (t // HG_CHUNK) == (u // HG_CHUNK)
    fwd = same & (u <= t)
    bwd = same & (u >= t)
    return (jnp.asarray(fwd, dtype=BF16), jnp.asarray(bwd, dtype=BF16),
            jnp.asarray(fwd, dtype=F32), jnp.asarray(bwd, dtype=F32))


def _hgrn_kernel(q_ref, zf_ref, zb_ref, i_ref, g_ref, lbf_ref, lbb_ref, nw_ref, mf_ref, mb_ref,
                 lmask_ref, umask_ref, o_ref, acc_ref, qk_ref, qd_ref, ks_ref, kv_ref, dec_ref,
                 st_ref, *, seq):
    n_tiles = seq // HG_TILE
    n_chunks = seq // HG_CHUNK
    cpt = HG_TILE // HG_CHUNK
    mid = HG_CHUNK // 2
    nt = (((1,), (1,)), ((), ()))

    def per_chunk_rows(rows):
        return jnp.concatenate([jnp.broadcast_to(r, (HG_CHUNK, HG_DK)) for r in rows], axis=0)

    def gates(z_ref, lb_ref, m_ref, rows, q, reverse):
        lb = lb_ref[...]
        f = lb + (1.0 - lb) * jax.nn.sigmoid(z_ref[rows, :].astype(F32))
        g = jnp.log(f) * (1.0 / np.log(2.0))
        g_hi = g.astype(BF16)
        g_lo = (g - g_hi.astype(F32)).astype(BF16)
        sums = jnp.dot(m_ref[...], jnp.concatenate([g_hi, g_lo], axis=1),
                       preferred_element_type=F32)
        beta = sums[:, :HG_DK] + sums[:, HG_DK:]
        tot_row = 0 if reverse else HG_CHUNK - 1
        cen_row = mid if reverse else mid - 1
        tot = [beta[c * HG_CHUNK + tot_row:c * HG_CHUNK + tot_row + 1] for c in range(cpt)]
        cen = [beta[c * HG_CHUNK + cen_row:c * HG_CHUNK + cen_row + 1] for c in range(cpt)]
        d = beta - per_chunk_rows(cen)
        q_c = q * jnp.exp2(d)
        k_c = (1.0 - f) * jnp.exp2(-d)
        q_d = q_c * per_chunk_rows([jnp.exp2(c_) for c_ in cen])
        k_s = k_c * per_chunk_rows([jnp.exp2(t_ - c_) for t_, c_ in zip(tot, cen)])
        dec = [jnp.exp2(t_) for t_ in tot]
        return q_c.astype(BF16), k_c.astype(BF16), q_d.astype(BF16), k_s.astype(BF16), dec

    def phase_a1(t_idx, slot):
        rows = pl.ds(pl.multiple_of(t_idx * HG_TILE, HG_TILE), HG_TILE)
        q = q_ref[rows, :].astype(F32)
        qcf, kcf, qdf, ksf, decf = gates(zf_ref, lbf_ref, mf_ref, rows, q, False)
        qcb, kcb, qdb, ksb, decb = gates(zb_ref, lbb_ref, mb_ref, rows, q, True)
        qk_ref[slot] = jnp.concatenate([qcf, kcf, qcb, kcb], axis=1)
        ks_ref[slot] = jnp.concatenate([ksf, ksb], axis=1)
        qd_ref[rows, :] = jnp.concatenate([qdf, qdb], axis=1)
        for c in range(cpt):
            dec_ref[t_idx * cpt + c] = jnp.concatenate(
                [jnp.broadcast_to(decf[c], (8, HG_DK)), jnp.broadcast_to(decb[c], (8, HG_DK))],
                axis=1)

    def phase_a2(t_idx, slot):
        rows = pl.ds(pl.multiple_of(t_idx * HG_TILE, HG_TILE), HG_TILE)
        v = i_ref[rows, :]
        a_f = lax.dot_general(qk_ref[slot, :, 0:HG_DK], qk_ref[slot, :, HG_DK:2 * HG_DK], nt,
                              preferred_element_type=F32)
        a_b = lax.dot_general(qk_ref[slot, :, 2 * HG_DK:3 * HG_DK], qk_ref[slot, :, 3 * HG_DK:], nt,
                              preferred_element_type=F32)
        a = (jnp.where(lmask_ref[...] > 0.0, a_f, 0.0) + jnp.where(umask_ref[...] > 0.0, a_b, 0.0))
        acc_ref[rows, :] = jnp.dot(a.astype(BF16), v, preferred_element_type=F32)
        ks = ks_ref[slot]
        for c in range(cpt):
            cr = slice(c * HG_CHUNK, (c + 1) * HG_CHUNK)
            kv_ref[t_idx * cpt + c] = lax.dot_general(v[cr], ks[cr], (((0,), (0,)), ((), ())),
                                                      preferred_element_type=F32)

    def phase_a(j, carry):
        phase_a1(2 * j + 1, 1)
        phase_a2(2 * j, 0)
        phase_a1(2 * j + 2, 0)
        phase_a2(2 * j + 1, 1)
        return carry

    phase_a1(0, 0)
    lax.fori_loop(0, n_tiles // 2 - 1, phase_a, 0)
    phase_a1(n_tiles - 1, 1)
    phase_a2(n_tiles - 2, 0)
    phase_a2(n_tiles - 1, 1)

    fw = slice(0, HG_DK)
    bw = slice(HG_DK, 2 * HG_DK)

    def phase_b(n, carry):
        sf, sb = carry
        m = n_chunks - 1 - n
        st_ref[n, :, fw] = sf.astype(BF16)
        sf = dec_ref[n][0:1, fw] * sf + kv_ref[n, :, fw]
        st_ref[m, :, bw] = sb.astype(BF16)
        sb = dec_ref[m][0:1, bw] * sb + kv_ref[m, :, bw]
        return sf, sb

    s0 = jnp.zeros((HG_DK, HG_DK), F32)
    lax.fori_loop(0, n_chunks, phase_b, (s0, s0), unroll=4)

    def phase_c(t_idx, carry):
        r0 = pl.multiple_of(t_idx * HG_TILE, HG_TILE)
        rows = pl.ds(r0, HG_TILE)
        inter = [lax.dot_general(qd_ref[pl.ds(r0 + c * HG_CHUNK, HG_CHUNK), :],
                                 st_ref[t_idx * cpt + c], nt, preferred_element_type=F32)
                 for c in range(cpt)]
        o = acc_ref[rows, :] + jnp.concatenate(inter, axis=0)
        ms = jnp.mean(o * o, axis=-1, keepdims=True)
        gate = g_ref[rows, :].astype(F32)
        o_ref[rows, :] = (o * lax.rsqrt(ms + EPS) * nw_ref[...] * (gate * jax.nn.sigmoid(gate))
                          ).astype(BF16)
        return carry

    lax.fori_loop(0, n_tiles, phase_c, 0, unroll=4)


def _hgrn(z_rest3, lb_f, lb_b, nw, col_q, col_zf, col_zb, col_i, col_g):
    b, seq, _ = z_rest3.shape
    mf, mb, lmask, umask = _hgrn_mats()
    n_chunks = seq // HG_CHUNK

    def zspec(col):
        return pl.BlockSpec((None, seq, HG_DK), lambda bi, h, col=col: (bi, 0, col + h))

    vec = pl.BlockSpec((1, HG_DK), lambda bi, h: (0, h))
    const = pl.BlockSpec((HG_TILE, HG_TILE), lambda bi, h: (0, 0))
    return pl.pallas_call(
        functools.partial(_hgrn_kernel, seq=seq),
        out_shape=jax.ShapeDtypeStruct((b, seq, HG_HEADS * HG_DK), BF16),
        grid=(b, HG_HEADS),
        in_specs=[zspec(col_q), zspec(col_zf), zspec(col_zb), zspec(col_i), zspec(col_g),
                  vec, vec, pl.BlockSpec((1, HG_DK), lambda bi, h: (0, 0)), const, const,
                  const, const],
        out_specs=pl.BlockSpec((None, seq, HG_DK), lambda bi, h: (bi, 0, h)),
        scratch_shapes=[pltpu.VMEM((seq, HG_DK), F32),
                        pltpu.VMEM((2, HG_TILE, 4 * HG_DK), BF16),
                        pltpu.VMEM((seq, 2 * HG_DK), BF16),
                        pltpu.VMEM((2, HG_TILE, 2 * HG_DK), BF16),
                        pltpu.VMEM((n_chunks, HG_DK, 2 * HG_DK), F32),
                        pltpu.VMEM((n_chunks, 8, 2 * HG_DK), F32),
                        pltpu.VMEM((n_chunks, HG_DK, 2 * HG_DK), BF16)],
        compiler_params=_cparams(("parallel", "parallel")),
        name="hgrn2",
    )(z_rest3, z_rest3, z_rest3, z_rest3, z_rest3, lb_f, lb_b, nw, mf, mb, lmask, umask)


def _final_kernel(o0_ref, l0_ref, o1_ref, l1_ref, o2_ref, l2_ref, ga_ref, ob_ref, za_ref, zb_ref,
                  x_ref, wa_ref, wb_ref, wo_ref, out_ref, so1_ref, sl1_ref, so2_ref, sl2_ref,
                  *, tm, dilations):
    for src, dst, dil in ((o1_ref, so1_ref, dilations[0]), (l1_ref, sl1_ref, dilations[0]),
                          (o2_ref, so2_ref, dilations[1]), (l2_ref, sl2_ref, dilations[1])):
        for r in range(dil):
            val = src[r].astype(F32)
            for c in range(ATTN_WIDTH // LANES):
                dst[c, pl.ds(r, tm // dil, stride=dil), :] = val[:, c * LANES:(c + 1) * LANES]

    def natural(ref):
        return jnp.concatenate([ref[c] for c in range(ATTN_WIDTH // LANES)], axis=1)

    l0, l1, l2 = l0_ref[...], natural(sl1_ref), natural(sl2_ref)
    mx = jnp.maximum(jnp.maximum(l0, l1), l2)
    w0, w1, w2 = jnp.exp2(l0 - mx), jnp.exp2(l1 - mx), jnp.exp2(l2 - mx)
    num = w0 * o0_ref[...].astype(F32) + w1 * natural(so1_ref) + w2 * natural(so2_ref)
    o_a = num / (w0 + w1 + w2)
    ga = ga_ref[...].astype(F32)
    a = (o_a * (ga * jax.nn.sigmoid(ga))).astype(BF16)
    y_a = jnp.dot(a, wa_ref[...], preferred_element_type=F32)
    y_b = jnp.dot(ob_ref[...], wb_ref[...], preferred_element_type=F32)
    merged = (jax.nn.sigmoid(za_ref[...].astype(F32)) * y_a
              + jax.nn.sigmoid(zb_ref[...].astype(F32)) * y_b)
    out_ref[...] = x_ref[...] + jnp.dot(merged.astype(BF16), wo_ref[...],
                                        preferred_element_type=F32)


def _final(o_g, lse_g, z_rest, o_b, x3, wa, wb, wo, dilations, col_ga, col_za, col_zb):
    b, seq, d = x3.shape
    tm = 512
    aw = ATTN_WIDTH

    def rows(width, col=0):
        return pl.BlockSpec((None, tm, width), lambda bi, i, col=col: (bi, i, col))

    def perm(dil):
        return pl.BlockSpec((None, dil, tm // dil, aw), lambda bi, i: (bi, 0, i, 0))

    def full(shape):
        return pl.BlockSpec(shape, lambda bi, i: (0, 0))

    d1, d2 = dilations
    return pl.pallas_call(
        functools.partial(_final_kernel, tm=tm, dilations=dilations),
        out_shape=jax.ShapeDtypeStruct((b, seq, d), F32),
        grid=(b, seq // tm),
        in_specs=[rows(aw), rows(aw), perm(d1), perm(d1), perm(d2), perm(d2),
                  rows(aw, col_ga), rows(d), rows(d, col_za), rows(d, col_zb), rows(d),
                  full(wa.shape), full(wb.shape), full(wo.shape)],
        out_specs=rows(d),
        scratch_shapes=[pltpu.VMEM((aw // LANES, tm, LANES), F32)] * 4,
        compiler_params=_cparams(("parallel", "parallel")),
        name="merge_out",
    )(o_g[0], lse_g[0], o_g[1], lse_g[1], o_g[2], lse_g[2], z_rest, o_b, z_rest, z_rest, x3,
      wa, wb, wo)


def kernel(x, norm_w, w_in, q_norm_w, k_norm_w, rel_bias, lb_fwd, lb_bwd, hg_norm_w,
           w_proj_a, w_proj_b, w_out):
    b, seq, d = x.shape
    n = b * seq
    layer = 0
    w = w_in[layer].astype(BF16)
    qkv_cols = 3 * len(ATTN_GROUPS) * ATTN_WIDTH
    dilations = tuple(dil for _, dil in ATTN_GROUPS)
    assert dilations[0] == 1

    xn_all = _norm(x.astype(F32), norm_w[layer].reshape(1, d).astype(F32), dilations[1:])
    xn = xn_all[0].reshape(n, d)

    w_rest = jnp.concatenate([w[:, qkv_cols + ATTN_WIDTH:], w[:, qkv_cols:qkv_cols + ATTN_WIDTH]],
                             axis=1)
    z_rest = _proj_rest(xn, w_rest).reshape(b, seq, -1)
    hw = HG_HEADS * HG_DK

    o_g, lse_g = [], []
    for g, dilation in enumerate(dilations):
        sub_len = seq // dilation
        wg = w[:, g * 3 * ATTN_WIDTH:(g + 1) * 3 * ATTN_WIDTH]
        q_scale = HEAD_DIM ** -0.5 * LOG2E
        nw = jnp.stack([jnp.tile(q_norm_w[layer, g].astype(F32), HEADS_PER_GROUP) * q_scale,
                        jnp.tile(k_norm_w[layer, g].astype(F32), HEADS_PER_GROUP)]
                       ).reshape(2, 1, ATTN_WIDTH)
        qkv = _proj_qkv(xn_all[g].reshape(n, d), wg, nw, f"proj_qkv_d{dilation}")
        heads = slice(g * HEADS_PER_GROUP, (g + 1) * HEADS_PER_GROUP)
        bias = _attn_bias(rel_bias.astype(F32)[:, heads] * LOG2E, dilation)
        o, lse = _attention(qkv.reshape(b * dilation, sub_len, 3 * ATTN_WIDTH), bias,
                            f"attn_d{dilation}")
        shape = (b, seq, ATTN_WIDTH) if dilation == 1 else (b, dilation, sub_len, ATTN_WIDTH)
        o_g.append(o.reshape(shape))
        lse_g.append(lse.reshape(shape))

    lb_f = jnp.cumsum(jax.nn.softmax(lb_fwd.astype(F32), axis=0), axis=0)[layer].reshape(1, hw)
    lb_b = jnp.cumsum(jax.nn.softmax(lb_bwd.astype(F32), axis=0), axis=0)[layer].reshape(1, hw)
    hb = hw // LANES
    o_b = _hgrn(z_rest, lb_f, lb_b, hg_norm_w[layer].reshape(1, HG_DK).astype(F32),
                col_q=0, col_zf=hb, col_zb=2 * hb, col_i=3 * hb, col_g=4 * hb)

    out = _final(o_g, lse_g, z_rest, o_b, x.astype(F32),
                 w_proj_a[layer].astype(BF16), w_proj_b[layer].astype(BF16),
                 w_out[layer].astype(BF16), dilations[1:],
                 col_ga=7 * hw // ATTN_WIDTH, col_za=5, col_zb=6)
    return out.astype(x.dtype)
```

```python
import functools

import numpy as np
import jax
import jax.numpy as jnp
from jax import lax
from jax.experimental import pallas as pl
from jax.experimental.pallas import tpu as pltpu

F32 = jnp.float32
BF16 = jnp.bfloat16

EPS = 1e-6
LOG2E = float(np.log2(np.e))
NEG_INF = -1e30
ATTN_GROUPS = ((128, 1), (512, 4), (2048, 16))
HEAD_DIM = 64
HEADS_PER_GROUP = 8
ATTN_WIDTH = HEADS_PER_GROUP * HEAD_DIM
NUM_BUCKETS = 32
REL_MAX_DISTANCE = 1024
HG_HEADS = 8
HG_DK = 128

LANES = 128
VMEM_LIMIT = 52 * 1024 * 1024

ATTN_TQ = 128
ATTN_SIDE = 64
ATTN_TK = ATTN_TQ + 2 * ATTN_SIDE

HG_CHUNK = 64
HG_TILE = 256
HG_TILES_PER_TRIP = 4


def _cparams(sem):
    return pltpu.CompilerParams(dimension_semantics=sem, vmem_limit_bytes=VMEM_LIMIT)


def _emit_skewed(units, stages):
    for step in range(len(units) + len(stages) - 1):
        for k, stage in enumerate(stages):
            if 0 <= step - k < len(units):
                stage(units[step - k])


def _norm_kernel(x_ref, w_ref, o_ref, *rest, tm, dilations):
    perm_refs, y_ref = rest[:-1], rest[-1]
    x = x_ref[...]
    ms = jnp.mean(x * x, axis=-1, keepdims=True)
    y = x * lax.rsqrt(ms + EPS) * w_ref[...]
    o_ref[...] = y.astype(BF16)
    n_lane_blocks = y.shape[1] // LANES
    for c in range(n_lane_blocks):
        y_ref[c] = y[:, c * LANES:(c + 1) * LANES]
    for p_ref, dil in zip(perm_refs, dilations):
        for r in range(dil):
            for c in range(n_lane_blocks):
                p_ref[r, :, c * LANES:(c + 1) * LANES] = (
                    y_ref[c, pl.ds(r, tm // dil, stride=dil), :].astype(BF16))


def _norm(x3, w, dilations):
    b, seq, d = x3.shape
    tm = 1024
    out_shape = [jax.ShapeDtypeStruct((b, seq, d), BF16)]
    out_specs = [pl.BlockSpec((None, tm, d), lambda bi, i: (bi, i, 0))]
    for dil in dilations:
        out_shape.append(jax.ShapeDtypeStruct((b, dil, seq // dil, d), BF16))
        out_specs.append(pl.BlockSpec((None, dil, tm // dil, d), lambda bi, i: (bi, 0, i, 0)))
    return pl.pallas_call(
        functools.partial(_norm_kernel, tm=tm, dilations=dilations),
        out_shape=out_shape,
        grid=(b, seq // tm),
        in_specs=[pl.BlockSpec((None, tm, d), lambda bi, i: (bi, i, 0)),
                  pl.BlockSpec((1, d), lambda bi, i: (0, 0))],
        out_specs=out_specs,
        scratch_shapes=[pltpu.VMEM((d // LANES, tm, LANES), F32)],
        compiler_params=_cparams(("parallel", "parallel")),
        name="rmsnorm",
    )(x3, w)


def _proj_kernel(x_ref, w_ref, o_ref):
    o_ref[...] = jnp.dot(x_ref[...], w_ref[...], preferred_element_type=F32).astype(o_ref.dtype)


def _proj_rest(xn, w):
    n, d = xn.shape
    c = w.shape[1]
    tm, tn = 1024, 1536
    return pl.pallas_call(
        _proj_kernel,
        out_shape=jax.ShapeDtypeStruct((n, c), BF16),
        grid=(n // tm, c // tn),
        in_specs=[pl.BlockSpec((tm, d), lambda i, j: (i, 0)),
                  pl.BlockSpec((d, tn), lambda i, j: (0, j))],
        out_specs=pl.BlockSpec((tm, tn), lambda i, j: (i, j)),
        compiler_params=_cparams(("parallel", "arbitrary")),
        name="proj_rest",
    )(xn, w)


def _proj_qkv_kernel(x_ref, w_ref, nw_ref, seg_ref, o_ref):
    j = pl.program_id(1)

    @pl.when(j < 2)
    def _():
        half = x_ref.shape[0] // 2
        units = [dict(rows=slice(r * half, (r + 1) * half)) for r in range(2)]

        def project(u):
            u["z"] = jnp.dot(x_ref[u["rows"], :], w_ref[...], preferred_element_type=F32)

        def square(u):
            u["zz"] = (u["z"] * u["z"]).astype(BF16)

        def mean_square(u):
            u["ms"] = jnp.dot(u.pop("zz"), seg_ref[...], preferred_element_type=F32)

        def normalise(u):
            o_ref[u["rows"], :] = (u.pop("z") * lax.rsqrt(u.pop("ms") + EPS) * nw_ref[...]
                                   ).astype(BF16)

        _emit_skewed(units, (project, square, mean_square, normalise))

    @pl.when(j == 2)
    def _():
        o_ref[...] = jnp.dot(x_ref[...], w_ref[...], preferred_element_type=F32).astype(BF16)


def _proj_qkv(xn2, w, nw, name):
    n, d = xn2.shape
    tm, tn = 1024, ATTN_WIDTH
    head = np.arange(tn) // HEAD_DIM
    seg = jnp.asarray((head[:, None] == head[None, :]) / HEAD_DIM, dtype=BF16)
    return pl.pallas_call(
        _proj_qkv_kernel,
        out_shape=jax.ShapeDtypeStruct((n, 3 * tn), BF16),
        grid=(n // tm, 3),
        in_specs=[pl.BlockSpec((tm, d), lambda i, j: (i, 0)),
                  pl.BlockSpec((d, tn), lambda i, j: (0, j)),
                  pl.BlockSpec((None, 1, tn), lambda i, j: (jnp.minimum(j, 1), 0, 0)),
                  pl.BlockSpec((tn, tn), lambda i, j: (0, 0))],
        out_specs=pl.BlockSpec((tm, tn), lambda i, j: (i, j)),
        compiler_params=_cparams(("parallel", "arbitrary")),
        name=name,
    )(xn2, w, nw, seg)


def _t5_bucket(rel):
    half = NUM_BUCKETS // 2
    max_exact = half // 2
    n = np.abs(rel)
    large = max_exact + (np.log(np.maximum(n, 1) / max_exact)
                         / np.log(REL_MAX_DISTANCE / max_exact) * (half - max_exact)).astype(np.int32)
    large = np.minimum(large, half - 1)
    return np.where(rel > 0, half, 0) + np.where(n < max_exact, n, large)


def _attn_bias(bias_tab, dilation):
    t = np.arange(ATTN_TQ)[:, None]
    j = np.arange(ATTN_TK)[None, :]
    rel = np.stack([j - off - t for off in (0, ATTN_SIDE, 2 * ATTN_SIDE)])
    valid = (np.abs(rel) <= ATTN_SIDE).reshape(-1)
    bucket = _t5_bucket(rel * dilation).reshape(-1).astype(np.int32)
    onehot = (jnp.asarray(bucket)[None, :] == jnp.arange(NUM_BUCKETS, dtype=jnp.int32)[:, None])
    vals = jnp.dot(bias_tab.T, onehot.astype(F32), precision=lax.Precision.HIGHEST)
    vals = jnp.where(jnp.asarray(valid)[None, :], vals, NEG_INF)
    return vals.reshape(HEADS_PER_GROUP, 3, ATTN_TQ, ATTN_TK)


def _attn_kernel(q_ref, k_ref, v_ref, bias_ref, o_ref, lse_ref, *, seq, n_sub, n_pairs, blocks):
    nb = seq // ATTN_TQ
    lane = lax.broadcasted_iota(jnp.int32, (ATTN_TQ, LANES), 1)
    first = lane < HEAD_DIM
    nt = (((1,), (1,)), ((), ()))

    def both_heads(col):
        return jnp.where(first, jnp.broadcast_to(col[:ATTN_TQ], (ATTN_TQ, LANES)),
                         jnp.broadcast_to(col[ATTN_TQ:], (ATTN_TQ, LANES)))

    def run(sub, i0):
        units = []
        for bo in range(blocks):
            i = i0 + bo
            q0 = pl.multiple_of(i * ATTN_TQ, ATTN_TQ)
            ks = pl.multiple_of(jnp.clip(i * ATTN_TQ - ATTN_SIDE, 0, seq - ATTN_TK), ATTN_SIDE)
            var = jnp.where(i == 0, 0, jnp.where(i == nb - 1, 2, 1))
            for pr in range(n_pairs):
                units.append(dict(q0=q0, ks=ks, var=var, pr=pr,
                                  cols=slice(pr * LANES, (pr + 1) * LANES)))

        def scores(u):
            q2 = q_ref[sub, pl.ds(u["q0"], ATTN_TQ), u["cols"]]
            k2 = k_ref[sub, pl.ds(u["ks"], ATTN_TK), u["cols"]]
            zero = jnp.zeros_like(q2)
            qq = jnp.concatenate([jnp.where(first, q2, zero), jnp.where(first, zero, q2)], axis=0)
            u["s"] = lax.dot_general(qq, k2, nt, preferred_element_type=F32)

        def row_max(u):
            pr, var = u["pr"], u["var"]
            u["s"] = u["s"] + jnp.concatenate([bias_ref[2 * pr, var], bias_ref[2 * pr + 1, var]],
                                              axis=0)
            u["m"] = jnp.max(u["s"], axis=-1, keepdims=True)

        def probs(u):
            p = jnp.exp2(u.pop("s") - u["m"])
            u["den"] = jnp.sum(p, axis=-1, keepdims=True)
            u["p"] = p.astype(BF16)

        def values(u):
            v2 = v_ref[sub, pl.ds(u["ks"], ATTN_TK), u["cols"]]
            u["pv"] = jnp.dot(u.pop("p"), v2, preferred_element_type=F32)

        def finish(u):
            on = u.pop("pv") * (1.0 / u["den"])
            lse = u["m"] + jnp.log(u["den"]) * (1.0 / np.log(2.0))
            rows = pl.ds(u["q0"], ATTN_TQ)
            o_ref[sub, rows, u["cols"]] = jnp.where(first, on[:ATTN_TQ], on[ATTN_TQ:]).astype(BF16)
            lse_ref[sub, rows, u["cols"]] = both_heads(lse)

        _emit_skewed(units, (scores, row_max, probs, values, finish))

    for sub in range(n_sub):
        if nb == blocks:
            run(sub, 0)
        else:
            def body(j, carry, sub=sub):
                run(sub, j * blocks)
                return carry

            lax.fori_loop(0, nb // blocks, body, 0)


def _attention(qkv, bias, name):
    n_seq, seq, _ = qkv.shape
    assert seq >= 2 * ATTN_TQ and seq % ATTN_TQ == 0
    if seq >= 4096:
        n_sub, n_pairs, blocks = 1, 2, 2
    else:
        n_pairs, blocks = ATTN_WIDTH // LANES, 2
        n_sub = max(1, min(n_seq, 1024 // seq))
    assert (seq // ATTN_TQ) % blocks == 0
    cw = n_pairs * LANES
    pb = ATTN_WIDTH // cw
    kern = functools.partial(_attn_kernel, seq=seq, n_sub=n_sub, n_pairs=n_pairs, blocks=blocks)
    blk = (n_sub, seq, cw)
    return pl.pallas_call(
        kern,
        out_shape=(jax.ShapeDtypeStruct((n_seq, seq, ATTN_WIDTH), BF16),
                   jax.ShapeDtypeStruct((n_seq, seq, ATTN_WIDTH), F32)),
        grid=(n_seq // n_sub, pb),
        in_specs=[pl.BlockSpec(blk, lambda r, p: (r, 0, p)),
                  pl.BlockSpec(blk, lambda r, p: (r, 0, pb + p)),
                  pl.BlockSpec(blk, lambda r, p: (r, 0, 2 * pb + p)),
                  pl.BlockSpec((2 * n_pairs, 3, ATTN_TQ, ATTN_TK), lambda r, p: (p, 0, 0, 0))],
        out_specs=(pl.BlockSpec(blk, lambda r, p: (r, 0, p)),
                   pl.BlockSpec(blk, lambda r, p: (r, 0, p))),
        compiler_params=_cparams(("parallel", "parallel")),
        name=name,
    )(qkv, qkv, qkv, bias)


def _hgrn_mats():
    t = np.arange(HG_TILE)[:, None]
    u = np.arange(HG_TILE)[None, :]
    same = (t // HG_CHUNK) == (u // HG_CHUNK)
    fwd = same & (u <= t)
    bwd = same & (u >= t)
    return (jnp.asarray(fwd, dtype=BF16), jnp.asarray(bwd, dtype=BF16),
            jnp.asarray(fwd, dtype=F32), jnp.asarray(bwd, dtype=F32))


def _hgrn_kernel(q_ref, zf_ref, zb_ref, i_ref, g_ref, lbf_ref, lbb_ref, nw_ref, mf_ref, mb_ref,
                 lmask_ref, umask_ref, o_ref, acc_ref, qd_ref, kv_ref, dec_ref, st_ref, *, seq):
    n_tiles = seq // HG_TILE
    n_chunks = seq // HG_CHUNK
    cpt = HG_TILE // HG_CHUNK
    mid = HG_CHUNK // 2
    nt = (((1,), (1,)), ((), ()))

    def per_chunk_rows(rows):
        return jnp.concatenate([jnp.broadcast_to(r, (HG_CHUNK, HG_DK)) for r in rows], axis=0)

    scans = ((zf_ref, lbf_ref, mf_ref, False), (zb_ref, lbb_ref, mb_ref, True))

    def a_gate(u):
        u["f"], u["ghl"] = [], []
        for z_ref, lb_ref, _, _ in scans:
            lb = lb_ref[...]
            f = lb + (1.0 - lb) * jax.nn.sigmoid(z_ref[u["rows"], :].astype(F32))
            g = jnp.log(f) * (1.0 / np.log(2.0))
            g_hi = g.astype(BF16)
            g_lo = (g - g_hi.astype(F32)).astype(BF16)
            u["f"].append(f)
            u["ghl"].append(jnp.concatenate([g_hi, g_lo], axis=1))

    def a_cumsum(u):
        u["sums"] = [jnp.dot(m_ref[...], ghl, preferred_element_type=F32)
                     for (_, _, m_ref, _), ghl in zip(scans, u.pop("ghl"))]

    def a_decay(u):
        q = q_ref[u["rows"], :].astype(F32)
        q_c, k_c, q_d, k_s, decs = [], [], [], [], []
        for (_, _, _, reverse), sums, f in zip(scans, u.pop("sums"), u.pop("f")):
            beta = sums[:, :HG_DK] + sums[:, HG_DK:]
            tot_row = 0 if reverse else HG_CHUNK - 1
            cen_row = mid if reverse else mid - 1
            tot = [beta[c * HG_CHUNK + tot_row:c * HG_CHUNK + tot_row + 1] for c in range(cpt)]
            cen = [beta[c * HG_CHUNK + cen_row:c * HG_CHUNK + cen_row + 1] for c in range(cpt)]
            d = beta - per_chunk_rows(cen)
            qc = q * jnp.exp2(d)
            kc = (1.0 - f) * jnp.exp2(-d)
            q_d.append((qc * per_chunk_rows([jnp.exp2(c_) for c_ in cen])).astype(BF16))
            k_s.append((kc * per_chunk_rows([jnp.exp2(t_ - c_) for t_, c_ in zip(tot, cen)])
                        ).astype(BF16))
            q_c.append(qc.astype(BF16))
            k_c.append(kc.astype(BF16))
            decs.append([jnp.exp2(t_) for t_ in tot])
        qd_ref[u["rows"], :] = jnp.concatenate(q_d, axis=1)
        u["ks"] = jnp.concatenate(k_s, axis=1)
        u["qc"], u["kc"] = q_c, k_c
        for c in range(cpt):
            dec_ref[u["t"] * cpt + c] = jnp.concatenate(
                [jnp.broadcast_to(decs[0][c], (8, HG_DK)), jnp.broadcast_to(decs[1][c], (8, HG_DK))],
                axis=1)

    def a_scores(u):
        u["a"] = [lax.dot_general(qc, kc, nt, preferred_element_type=F32)
                  for qc, kc in zip(u.pop("qc"), u.pop("kc"))]

    def a_mask(u):
        a_f, a_b = u.pop("a")
        u["a"] = (jnp.where(lmask_ref[...] > 0.0, a_f, 0.0)
                  + jnp.where(umask_ref[...] > 0.0, a_b, 0.0)).astype(BF16)

    def a_values(u):
        v = i_ref[u["rows"], :]
        acc_ref[u["rows"], :] = jnp.dot(u.pop("a"), v, preferred_element_type=F32)
        ks = u.pop("ks")
        for c in range(cpt):
            cr = slice(c * HG_CHUNK, (c + 1) * HG_CHUNK)
            kv_ref[u["t"] * cpt + c] = lax.dot_general(v[cr], ks[cr], (((0,), (0,)), ((), ())),
                                                       preferred_element_type=F32)

    a_stages = (a_gate, a_cumsum, a_decay, a_scores, a_mask, a_values)

    def phase_a(j, carry):
        units = []
        for k in range(HG_TILES_PER_TRIP):
            t = j * HG_TILES_PER_TRIP + k
            units.append(dict(t=t, rows=pl.ds(pl.multiple_of(t * HG_TILE, HG_TILE), HG_TILE)))
        _emit_skewed(units, a_stages)
        return carry

    lax.fori_loop(0, n_tiles // HG_TILES_PER_TRIP, phase_a, 0)

    fw = slice(0, HG_DK)
    bw = slice(HG_DK, 2 * HG_DK)

    def phase_b(n, carry):
        sf, sb = carry
        m = n_chunks - 1 - n
        st_ref[n, :, fw] = sf.astype(BF16)
        sf = dec_ref[n][0:1, fw] * sf + kv_ref[n, :, fw]
        st_ref[m, :, bw] = sb.astype(BF16)
        sb = dec_ref[m][0:1, bw] * sb + kv_ref[m, :, bw]
        return sf, sb

    s0 = jnp.zeros((HG_DK, HG_DK), F32)
    lax.fori_loop(0, n_chunks, phase_b, (s0, s0), unroll=4)

    def phase_c(t_idx, carry):
        r0 = pl.multiple_of(t_idx * HG_TILE, HG_TILE)
        rows = pl.ds(r0, HG_TILE)
        inter = [lax.dot_general(qd_ref[pl.ds(r0 + c * HG_CHUNK, HG_CHUNK), :],
                                 st_ref[t_idx * cpt + c], nt, preferred_element_type=F32)
                 for c in range(cpt)]
        o = acc_ref[rows, :] + jnp.concatenate(inter, axis=0)
        ms = jnp.mean(o * o, axis=-1, keepdims=True)
        gate = g_ref[rows, :].astype(F32)
        o_ref[rows, :] = (o * lax.rsqrt(ms + EPS) * nw_ref[...] * (gate * jax.nn.sigmoid(gate))
                          ).astype(BF16)
        return carry

    lax.fori_loop(0, n_tiles, phase_c, 0, unroll=4)


def _hgrn(z_rest3, lb_f, lb_b, nw, col_q, col_zf, col_zb, col_i, col_g):
    b, seq, _ = z_rest3.shape
    mf, mb, lmask, umask = _hgrn_mats()
    n_chunks = seq // HG_CHUNK

    def zspec(col):
        return pl.BlockSpec((None, seq, HG_DK), lambda bi, h, col=col: (bi, 0, col + h))

    vec = pl.BlockSpec((1, HG_DK), lambda bi, h: (0, h))
    const = pl.BlockSpec((HG_TILE, HG_TILE), lambda bi, h: (0, 0))
    return pl.pallas_call(
        functools.partial(_hgrn_kernel, seq=seq),
        out_shape=jax.ShapeDtypeStruct((b, seq, HG_HEADS * HG_DK), BF16),
        grid=(b, HG_HEADS),
        in_specs=[zspec(col_q), zspec(col_zf), zspec(col_zb), zspec(col_i), zspec(col_g),
                  vec, vec, pl.BlockSpec((1, HG_DK), lambda bi, h: (0, 0)), const, const,
                  const, const],
        out_specs=pl.BlockSpec((None, seq, HG_DK), lambda bi, h: (bi, 0, h)),
        scratch_shapes=[pltpu.VMEM((seq, HG_DK), F32),
                        pltpu.VMEM((seq, 2 * HG_DK), BF16),
                        pltpu.VMEM((n_chunks, HG_DK, 2 * HG_DK), F32),
                        pltpu.VMEM((n_chunks, 8, 2 * HG_DK), F32),
                        pltpu.VMEM((n_chunks, HG_DK, 2 * HG_DK), BF16)],
        compiler_params=_cparams(("parallel", "parallel")),
        name="hgrn2",
    )(z_rest3, z_rest3, z_rest3, z_rest3, z_rest3, lb_f, lb_b, nw, mf, mb, lmask, umask)


def _final_kernel(o0_ref, l0_ref, o1_ref, l1_ref, o2_ref, l2_ref, ga_ref, ob_ref, za_ref, zb_ref,
                  x_ref, wa_ref, wb_ref, wo_ref, out_ref, so1_ref, sl1_ref, so2_ref, sl2_ref,
                  *, tm, dilations):
    for src, dst, dil in ((o1_ref, so1_ref, dilations[0]), (l1_ref, sl1_ref, dilations[0]),
                          (o2_ref, so2_ref, dilations[1]), (l2_ref, sl2_ref, dilations[1])):
        for r in range(dil):
            val = src[r].astype(F32)
            for c in range(ATTN_WIDTH // LANES):
                dst[c, pl.ds(r, tm // dil, stride=dil), :] = val[:, c * LANES:(c + 1) * LANES]

    def natural(ref, rows):
        return jnp.concatenate([ref[c, rows, :] for c in range(ATTN_WIDTH // LANES)], axis=1)

    def merge(u):
        rows = u["rows"]
        l0, l1, l2 = l0_ref[rows, :], natural(sl1_ref, rows), natural(sl2_ref, rows)
        mx = jnp.maximum(jnp.maximum(l0, l1), l2)
        w0, w1, w2 = jnp.exp2(l0 - mx), jnp.exp2(l1 - mx), jnp.exp2(l2 - mx)
        num = (w0 * o0_ref[rows, :].astype(F32) + w1 * natural(so1_ref, rows)
               + w2 * natural(so2_ref, rows))
        ga = ga_ref[rows, :].astype(F32)
        u["a"] = (num / (w0 + w1 + w2) * (ga * jax.nn.sigmoid(ga))).astype(BF16)

    def branch_proj(u):
        u["y_a"] = jnp.dot(u.pop("a"), wa_ref[...], preferred_element_type=F32)
        u["y_b"] = jnp.dot(ob_ref[u["rows"], :], wb_ref[...], preferred_element_type=F32)

    def gate(u):
        rows = u["rows"]
        u["merged"] = (jax.nn.sigmoid(za_ref[rows, :].astype(F32)) * u.pop("y_a")
                       + jax.nn.sigmoid(zb_ref[rows, :].astype(F32)) * u.pop("y_b")).astype(BF16)

    def out_proj(u):
        rows = u["rows"]
        out_ref[rows, :] = x_ref[rows, :] + jnp.dot(u.pop("merged"), wo_ref[...],
                                                    preferred_element_type=F32)

    tile = dict(rows=slice(0, tm))
    for stage in (merge, branch_proj, gate, out_proj):
        stage(tile)


def _final(o_g, lse_g, z_rest, o_b, x3, wa, wb, wo, dilations, col_ga, col_za, col_zb):
    b, seq, d = x3.shape
    tm = 512
    aw = ATTN_WIDTH

    def rows(width, col=0):
        return pl.BlockSpec((None, tm, width), lambda bi, i, col=col: (bi, i, col))

    def perm(dil):
        return pl.BlockSpec((None, dil, tm // dil, aw), lambda bi, i: (bi, 0, i, 0))

    def full(shape):
        return pl.BlockSpec(shape, lambda bi, i: (0, 0))

    d1, d2 = dilations
    return pl.pallas_call(
        functools.partial(_final_kernel, tm=tm, dilations=dilations),
        out_shape=jax.ShapeDtypeStruct((b, seq, d), F32),
        grid=(b, seq // tm),
        in_specs=[rows(aw), rows(aw), perm(d1), perm(d1), perm(d2), perm(d2),
                  rows(aw, col_ga), rows(d), rows(d, col_za), rows(d, col_zb), rows(d),
                  full(wa.shape), full(wb.shape), full(wo.shape)],
        out_specs=rows(d),
        scratch_shapes=[pltpu.VMEM((aw // LANES, tm, LANES), F32)] * 4,
        compiler_params=_cparams(("parallel", "parallel")),
        name="merge_out",
    )(o_g[0], lse_g[0], o_g[1], lse_g[1], o_g[2], lse_g[2], z_rest, o_b, z_rest, z_rest, x3,
      wa, wb, wo)


def kernel(x, norm_w, w_in, q_norm_w, k_norm_w, rel_bias, lb_fwd, lb_bwd, hg_norm_w,
           w_proj_a, w_proj_b, w_out):
    b, seq, d = x.shape
    n = b * seq
    layer = 0
    w = w_in[layer].astype(BF16)
    qkv_cols = 3 * len(ATTN_GROUPS) * ATTN_WIDTH
    dilations = tuple(dil for _, dil in ATTN_GROUPS)
    assert dilations[0] == 1

    xn_all = _norm(x.astype(F32), norm_w[layer].reshape(1, d).astype(F32), dilations[1:])
    xn = xn_all[0].reshape(n, d)

    w_rest = jnp.concatenate([w[:, qkv_cols + ATTN_WIDTH:], w[:, qkv_cols:qkv_cols + ATTN_WIDTH]],
                             axis=1)
    z_rest = _proj_rest(xn, w_rest).reshape(b, seq, -1)
    hw = HG_HEADS * HG_DK

    o_g, lse_g = [], []
    for g, dilation in enumerate(dilations):
        sub_len = seq // dilation
        wg = w[:, g * 3 * ATTN_WIDTH:(g + 1) * 3 * ATTN_WIDTH]
        q_scale = HEAD_DIM ** -0.5 * LOG2E
        nw = jnp.stack([jnp.tile(q_norm_w[layer, g].astype(F32), HEADS_PER_GROUP) * q_scale,
                        jnp.tile(k_norm_w[layer, g].astype(F32), HEADS_PER_GROUP)]
                       ).reshape(2, 1, ATTN_WIDTH)
        qkv = _proj_qkv(xn_all[g].reshape(n, d), wg, nw, f"proj_qkv_d{dilation}")
        heads = slice(g * HEADS_PER_GROUP, (g + 1) * HEADS_PER_GROUP)
        bias = _attn_bias(rel_bias.astype(F32)[:, heads] * LOG2E, dilation)
        o, lse = _attention(qkv.reshape(b * dilation, sub_len, 3 * ATTN_WIDTH), bias,
                            f"attn_d{dilation}")
        shape = (b, seq, ATTN_WIDTH) if dilation == 1 else (b, dilation, sub_len, ATTN_WIDTH)
        o_g.append(o.reshape(shape))
        lse_g.append(lse.reshape(shape))

    lb_f = jnp.cumsum(jax.nn.softmax(lb_fwd.astype(F32), axis=0), axis=0)[layer].reshape(1, hw)
    lb_b = jnp.cumsum(jax.nn.softmax(lb_bwd.astype(F32), axis=0), axis=0)[layer].reshape(1, hw)
    hb = hw // LANES
    o_b = _hgrn(z_rest, lb_f, lb_b, hg_norm_w[layer].reshape(1, HG_DK).astype(F32),
                col_q=0, col_zf=hb, col_zb=2 * hb, col_i=3 * hb, col_g=4 * hb)

    out = _final(o_g, lse_g, z_rest, o_b, x.astype(F32),
                 w_proj_a[layer].astype(BF16), w_proj_b[layer].astype(BF16),
                 w_out[layer].astype(BF16), dilations[1:],
                 col_ga=7 * hw // ATTN_WIDTH, col_za=5, col_zb=6)
    return out.astype(x.dtype)
```

```python
import functools

import numpy as np
import jax
import jax.numpy as jnp
from jax import lax
from jax.experimental import pallas as pl
from jax.experimental.pallas import tpu as pltpu

F32 = jnp.float32
BF16 = jnp.bfloat16

EPS = 1e-6
LOG2E = float(np.log2(np.e))
NEG_INF = -1e30
ATTN_GROUPS = ((128, 1), (512, 4), (2048, 16))
HEAD_DIM = 64
HEADS_PER_GROUP = 8
ATTN_WIDTH = HEADS_PER_GROUP * HEAD_DIM
NUM_BUCKETS = 32
REL_MAX_DISTANCE = 1024
HG_HEADS = 8
HG_DK = 128

LANES = 128
VMEM_LIMIT = 52 * 1024 * 1024

ATTN_TQ = 128
ATTN_SIDE = 64
ATTN_TK = ATTN_TQ + 2 * ATTN_SIDE

HG_CHUNK = 64
HG_TILE = 256
HG_TILES_PER_TRIP = 4


def _cparams(sem):
    return pltpu.CompilerParams(dimension_semantics=sem, vmem_limit_bytes=VMEM_LIMIT)


def _emit_skewed(units, stages):
    for step in range(len(units) + len(stages) - 1):
        for k, stage in enumerate(stages):
            if 0 <= step - k < len(units):
                stage(units[step - k])


def _norm_kernel(x_ref, w_ref, o_ref, *rest, tm, dilations):
    perm_refs, y_ref = rest[:-1], rest[-1]
    x = x_ref[...]
    ms = jnp.mean(x * x, axis=-1, keepdims=True)
    y = x * lax.rsqrt(ms + EPS) * w_ref[...]
    o_ref[...] = y.astype(BF16)
    n_lane_blocks = y.shape[1] // LANES
    for c in range(n_lane_blocks):
        y_ref[c] = y[:, c * LANES:(c + 1) * LANES]
    for p_ref, dil in zip(perm_refs, dilations):
        for r in range(dil):
            for c in range(n_lane_blocks):
                p_ref[r, :, c * LANES:(c + 1) * LANES] = (
                    y_ref[c, pl.ds(r, tm // dil, stride=dil), :].astype(BF16))


def _norm(x3, w, dilations):
    b, seq, d = x3.shape
    tm = 1024
    out_shape = [jax.ShapeDtypeStruct((b, seq, d), BF16)]
    out_specs = [pl.BlockSpec((None, tm, d), lambda bi, i: (bi, i, 0))]
    for dil in dilations:
        out_shape.append(jax.ShapeDtypeStruct((b, dil, seq // dil, d), BF16))
        out_specs.append(pl.BlockSpec((None, dil, tm // dil, d), lambda bi, i: (bi, 0, i, 0)))
    return pl.pallas_call(
        functools.partial(_norm_kernel, tm=tm, dilations=dilations),
        out_shape=out_shape,
        grid=(b, seq // tm),
        in_specs=[pl.BlockSpec((None, tm, d), lambda bi, i: (bi, i, 0)),
                  pl.BlockSpec((1, d), lambda bi, i: (0, 0))],
        out_specs=out_specs,
        scratch_shapes=[pltpu.VMEM((d // LANES, tm, LANES), F32)],
        compiler_params=_cparams(("parallel", "parallel")),
        name="rmsnorm",
    )(x3, w)


def _proj_kernel(x_ref, w_ref, o_ref):
    o_ref[...] = jnp.dot(x_ref[...], w_ref[...], preferred_element_type=F32).astype(o_ref.dtype)


def _proj_rest(xn, w):
    n, d = xn.shape
    c = w.shape[1]
    tm, tn = 1024, 1536
    return pl.pallas_call(
        _proj_kernel,
        out_shape=jax.ShapeDtypeStruct((n, c), BF16),
        grid=(n // tm, c // tn),
        in_specs=[pl.BlockSpec((tm, d), lambda i, j: (i, 0)),
                  pl.BlockSpec((d, tn), lambda i, j: (0, j))],
        out_specs=pl.BlockSpec((tm, tn), lambda i, j: (i, j)),
        compiler_params=_cparams(("parallel", "arbitrary")),
        name="proj_rest",
    )(xn, w)


def _proj_qkv_kernel(x_ref, w_ref, nw_ref, seg_ref, o_ref):
    j = pl.program_id(1)

    @pl.when(j < 2)
    def _():
        sub = 512
        units = [dict(rows=slice(r * sub, (r + 1) * sub)) for r in range(x_ref.shape[0] // sub)]

        def project(u):
            u["z"] = jnp.dot(x_ref[u["rows"], :], w_ref[...], preferred_element_type=F32)

        def square(u):
            u["zz"] = (u["z"] * u["z"]).astype(BF16)

        def mean_square(u):
            u["ms"] = jnp.dot(u.pop("zz"), seg_ref[...], preferred_element_type=F32)

        def normalise(u):
            o_ref[u["rows"], :] = (u.pop("z") * lax.rsqrt(u.pop("ms") + EPS) * nw_ref[...]
                                   ).astype(BF16)

        _emit_skewed(units, (project, square, mean_square, normalise))

    @pl.when(j == 2)
    def _():
        o_ref[...] = jnp.dot(x_ref[...], w_ref[...], preferred_element_type=F32).astype(BF16)


def _proj_qkv(xn2, w, nw, name):
    n, d = xn2.shape
    tm, tn = 2048, ATTN_WIDTH
    head = np.arange(tn) // HEAD_DIM
    seg = jnp.asarray((head[:, None] == head[None, :]) / HEAD_DIM, dtype=BF16)
    return pl.pallas_call(
        _proj_qkv_kernel,
        out_shape=jax.ShapeDtypeStruct((n, 3 * tn), BF16),
        grid=(n // tm, 3),
        in_specs=[pl.BlockSpec((tm, d), lambda i, j: (i, 0)),
                  pl.BlockSpec((d, tn), lambda i, j: (0, j)),
                  pl.BlockSpec((None, 1, tn), lambda i, j: (jnp.minimum(j, 1), 0, 0)),
                  pl.BlockSpec((tn, tn), lambda i, j: (0, 0))],
        out_specs=pl.BlockSpec((tm, tn), lambda i, j: (i, j)),
        compiler_params=_cparams(("parallel", "arbitrary")),
        name=name,
    )(xn2, w, nw, seg)


def _t5_bucket(rel):
    half = NUM_BUCKETS // 2
    max_exact = half // 2
    n = np.abs(rel)
    large = max_exact + (np.log(np.maximum(n, 1) / max_exact)
                         / np.log(REL_MAX_DISTANCE / max_exact) * (half - max_exact)).astype(np.int32)
    large = np.minimum(large, half - 1)
    return np.where(rel > 0, half, 0) + np.where(n < max_exact, n, large)


def _attn_bias(bias_tab, dilation):
    t = np.arange(ATTN_TQ)[:, None]
    j = np.arange(ATTN_TK)[None, :]
    rel = np.stack([j - off - t for off in (0, ATTN_SIDE, 2 * ATTN_SIDE)])
    valid = (np.abs(rel) <= ATTN_SIDE).reshape(-1)
    bucket = _t5_bucket(rel * dilation).reshape(-1).astype(np.int32)
    onehot = (jnp.asarray(bucket)[None, :] == jnp.arange(NUM_BUCKETS, dtype=jnp.int32)[:, None])
    vals = jnp.dot(bias_tab.T, onehot.astype(F32), precision=lax.Precision.HIGHEST)
    vals = jnp.where(jnp.asarray(valid)[None, :], vals, NEG_INF)
    return vals.reshape(HEADS_PER_GROUP, 3, ATTN_TQ, ATTN_TK)


def _attn_kernel(q_ref, k_ref, v_ref, bias_ref, o_ref, lse_ref, *, seq, n_sub, n_pairs, blocks):
    nb = seq // ATTN_TQ
    lane = lax.broadcasted_iota(jnp.int32, (ATTN_TQ, LANES), 1)
    first = lane < HEAD_DIM
    nt = (((1,), (1,)), ((), ()))

    def both_heads(col):
        return jnp.where(first, jnp.broadcast_to(col[:ATTN_TQ], (ATTN_TQ, LANES)),
                         jnp.broadcast_to(col[ATTN_TQ:], (ATTN_TQ, LANES)))

    def run(subs, i0):
        units = []
        for sub in subs:
            for bo in range(blocks):
                i = i0 + bo
                q0 = pl.multiple_of(i * ATTN_TQ, ATTN_TQ)
                ks = pl.multiple_of(jnp.clip(i * ATTN_TQ - ATTN_SIDE, 0, seq - ATTN_TK), ATTN_SIDE)
                var = jnp.where(i == 0, 0, jnp.where(i == nb - 1, 2, 1))
                for pr in range(n_pairs):
                    units.append(dict(sub=sub, q0=q0, ks=ks, var=var, pr=pr,
                                      cols=slice(pr * LANES, (pr + 1) * LANES)))

        def scores(u):
            q2 = q_ref[u["sub"], pl.ds(u["q0"], ATTN_TQ), u["cols"]]
            k2 = k_ref[u["sub"], pl.ds(u["ks"], ATTN_TK), u["cols"]]
            zero = jnp.zeros_like(q2)
            qq = jnp.concatenate([jnp.where(first, q2, zero), jnp.where(first, zero, q2)], axis=0)
            u["s"] = lax.dot_general(qq, k2, nt, preferred_element_type=F32)

        def row_max(u):
            pr, var = u["pr"], u["var"]
            u["s"] = u["s"] + jnp.concatenate([bias_ref[2 * pr, var], bias_ref[2 * pr + 1, var]],
                                              axis=0)
            u["m"] = jnp.max(u["s"], axis=-1, keepdims=True)

        def probs(u):
            p = jnp.exp2(u.pop("s") - u["m"])
            u["den"] = jnp.sum(p, axis=-1, keepdims=True)
            u["p"] = p.astype(BF16)

        def values(u):
            v2 = v_ref[u["sub"], pl.ds(u["ks"], ATTN_TK), u["cols"]]
            u["pv"] = jnp.dot(u.pop("p"), v2, preferred_element_type=F32)

        def finish(u):
            den = both_heads(u.pop("den"))
            pv = u.pop("pv")
            rows = pl.ds(u["q0"], ATTN_TQ)
            o_ref[u["sub"], rows, u["cols"]] = (jnp.where(first, pv[:ATTN_TQ], pv[ATTN_TQ:])
                                                * (1.0 / den)).astype(BF16)
            lse_ref[u["sub"], rows, u["cols"]] = (both_heads(u.pop("m"))
                                                  + jnp.log(den) * (1.0 / np.log(2.0)))

        _emit_skewed(units, (scores, row_max, probs, values, finish))

    if nb == blocks:
        run(range(n_sub), 0)
    else:
        def body(j, carry):
            run(range(n_sub), j * blocks)
            return carry

        lax.fori_loop(0, nb // blocks, body, 0)


def _attention(qkv, bias, name):
    n_seq, seq, _ = qkv.shape
    assert seq >= 2 * ATTN_TQ and seq % ATTN_TQ == 0
    if seq >= 4096:
        n_sub, n_pairs = 1, 2
    else:
        n_pairs = ATTN_WIDTH // LANES
        n_sub = max(1, min(n_seq, 1024 // seq))
    blocks = max(2, min(seq // ATTN_TQ, 16 // (n_sub * n_pairs)))
    assert (seq // ATTN_TQ) % blocks == 0
    cw = n_pairs * LANES
    pb = ATTN_WIDTH // cw
    kern = functools.partial(_attn_kernel, seq=seq, n_sub=n_sub, n_pairs=n_pairs, blocks=blocks)
    blk = (n_sub, seq, cw)
    return pl.pallas_call(
        kern,
        out_shape=(jax.ShapeDtypeStruct((n_seq, seq, ATTN_WIDTH), BF16),
                   jax.ShapeDtypeStruct((n_seq, seq, ATTN_WIDTH), F32)),
        grid=(n_seq // n_sub, pb),
        in_specs=[pl.BlockSpec(blk, lambda r, p: (r, 0, p)),
                  pl.BlockSpec(blk, lambda r, p: (r, 0, pb + p)),
                  pl.BlockSpec(blk, lambda r, p: (r, 0, 2 * pb + p)),
                  pl.BlockSpec((2 * n_pairs, 3, ATTN_TQ, ATTN_TK), lambda r, p: (p, 0, 0, 0))],
        out_specs=(pl.BlockSpec(blk, lambda r, p: (r, 0, p)),
                   pl.BlockSpec(blk, lambda r, p: (r, 0, p))),
        compiler_params=_cparams(("parallel", "parallel")),
        name=name,
    )(qkv, qkv, qkv, bias)


def _hgrn_mats():
    t = np.arange(HG_TILE)[:, None]
    u = np.arange(HG_TILE)[None, :]
    same = (t // HG_CHUNK) == (u // HG_CHUNK)
    fwd = same & (u <= t)
    bwd = same & (u >= t)
    return (jnp.asarray(fwd, dtype=BF16), jnp.asarray(bwd, dtype=BF16),
            jnp.asarray(-fwd.astype(np.int32)), jnp.asarray(-bwd.astype(np.int32)))


def _hgrn_kernel(q_ref, zf_ref, zb_ref, i_ref, g_ref, lbf_ref, lbb_ref, nw_ref, mf_ref, mb_ref,
                 lmask_ref, umask_ref, o_ref, acc_ref, qd_ref, kv_ref, dec_ref, st_ref, *, seq):
    n_tiles = seq // HG_TILE
    n_chunks = seq // HG_CHUNK
    cpt = HG_TILE // HG_CHUNK
    mid = HG_CHUNK // 2
    nt = (((1,), (1,)), ((), ()))

    def per_chunk_rows(rows):
        return jnp.concatenate([jnp.broadcast_to(r, (HG_CHUNK, HG_DK)) for r in rows], axis=0)

    scans = ((zf_ref, lbf_ref, mf_ref, False), (zb_ref, lbb_ref, mb_ref, True))

    def a_gate(u):
        u["kk"], u["ghl"] = [], []
        for z_ref, lb_ref, _, _ in scans:
            kk = (1.0 - lb_ref[...]) * jax.nn.sigmoid(-z_ref[u["rows"], :].astype(F32))
            g = jnp.log(1.0 - kk) * (1.0 / np.log(2.0))
            g_hi = g.astype(BF16)
            g_lo = (g - g_hi.astype(F32)).astype(BF16)
            u["kk"].append(kk)
            u["ghl"].append(jnp.concatenate([g_hi, g_lo], axis=1))

    def a_cumsum(u):
        u["sums"] = [jnp.dot(m_ref[...], ghl, preferred_element_type=F32)
                     for (_, _, m_ref, _), ghl in zip(scans, u.pop("ghl"))]

    def a_decay(u):
        q = q_ref[u["rows"], :].astype(F32)
        q_c, k_c, q_d, k_s, decs = [], [], [], [], []
        for (_, _, _, reverse), sums, kk in zip(scans, u.pop("sums"), u.pop("kk")):
            beta = sums[:, :HG_DK] + sums[:, HG_DK:]
            tot_row = 0 if reverse else HG_CHUNK - 1
            cen_row = mid if reverse else mid - 1
            tot = [beta[c * HG_CHUNK + tot_row:c * HG_CHUNK + tot_row + 1] for c in range(cpt)]
            cen = [beta[c * HG_CHUNK + cen_row:c * HG_CHUNK + cen_row + 1] for c in range(cpt)]
            d = beta - per_chunk_rows(cen)
            qc = q * jnp.exp2(d)
            kc = kk * jnp.exp2(-d)
            q_d.append((qc * per_chunk_rows([jnp.exp2(c_) for c_ in cen])).astype(BF16))
            k_s.append((kc * per_chunk_rows([jnp.exp2(t_ - c_) for t_, c_ in zip(tot, cen)])
                        ).astype(BF16))
            q_c.append(qc.astype(BF16))
            k_c.append(kc.astype(BF16))
            decs.append([jnp.exp2(t_) for t_ in tot])
        qd_ref[u["rows"], :] = jnp.concatenate(q_d, axis=1)
        u["ks"] = jnp.concatenate(k_s, axis=1)
        u["qc"], u["kc"] = q_c, k_c
        for c in range(cpt):
            dec_ref[u["t"] * cpt + c] = jnp.concatenate(
                [jnp.broadcast_to(decs[0][c], (8, HG_DK)), jnp.broadcast_to(decs[1][c], (8, HG_DK))],
                axis=1)

    def a_scores(u):
        u["a"] = [lax.dot_general(qc, kc, nt, preferred_element_type=F32)
                  for qc, kc in zip(u.pop("qc"), u.pop("kc"))]

    def a_mask(u):
        a_f, a_b = [pltpu.bitcast(a, jnp.int32) for a in u.pop("a")]
        u["a"] = (pltpu.bitcast(a_f & lmask_ref[...], F32)
                  + pltpu.bitcast(a_b & umask_ref[...], F32)).astype(BF16)

    def a_values(u):
        v = i_ref[u["rows"], :]
        acc_ref[u["rows"], :] = jnp.dot(u.pop("a"), v, preferred_element_type=F32)
        ks = u.pop("ks")
        for c in range(cpt):
            cr = slice(c * HG_CHUNK, (c + 1) * HG_CHUNK)
            kv_ref[u["t"] * cpt + c] = lax.dot_general(v[cr], ks[cr], (((0,), (0,)), ((), ())),
                                                       preferred_element_type=F32)

    a_stages = (a_gate, a_cumsum, a_decay, a_scores, a_mask, a_values)

    def phase_a(j, carry):
        units = []
        for k in range(HG_TILES_PER_TRIP):
            t = j * HG_TILES_PER_TRIP + k
            units.append(dict(t=t, rows=pl.ds(pl.multiple_of(t * HG_TILE, HG_TILE), HG_TILE)))
        _emit_skewed(units, a_stages)
        return carry

    lax.fori_loop(0, n_tiles // HG_TILES_PER_TRIP, phase_a, 0)

    fw = slice(0, HG_DK)
    bw = slice(HG_DK, 2 * HG_DK)

    def phase_b(n, carry):
        sf, sb = carry
        m = n_chunks - 1 - n
        st_ref[n, :, fw] = sf.astype(BF16)
        sf = dec_ref[n][0:1, fw] * sf + kv_ref[n, :, fw]
        st_ref[m, :, bw] = sb.astype(BF16)
        sb = dec_ref[m][0:1, bw] * sb + kv_ref[m, :, bw]
        return sf, sb

    s0 = jnp.zeros((HG_DK, HG_DK), F32)
    lax.fori_loop(0, n_chunks, phase_b, (s0, s0), unroll=4)

    def c_inter(u):
        u["inter"] = [lax.dot_general(qd_ref[pl.ds(u["r0"] + c * HG_CHUNK, HG_CHUNK), :],
                                      st_ref[u["t"] * cpt + c], nt, preferred_element_type=F32)
                      for c in range(cpt)]

    def c_sum(u):
        u["o"] = acc_ref[u["rows"], :] + jnp.concatenate(u.pop("inter"), axis=0)
        u["ms"] = jnp.mean(u["o"] * u["o"], axis=-1, keepdims=True)

    def c_out(u):
        gate = g_ref[u["rows"], :].astype(F32)
        o_ref[u["rows"], :] = (u.pop("o") * lax.rsqrt(u.pop("ms") + EPS) * nw_ref[...]
                               * (gate * jax.nn.sigmoid(gate))).astype(BF16)

    def phase_c(j, carry):
        units = []
        for k in range(HG_TILES_PER_TRIP):
            t = j * HG_TILES_PER_TRIP + k
            r0 = pl.multiple_of(t * HG_TILE, HG_TILE)
            units.append(dict(t=t, r0=r0, rows=pl.ds(r0, HG_TILE)))
        _emit_skewed(units, (c_inter, c_sum, c_out))
        return carry

    lax.fori_loop(0, n_tiles // HG_TILES_PER_TRIP, phase_c, 0)


def _hgrn(z_rest3, lb_f, lb_b, nw, col_q, col_zf, col_zb, col_i, col_g):
    b, seq, _ = z_rest3.shape
    mf, mb, lmask, umask = _hgrn_mats()
    n_chunks = seq // HG_CHUNK

    def zspec(col):
        return pl.BlockSpec((None, seq, HG_DK), lambda bi, h, col=col: (bi, 0, col + h))

    vec = pl.BlockSpec((1, HG_DK), lambda bi, h: (0, h))
    const = pl.BlockSpec((HG_TILE, HG_TILE), lambda bi, h: (0, 0))
    return pl.pallas_call(
        functools.partial(_hgrn_kernel, seq=seq),
        out_shape=jax.ShapeDtypeStruct((b, seq, HG_HEADS * HG_DK), BF16),
        grid=(b, HG_HEADS),
        in_specs=[zspec(col_q), zspec(col_zf), zspec(col_zb), zspec(col_i), zspec(col_g),
                  vec, vec, pl.BlockSpec((1, HG_DK), lambda bi, h: (0, 0)), const, const,
                  const, const],
        out_specs=pl.BlockSpec((None, seq, HG_DK), lambda bi, h: (bi, 0, h)),
        scratch_shapes=[pltpu.VMEM((seq, HG_DK), F32),
                        pltpu.VMEM((seq, 2 * HG_DK), BF16),
                        pltpu.VMEM((n_chunks, HG_DK, 2 * HG_DK), F32),
                        pltpu.VMEM((n_chunks, 8, 2 * HG_DK), F32),
                        pltpu.VMEM((n_chunks, HG_DK, 2 * HG_DK), BF16)],
        compiler_params=_cparams(("parallel", "parallel")),
        name="hgrn2",
    )(z_rest3, z_rest3, z_rest3, z_rest3, z_rest3, lb_f, lb_b, nw, mf, mb, lmask, umask)


def _final_kernel(o0_ref, l0_ref, o1_ref, l1_ref, o2_ref, l2_ref, ga_ref, ob_ref, za_ref, zb_ref,
                  x_ref, wa_ref, wb_ref, wo_ref, out_ref, so1_ref, sl1_ref, so2_ref, sl2_ref,
                  *, tm, dilations):
    for src, dst, dil in ((o1_ref, so1_ref, dilations[0]), (l1_ref, sl1_ref, dilations[0]),
                          (o2_ref, so2_ref, dilations[1]), (l2_ref, sl2_ref, dilations[1])):
        for r in range(dil):
            val = src[r].astype(F32)
            for c in range(ATTN_WIDTH // LANES):
                dst[c, pl.ds(r, tm // dil, stride=dil), :] = val[:, c * LANES:(c + 1) * LANES]

    def natural(ref, rows):
        return jnp.concatenate([ref[c, rows, :] for c in range(ATTN_WIDTH // LANES)], axis=1)

    def merge(u):
        rows = u["rows"]
        l0, l1, l2 = l0_ref[rows, :], natural(sl1_ref, rows), natural(sl2_ref, rows)
        mx = jnp.maximum(jnp.maximum(l0, l1), l2)
        w0, w1, w2 = jnp.exp2(l0 - mx), jnp.exp2(l1 - mx), jnp.exp2(l2 - mx)
        num = (w0 * o0_ref[rows, :].astype(F32) + w1 * natural(so1_ref, rows)
               + w2 * natural(so2_ref, rows))
        ga = ga_ref[rows, :].astype(F32)
        u["a"] = (num / (w0 + w1 + w2) * (ga * jax.nn.sigmoid(ga))).astype(BF16)

    def branch_proj(u):
        u["y_a"] = jnp.dot(u.pop("a"), wa_ref[...], preferred_element_type=F32)
        u["y_b"] = jnp.dot(ob_ref[u["rows"], :], wb_ref[...], preferred_element_type=F32)

    def gate(u):
        rows = u["rows"]
        u["merged"] = (jax.nn.sigmoid(za_ref[rows, :].astype(F32)) * u.pop("y_a")
                       + jax.nn.sigmoid(zb_ref[rows, :].astype(F32)) * u.pop("y_b")).astype(BF16)

    def out_proj(u):
        rows = u["rows"]
        out_ref[rows, :] = x_ref[rows, :] + jnp.dot(u.pop("merged"), wo_ref[...],
                                                    preferred_element_type=F32)

    tile = dict(rows=slice(0, tm))
    for stage in (merge, branch_proj, gate, out_proj):
        stage(tile)


def _final(o_g, lse_g, z_rest, o_b, x3, wa, wb, wo, dilations, col_ga, col_za, col_zb):
    b, seq, d = x3.shape
    tm = 512
    aw = ATTN_WIDTH

    def rows(width, col=0):
        return pl.BlockSpec((None, tm, width), lambda bi, i, col=col: (bi, i, col))

    def perm(dil):
        return pl.BlockSpec((None, dil, tm // dil, aw), lambda bi, i: (bi, 0, i, 0))

    def full(shape):
        return pl.BlockSpec(shape, lambda bi, i: (0, 0))

    d1, d2 = dilations
    return pl.pallas_call(
        functools.partial(_final_kernel, tm=tm, dilations=dilations),
        out_shape=jax.ShapeDtypeStruct((b, seq, d), F32),
        grid=(b, seq // tm),
        in_specs=[rows(aw), rows(aw), perm(d1), perm(d1), perm(d2), perm(d2),
                  rows(aw, col_ga), rows(d), rows(d, col_za), rows(d, col_zb), rows(d),
                  full(wa.shape), full(wb.shape), full(wo.shape)],
        out_specs=rows(d),
        scratch_shapes=[pltpu.VMEM((aw // LANES, tm, LANES), F32)] * 4,
        compiler_params=_cparams(("parallel", "parallel")),
        name="merge_out",
    )(o_g[0], lse_g[0], o_g[1], lse_g[1], o_g[2], lse_g[2], z_rest, o_b, z_rest, z_rest, x3,
      wa, wb, wo)


def kernel(x, norm_w, w_in, q_norm_w, k_norm_w, rel_bias, lb_fwd, lb_bwd, hg_norm_w,
           w_proj_a, w_proj_b, w_out):
    b, seq, d = x.shape
    n = b * seq
    layer = 0
    w = w_in[layer].astype(BF16)
    qkv_cols = 3 * len(ATTN_GROUPS) * ATTN_WIDTH
    dilations = tuple(dil for _, dil in ATTN_GROUPS)
    assert dilations[0] == 1

    xn_all = _norm(x.astype(F32), norm_w[layer].reshape(1, d).astype(F32), dilations[1:])
    xn = xn_all[0].reshape(n, d)

    w_rest = jnp.concatenate([w[:, qkv_cols + ATTN_WIDTH:], w[:, qkv_cols:qkv_cols + ATTN_WIDTH]],
                             axis=1)
    z_rest = _proj_rest(xn, w_rest).reshape(b, seq, -1)
    hw = HG_HEADS * HG_DK

    o_g, lse_g = [], []
    for g, dilation in enumerate(dilations):
        sub_len = seq // dilation
        wg = w[:, g * 3 * ATTN_WIDTH:(g + 1) * 3 * ATTN_WIDTH]
        q_scale = HEAD_DIM ** -0.5 * LOG2E
        nw = jnp.stack([jnp.tile(q_norm_w[layer, g].astype(F32), HEADS_PER_GROUP) * q_scale,
                        jnp.tile(k_norm_w[layer, g].astype(F32), HEADS_PER_GROUP)]
                       ).reshape(2, 1, ATTN_WIDTH)
        qkv = _proj_qkv(xn_all[g].reshape(n, d), wg, nw, f"proj_qkv_d{dilation}")
        heads = slice(g * HEADS_PER_GROUP, (g + 1) * HEADS_PER_GROUP)
        bias = _attn_bias(rel_bias.astype(F32)[:, heads] * LOG2E, dilation)
        o, lse = _attention(qkv.reshape(b * dilation, sub_len, 3 * ATTN_WIDTH), bias,
                            f"attn_d{dilation}")
        shape = (b, seq, ATTN_WIDTH) if dilation == 1 else (b, dilation, sub_len, ATTN_WIDTH)
        o_g.append(o.reshape(shape))
        lse_g.append(lse.reshape(shape))

    lb_f = jnp.cumsum(jax.nn.softmax(lb_fwd.astype(F32), axis=0), axis=0)[layer].reshape(1, hw)
    lb_b = jnp.cumsum(jax.nn.softmax(lb_bwd.astype(F32), axis=0), axis=0)[layer].reshape(1, hw)
    hb = hw // LANES
    o_b = _hgrn(z_rest, lb_f, lb_b, hg_norm_w[layer].reshape(1, HG_DK).astype(F32),
                col_q=0, col_zf=hb, col_zb=2 * hb, col_i=3 * hb, col_g=4 * hb)

    out = _final(o_g, lse_g, z_rest, o_b, x.astype(F32),
                 w_proj_a[layer].astype(BF16), w_proj_b[layer].astype(BF16),
                 w_out[layer].astype(BF16), dilations[1:],
                 col_ga=7 * hw // ATTN_WIDTH, col_za=5, col_zb=6)
    return out.astype(x.dtype)
```

```python
import functools

import numpy as np
import jax
import jax.numpy as jnp
from jax import lax
from jax.experimental import pallas as pl
from jax.experimental.pallas import tpu as pltpu

F32 = jnp.float32
BF16 = jnp.bfloat16

EPS = 1e-6
LOG2E = float(np.log2(np.e))
NEG_INF = -1e30
ATTN_GROUPS = ((128, 1), (512, 4), (2048, 16))
HEAD_DIM = 64
HEADS_PER_GROUP = 8
ATTN_WIDTH = HEADS_PER_GROUP * HEAD_DIM
NUM_BUCKETS = 32
REL_MAX_DISTANCE = 1024
HG_HEADS = 8
HG_DK = 128

LANES = 128
VMEM_LIMIT = 52 * 1024 * 1024

ATTN_TQ = 128
ATTN_SIDE = 64
ATTN_TK = ATTN_TQ + 2 * ATTN_SIDE

HG_CHUNK = 64
HG_TILE = 256
HG_TILES_PER_TRIP = 16


def _cparams(sem):
    return pltpu.CompilerParams(dimension_semantics=sem, vmem_limit_bytes=VMEM_LIMIT)


def _emit_skewed(units, stages):
    for step in range(len(units) + len(stages) - 1):
        for k, stage in enumerate(stages):
            if 0 <= step - k < len(units):
                stage(units[step - k])


def _norm_kernel(x_ref, w_ref, o_ref, *rest, tm, dilations):
    perm_refs, y_ref = rest[:-1], rest[-1]
    x = x_ref[...]
    ms = jnp.mean(x * x, axis=-1, keepdims=True)
    y = x * lax.rsqrt(ms + EPS) * w_ref[...]
    o_ref[...] = y.astype(BF16)
    n_lane_blocks = y.shape[1] // LANES
    for c in range(n_lane_blocks):
        y_ref[c] = y[:, c * LANES:(c + 1) * LANES]
    for p_ref, dil in zip(perm_refs, dilations):
        for r in range(dil):
            for c in range(n_lane_blocks):
                p_ref[r, :, c * LANES:(c + 1) * LANES] = (
                    y_ref[c, pl.ds(r, tm // dil, stride=dil), :].astype(BF16))


def _norm(x3, w, dilations):
    b, seq, d = x3.shape
    tm = 1024
    out_shape = [jax.ShapeDtypeStruct((b, seq, d), BF16)]
    out_specs = [pl.BlockSpec((None, tm, d), lambda bi, i: (bi, i, 0))]
    for dil in dilations:
        out_shape.append(jax.ShapeDtypeStruct((b, dil, seq // dil, d), BF16))
        out_specs.append(pl.BlockSpec((None, dil, tm // dil, d), lambda bi, i: (bi, 0, i, 0)))
    return pl.pallas_call(
        functools.partial(_norm_kernel, tm=tm, dilations=dilations),
        out_shape=out_shape,
        grid=(b, seq // tm),
        in_specs=[pl.BlockSpec((None, tm, d), lambda bi, i: (bi, i, 0)),
                  pl.BlockSpec((1, d), lambda bi, i: (0, 0))],
        out_specs=out_specs,
        scratch_shapes=[pltpu.VMEM((d // LANES, tm, LANES), F32)],
        compiler_params=_cparams(("parallel", "parallel")),
        name="rmsnorm",
    )(x3, w)


def _proj_kernel(x_ref, w_ref, o_ref):
    o_ref[...] = jnp.dot(x_ref[...], w_ref[...], preferred_element_type=F32).astype(o_ref.dtype)


def _proj_rest(xn, w):
    n, d = xn.shape
    c = w.shape[1]
    tm, tn = 2048, 1536
    return pl.pallas_call(
        _proj_kernel,
        out_shape=jax.ShapeDtypeStruct((n, c), BF16),
        grid=(n // tm, c // tn),
        in_specs=[pl.BlockSpec((tm, d), lambda i, j: (i, 0)),
                  pl.BlockSpec((d, tn), lambda i, j: (0, j))],
        out_specs=pl.BlockSpec((tm, tn), lambda i, j: (i, j)),
        compiler_params=_cparams(("parallel", "arbitrary")),
        name="proj_rest",
    )(xn, w)


def _proj_qkv_kernel(x_ref, w_ref, nw_ref, seg_ref, o_ref):
    j = pl.program_id(1)

    @pl.when(j < 2)
    def _():
        sub = 512
        units = [dict(rows=slice(r * sub, (r + 1) * sub)) for r in range(x_ref.shape[0] // sub)]

        def project(u):
            u["z"] = jnp.dot(x_ref[u["rows"], :], w_ref[...], preferred_element_type=F32)

        def square(u):
            u["zz"] = (u["z"] * u["z"]).astype(BF16)

        def mean_square(u):
            u["ms"] = jnp.dot(u.pop("zz"), seg_ref[...], preferred_element_type=F32)

        def normalise(u):
            o_ref[u["rows"], :] = (u.pop("z") * lax.rsqrt(u.pop("ms") + EPS) * nw_ref[...]
                                   ).astype(BF16)

        _emit_skewed(units, (project, square, mean_square, normalise))

    @pl.when(j == 2)
    def _():
        o_ref[...] = jnp.dot(x_ref[...], w_ref[...], preferred_element_type=F32).astype(BF16)


def _proj_qkv(xn2, w, nw, name):
    n, d = xn2.shape
    tm, tn = 2048, ATTN_WIDTH
    head = np.arange(tn) // HEAD_DIM
    seg = jnp.asarray((head[:, None] == head[None, :]) / HEAD_DIM, dtype=BF16)
    return pl.pallas_call(
        _proj_qkv_kernel,
        out_shape=jax.ShapeDtypeStruct((n, 3 * tn), BF16),
        grid=(n // tm, 3),
        in_specs=[pl.BlockSpec((tm, d), lambda i, j: (i, 0)),
                  pl.BlockSpec((d, tn), lambda i, j: (0, j)),
                  pl.BlockSpec((None, 1, tn), lambda i, j: (jnp.minimum(j, 1), 0, 0)),
                  pl.BlockSpec((tn, tn), lambda i, j: (0, 0))],
        out_specs=pl.BlockSpec((tm, tn), lambda i, j: (i, j)),
        compiler_params=_cparams(("parallel", "arbitrary")),
        name=name,
    )(xn2, w, nw, seg)


def _t5_bucket(rel):
    half = NUM_BUCKETS // 2
    max_exact = half // 2
    n = np.abs(rel)
    large = max_exact + (np.log(np.maximum(n, 1) / max_exact)
                         / np.log(REL_MAX_DISTANCE / max_exact) * (half - max_exact)).astype(np.int32)
    large = np.minimum(large, half - 1)
    return np.where(rel > 0, half, 0) + np.where(n < max_exact, n, large)


def _attn_bias(bias_tab, dilation):
    t = np.arange(ATTN_TQ)[:, None]
    j = np.arange(ATTN_TK)[None, :]
    rel = np.stack([j - off - t for off in (0, ATTN_SIDE, 2 * ATTN_SIDE)])
    valid = (np.abs(rel) <= ATTN_SIDE).reshape(-1)
    bucket = _t5_bucket(rel * dilation).reshape(-1).astype(np.int32)
    onehot = (jnp.asarray(bucket)[None, :] == jnp.arange(NUM_BUCKETS, dtype=jnp.int32)[:, None])
    vals = jnp.dot(bias_tab.T, onehot.astype(F32), precision=lax.Precision.HIGHEST)
    vals = jnp.where(jnp.asarray(valid)[None, :], vals, NEG_INF)
    return vals.reshape(HEADS_PER_GROUP, 3, ATTN_TQ, ATTN_TK)


def _attn_kernel(q_ref, k_ref, v_ref, bias_ref, o_ref, lse_ref, *, seq, n_sub, n_pairs, blocks):
    nb = seq // ATTN_TQ
    lane = lax.broadcasted_iota(jnp.int32, (ATTN_TQ, LANES), 1)
    first = lane < HEAD_DIM
    nt = (((1,), (1,)), ((), ()))

    def both_heads(col):
        return jnp.where(first, jnp.broadcast_to(col[:ATTN_TQ], (ATTN_TQ, LANES)),
                         jnp.broadcast_to(col[ATTN_TQ:], (ATTN_TQ, LANES)))

    def run(subs, i0):
        units = []
        for sub in subs:
            for bo in range(blocks):
                i = i0 + bo
                q0 = pl.multiple_of(i * ATTN_TQ, ATTN_TQ)
                ks = pl.multiple_of(jnp.clip(i * ATTN_TQ - ATTN_SIDE, 0, seq - ATTN_TK), ATTN_SIDE)
                var = jnp.where(i == 0, 0, jnp.where(i == nb - 1, 2, 1))
                for pr in range(n_pairs):
                    units.append(dict(sub=sub, q0=q0, ks=ks, var=var, pr=pr,
                                      cols=slice(pr * LANES, (pr + 1) * LANES)))

        def scores(u):
            q2 = q_ref[u["sub"], pl.ds(u["q0"], ATTN_TQ), u["cols"]]
            k2 = k_ref[u["sub"], pl.ds(u["ks"], ATTN_TK), u["cols"]]
            zero = jnp.zeros_like(q2)
            qq = jnp.concatenate([jnp.where(first, q2, zero), jnp.where(first, zero, q2)], axis=0)
            u["s"] = lax.dot_general(qq, k2, nt, preferred_element_type=F32)

        def row_max(u):
            pr, var = u["pr"], u["var"]
            u["s"] = u["s"] + jnp.concatenate([bias_ref[2 * pr, var], bias_ref[2 * pr + 1, var]],
                                              axis=0)
            u["m"] = jnp.max(u["s"], axis=-1, keepdims=True)

        def probs(u):
            p = jnp.exp2(u.pop("s") - u["m"])
            u["den"] = jnp.sum(p, axis=-1, keepdims=True)
            u["p"] = p.astype(BF16)

        def values(u):
            v2 = v_ref[u["sub"], pl.ds(u["ks"], ATTN_TK), u["cols"]]
            u["pv"] = jnp.dot(u.pop("p"), v2, preferred_element_type=F32)

        def finish(u):
            den = both_heads(u.pop("den"))
            pv = u.pop("pv")
            rows = pl.ds(u["q0"], ATTN_TQ)
            o_ref[u["sub"], rows, u["cols"]] = (jnp.where(first, pv[:ATTN_TQ], pv[ATTN_TQ:])
                                                * (1.0 / den)).astype(BF16)
            lse_ref[u["sub"], rows, u["cols"]] = (both_heads(u.pop("m"))
                                                  + jnp.log(den) * (1.0 / np.log(2.0)))

        _emit_skewed(units, (scores, row_max, probs, values, finish))

    if nb == blocks:
        run(range(n_sub), 0)
    else:
        def body(j, carry):
            run(range(n_sub), j * blocks)
            return carry

        lax.fori_loop(0, nb // blocks, body, 0)


def _attention(qkv, bias, name):
    n_seq, seq, _ = qkv.shape
    assert seq >= 2 * ATTN_TQ and seq % ATTN_TQ == 0
    if seq >= 4096:
        n_sub, n_pairs = 1, 2
    else:
        n_pairs = ATTN_WIDTH // LANES
        n_sub = max(1, min(n_seq, 1024 // seq))
    blocks = max(2, min(seq // ATTN_TQ, 16 // (n_sub * n_pairs)))
    assert (seq // ATTN_TQ) % blocks == 0
    cw = n_pairs * LANES
    pb = ATTN_WIDTH // cw
    kern = functools.partial(_attn_kernel, seq=seq, n_sub=n_sub, n_pairs=n_pairs, blocks=blocks)
    blk = (n_sub, seq, cw)
    return pl.pallas_call(
        kern,
        out_shape=(jax.ShapeDtypeStruct((n_seq, seq, ATTN_WIDTH), BF16),
                   jax.ShapeDtypeStruct((n_seq, seq, ATTN_WIDTH), F32)),
        grid=(n_seq // n_sub, pb),
        in_specs=[pl.BlockSpec(blk, lambda r, p: (r, 0, p)),
                  pl.BlockSpec(blk, lambda r, p: (r, 0, pb + p)),
                  pl.BlockSpec(blk, lambda r, p: (r, 0, 2 * pb + p)),
                  pl.BlockSpec((2 * n_pairs, 3, ATTN_TQ, ATTN_TK), lambda r, p: (p, 0, 0, 0))],
        out_specs=(pl.BlockSpec(blk, lambda r, p: (r, 0, p)),
                   pl.BlockSpec(blk, lambda r, p: (r, 0, p))),
        compiler_params=_cparams(("parallel", "parallel")),
        name=name,
    )(qkv, qkv, qkv, bias)


def _hgrn_mats():
    t = np.arange(HG_TILE)[:, None]
    u = np.arange(HG_TILE)[None, :]
    same = (t // HG_CHUNK) == (u // HG_CHUNK)
    fwd = same & (u <= t)
    bwd = same & (u >= t)
    return (jnp.asarray(fwd, dtype=BF16), jnp.asarray(bwd, dtype=BF16),
            jnp.asarray(-fwd.astype(np.int32)), jnp.asarray(-bwd.astype(np.int32)))


def _hgrn_kernel(q_ref, zf_ref, zb_ref, i_ref, g_ref, lbf_ref, lbb_ref, nw_ref, mf_ref, mb_ref,
                 lmask_ref, umask_ref, o_ref, acc_ref, qd_ref, kv_ref, dec_ref, st_ref, *, seq):
    n_tiles = seq // HG_TILE
    n_chunks = seq // HG_CHUNK
    cpt = HG_TILE // HG_CHUNK
    mid = HG_CHUNK // 2
    nt = (((1,), (1,)), ((), ()))

    def per_chunk_rows(rows):
        return jnp.concatenate([jnp.broadcast_to(r, (HG_CHUNK, HG_DK)) for r in rows], axis=0)

    scans = ((zf_ref, lbf_ref, mf_ref, False), (zb_ref, lbb_ref, mb_ref, True))

    def a_gate(u):
        u["kk"], u["ghl"] = [], []
        for z_ref, lb_ref, _, _ in scans:
            kk = (1.0 - lb_ref[...]) * jax.nn.sigmoid(-z_ref[u["rows"], :].astype(F32))
            g = jnp.log(1.0 - kk) * (1.0 / np.log(2.0))
            g_hi = g.astype(BF16)
            g_lo = (g - g_hi.astype(F32)).astype(BF16)
            u["kk"].append(kk)
            u["ghl"].append(jnp.concatenate([g_hi, g_lo], axis=1))

    def a_cumsum(u):
        u["sums"] = [jnp.dot(m_ref[...], ghl, preferred_element_type=F32)
                     for (_, _, m_ref, _), ghl in zip(scans, u.pop("ghl"))]

    def a_decay(u):
        q = q_ref[u["rows"], :].astype(F32)
        q_c, k_c, q_d, k_s, decs = [], [], [], [], []
        for (_, _, _, reverse), sums, kk in zip(scans, u.pop("sums"), u.pop("kk")):
            beta = sums[:, :HG_DK] + sums[:, HG_DK:]
            tot_row = 0 if reverse else HG_CHUNK - 1
            cen_row = mid if reverse else mid - 1
            tot = [beta[c * HG_CHUNK + tot_row:c * HG_CHUNK + tot_row + 1] for c in range(cpt)]
            cen = [beta[c * HG_CHUNK + cen_row:c * HG_CHUNK + cen_row + 1] for c in range(cpt)]
            d = beta - per_chunk_rows(cen)
            qc = q * jnp.exp2(d)
            kc = kk * jnp.exp2(-d)
            q_d.append((qc * per_chunk_rows([jnp.exp2(c_) for c_ in cen])).astype(BF16))
            k_s.append((kc * per_chunk_rows([jnp.exp2(t_ - c_) for t_, c_ in zip(tot, cen)])
                        ).astype(BF16))
            q_c.append(qc.astype(BF16))
            k_c.append(kc.astype(BF16))
            decs.append([jnp.exp2(t_) for t_ in tot])
        qd_ref[u["rows"], :] = jnp.concatenate(q_d, axis=1)
        u["ks"] = jnp.concatenate(k_s, axis=1)
        u["qc"], u["kc"] = q_c, k_c
        for c in range(cpt):
            dec_ref[u["t"] * cpt + c] = jnp.concatenate(
                [jnp.broadcast_to(decs[0][c], (8, HG_DK)), jnp.broadcast_to(decs[1][c], (8, HG_DK))],
                axis=1)

    def a_scores(u):
        u["a"] = [lax.dot_general(qc, kc, nt, preferred_element_type=F32)
                  for qc, kc in zip(u.pop("qc"), u.pop("kc"))]

    def a_mask(u):
        a_f, a_b = u.pop("a")
        u["a"] = (jnp.where(lmask_ref[...] != 0, a_f, 0.0)
                  + jnp.where(umask_ref[...] != 0, a_b, 0.0)).astype(BF16)

    def a_values(u):
        v = i_ref[u["rows"], :]
        acc_ref[u["rows"], :] = jnp.dot(u.pop("a"), v, preferred_element_type=F32)
        ks = u.pop("ks")
        for c in range(cpt):
            cr = slice(c * HG_CHUNK, (c + 1) * HG_CHUNK)
            kv_ref[u["t"] * cpt + c] = lax.dot_general(v[cr], ks[cr], (((0,), (0,)), ((), ())),
                                                       preferred_element_type=F32)

    a_stages = (a_gate, a_cumsum, a_decay, a_scores, a_mask, a_values)

    def phase_a(j, carry):
        units = []
        for k in range(HG_TILES_PER_TRIP):
            t = j * HG_TILES_PER_TRIP + k
            units.append(dict(t=t, rows=pl.ds(pl.multiple_of(t * HG_TILE, HG_TILE), HG_TILE)))
        _emit_skewed(units, a_stages)
        return carry

    lax.fori_loop(0, n_tiles // HG_TILES_PER_TRIP, phase_a, 0)

    fw = slice(0, HG_DK)
    bw = slice(HG_DK, 2 * HG_DK)

    def phase_b(n, carry):
        sf, sb = carry
        m = n_chunks - 1 - n
        st_ref[n, :, fw] = sf.astype(BF16)
        sf = dec_ref[n][0:1, fw] * sf + kv_ref[n, :, fw]
        st_ref[m, :, bw] = sb.astype(BF16)
        sb = dec_ref[m][0:1, bw] * sb + kv_ref[m, :, bw]
        return sf, sb

    s0 = jnp.zeros((HG_DK, HG_DK), F32)
    lax.fori_loop(0, n_chunks, phase_b, (s0, s0), unroll=4)

    def c_inter(u):
        u["inter"] = [lax.dot_general(qd_ref[pl.ds(u["r0"] + c * HG_CHUNK, HG_CHUNK), :],
                                      st_ref[u["t"] * cpt + c], nt, preferred_element_type=F32)
                      for c in range(cpt)]

    def c_sum(u):
        u["o"] = acc_ref[u["rows"], :] + jnp.concatenate(u.pop("inter"), axis=0)
        u["ms"] = jnp.mean(u["o"] * u["o"], axis=-1, keepdims=True)

    def c_out(u):
        gate = g_ref[u["rows"], :].astype(F32)
        o_ref[u["rows"], :] = (u.pop("o") * lax.rsqrt(u.pop("ms") + EPS) * nw_ref[...]
                               * (gate * jax.nn.sigmoid(gate))).astype(BF16)

    def phase_c(j, carry):
        units = []
        for k in range(HG_TILES_PER_TRIP):
            t = j * HG_TILES_PER_TRIP + k
            r0 = pl.multiple_of(t * HG_TILE, HG_TILE)
            units.append(dict(t=t, r0=r0, rows=pl.ds(r0, HG_TILE)))
        _emit_skewed(units, (c_inter, c_sum, c_out))
        return carry

    lax.fori_loop(0, n_tiles // HG_TILES_PER_TRIP, phase_c, 0)


def _hgrn(z_rest3, lb_f, lb_b, nw, col_q, col_zf, col_zb, col_i, col_g):
    b, seq, _ = z_rest3.shape
    mf, mb, lmask, umask = _hgrn_mats()
    n_chunks = seq // HG_CHUNK

    def zspec(col):
        return pl.BlockSpec((None, seq, HG_DK), lambda bi, h, col=col: (bi, 0, col + h))

    vec = pl.BlockSpec((1, HG_DK), lambda bi, h: (0, h))
    const = pl.BlockSpec((HG_TILE, HG_TILE), lambda bi, h: (0, 0))
    return pl.pallas_call(
        functools.partial(_hgrn_kernel, seq=seq),
        out_shape=jax.ShapeDtypeStruct((b, seq, HG_HEADS * HG_DK), BF16),
        grid=(b, HG_HEADS),
        in_specs=[zspec(col_q), zspec(col_zf), zspec(col_zb), zspec(col_i), zspec(col_g),
                  vec, vec, pl.BlockSpec((1, HG_DK), lambda bi, h: (0, 0)), const, const,
                  const, const],
        out_specs=pl.BlockSpec((None, seq, HG_DK), lambda bi, h: (bi, 0, h)),
        scratch_shapes=[pltpu.VMEM((seq, HG_DK), F32),
                        pltpu.VMEM((seq, 2 * HG_DK), BF16),
                        pltpu.VMEM((n_chunks, HG_DK, 2 * HG_DK), F32),
                        pltpu.VMEM((n_chunks, 8, 2 * HG_DK), F32),
                        pltpu.VMEM((n_chunks, HG_DK, 2 * HG_DK), BF16)],
        compiler_params=_cparams(("parallel", "parallel")),
        name="hgrn2",
    )(z_rest3, z_rest3, z_rest3, z_rest3, z_rest3, lb_f, lb_b, nw, mf, mb, lmask, umask)


def _final_kernel(o0_ref, l0_ref, o1_ref, l1_ref, o2_ref, l2_ref, ga_ref, ob_ref, za_ref, zb_ref,
                  x_ref, wa_ref, wb_ref, wo_ref, out_ref, so1_ref, sl1_ref, so2_ref, sl2_ref,
                  *, tm, dilations):
    for src, dst, dil in ((o1_ref, so1_ref, dilations[0]), (l1_ref, sl1_ref, dilations[0]),
                          (o2_ref, so2_ref, dilations[1]), (l2_ref, sl2_ref, dilations[1])):
        for r in range(dil):
            val = src[r].astype(F32)
            for c in range(ATTN_WIDTH // LANES):
                dst[c, pl.ds(r, tm // dil, stride=dil), :] = val[:, c * LANES:(c + 1) * LANES]

    def natural(ref, rows):
        return jnp.concatenate([ref[c, rows, :] for c in range(ATTN_WIDTH // LANES)], axis=1)

    def merge(u):
        rows = u["rows"]
        l0, l1, l2 = l0_ref[rows, :], natural(sl1_ref, rows), natural(sl2_ref, rows)
        mx = jnp.maximum(jnp.maximum(l0, l1), l2)
        w0, w1, w2 = jnp.exp2(l0 - mx), jnp.exp2(l1 - mx), jnp.exp2(l2 - mx)
        num = (w0 * o0_ref[rows, :].astype(F32) + w1 * natural(so1_ref, rows)
               + w2 * natural(so2_ref, rows))
        ga = ga_ref[rows, :].astype(F32)
        u["a"] = (num / (w0 + w1 + w2) * (ga * jax.nn.sigmoid(ga))).astype(BF16)

    def branch_proj(u):
        u["y_a"] = jnp.dot(u.pop("a"), wa_ref[...], preferred_element_type=F32)
        u["y_b"] = jnp.dot(ob_ref[u["rows"], :], wb_ref[...], preferred_element_type=F32)

    def gate(u):
        rows = u["rows"]
        u["merged"] = (jax.nn.sigmoid(za_ref[rows, :].astype(F32)) * u.pop("y_a")
                       + jax.nn.sigmoid(zb_ref[rows, :].astype(F32)) * u.pop("y_b")).astype(BF16)

    def out_proj(u):
        rows = u["rows"]
        out_ref[rows, :] = x_ref[rows, :] + jnp.dot(u.pop("merged"), wo_ref[...],
                                                    preferred_element_type=F32)

    tile = dict(rows=slice(0, tm))
    for stage in (merge, branch_proj, gate, out_proj):
        stage(tile)


def _final(o_g, lse_g, z_rest, o_b, x3, wa, wb, wo, dilations, col_ga, col_za, col_zb):
    b, seq, d = x3.shape
    tm = 512
    aw = ATTN_WIDTH

    def rows(width, col=0):
        return pl.BlockSpec((None, tm, width), lambda bi, i, col=col: (bi, i, col))

    def perm(dil):
        return pl.BlockSpec((None, dil, tm // dil, aw), lambda bi, i: (bi, 0, i, 0))

    def full(shape):
        return pl.BlockSpec(shape, lambda bi, i: (0, 0))

    d1, d2 = dilations
    return pl.pallas_call(
        functools.partial(_final_kernel, tm=tm, dilations=dilations),
        out_shape=jax.ShapeDtypeStruct((b, seq, d), F32),
        grid=(b, seq // tm),
        in_specs=[rows(aw), rows(aw), perm(d1), perm(d1), perm(d2), perm(d2),
                  rows(aw, col_ga), rows(d), rows(d, col_za), rows(d, col_zb), rows(d),
                  full(wa.shape), full(wb.shape), full(wo.shape)],
        out_specs=rows(d),
        scratch_shapes=[pltpu.VMEM((aw // LANES, tm, LANES), F32)] * 4,
        compiler_params=_cparams(("parallel", "parallel")),
        name="merge_out",
    )(o_g[0], lse_g[0], o_g[1], lse_g[1], o_g[2], lse_g[2], z_rest, o_b, z_rest, z_rest, x3,
      wa, wb, wo)


def kernel(x, norm_w, w_in, q_norm_w, k_norm_w, rel_bias, lb_fwd, lb_bwd, hg_norm_w,
           w_proj_a, w_proj_b, w_out):
    b, seq, d = x.shape
    n = b * seq
    layer = 0
    w = w_in[layer].astype(BF16)
    qkv_cols = 3 * len(ATTN_GROUPS) * ATTN_WIDTH
    dilations = tuple(dil for _, dil in ATTN_GROUPS)
    assert dilations[0] == 1

    xn_all = _norm(x.astype(F32), norm_w[layer].reshape(1, d).astype(F32), dilations[1:])
    xn = xn_all[0].reshape(n, d)

    w_rest = jnp.concatenate([w[:, qkv_cols + ATTN_WIDTH:], w[:, qkv_cols:qkv_cols + ATTN_WIDTH]],
                             axis=1)
    z_rest = _proj_rest(xn, w_rest).reshape(b, seq, -1)
    hw = HG_HEADS * HG_DK

    o_g, lse_g = [], []
    for g, dilation in enumerate(dilations):
        sub_len = seq // dilation
        wg = w[:, g * 3 * ATTN_WIDTH:(g + 1) * 3 * ATTN_WIDTH]
        q_scale = HEAD_DIM ** -0.5 * LOG2E
        nw = jnp.stack([jnp.tile(q_norm_w[layer, g].astype(F32), HEADS_PER_GROUP) * q_scale,
                        jnp.tile(k_norm_w[layer, g].astype(F32), HEADS_PER_GROUP)]
                       ).reshape(2, 1, ATTN_WIDTH)
        qkv = _proj_qkv(xn_all[g].reshape(n, d), wg, nw, f"proj_qkv_d{dilation}")
        heads = slice(g * HEADS_PER_GROUP, (g + 1) * HEADS_PER_GROUP)
        bias = _attn_bias(rel_bias.astype(F32)[:, heads] * LOG2E, dilation)
        o, lse = _attention(qkv.reshape(b * dilation, sub_len, 3 * ATTN_WIDTH), bias,
                            f"attn_d{dilation}")
        shape = (b, seq, ATTN_WIDTH) if dilation == 1 else (b, dilation, sub_len, ATTN_WIDTH)
        o_g.append(o.reshape(shape))
        lse_g.append(lse.reshape(shape))

    lb_f = jnp.cumsum(jax.nn.softmax(lb_fwd.astype(F32), axis=0), axis=0)[layer].reshape(1, hw)
    lb_b = jnp.cumsum(jax.nn.softmax(lb_bwd.astype(F32), axis=0), axis=0)[layer].reshape(1, hw)
    hb = hw // LANES
    o_b = _hgrn(z_rest, lb_f, lb_b, hg_norm_w[layer].reshape(1, HG_DK).astype(F32),
                col_q=0, col_zf=hb, col_zb=2 * hb, col_i=3 * hb, col_g=4 * hb)

    out = _final(o_g, lse_g, z_rest, o_b, x.astype(F32),
                 w_proj_a[layer].astype(BF16), w_proj_b[layer].astype(BF16),
                 w_out[layer].astype(BF16), dilations[1:],
                 col_ga=7 * hw // ATTN_WIDTH, col_za=5, col_zb=6)
    return out.astype(x.dtype)
```

```python
import functools

import numpy as np
import jax
import jax.numpy as jnp
from jax import lax
from jax.experimental import pallas as pl
from jax.experimental.pallas import tpu as pltpu

F32 = jnp.float32
BF16 = jnp.bfloat16

EPS = 1e-6
LOG2E = float(np.log2(np.e))
NEG_INF = -1e30
ATTN_GROUPS = ((128, 1), (512, 4), (2048, 16))
HEAD_DIM = 64
HEADS_PER_GROUP = 8
ATTN_WIDTH = HEADS_PER_GROUP * HEAD_DIM
NUM_BUCKETS = 32
REL_MAX_DISTANCE = 1024
HG_HEADS = 8
HG_DK = 128

LANES = 128
VMEM_LIMIT = 52 * 1024 * 1024

ATTN_TQ = 128
ATTN_SIDE = 64
ATTN_TK = ATTN_TQ + 2 * ATTN_SIDE

HG_CHUNK = 64
HG_TILE = 256
HG_TILES_PER_TRIP = 16


def _cparams(sem):
    return pltpu.CompilerParams(dimension_semantics=sem, vmem_limit_bytes=VMEM_LIMIT)


def _emit_skewed(units, stages):
    for step in range(len(units) + len(stages) - 1):
        for k, stage in enumerate(stages):
            if 0 <= step - k < len(units):
                stage(units[step - k])


def _norm_kernel(x_ref, w_ref, o_ref, *rest, tm, dilations):
    perm_refs, y_ref = rest[:-1], rest[-1]
    x = x_ref[...]
    ms = jnp.mean(x * x, axis=-1, keepdims=True)
    y = x * lax.rsqrt(ms + EPS) * w_ref[...]
    o_ref[...] = y.astype(BF16)
    n_lane_blocks = y.shape[1] // LANES
    for c in range(n_lane_blocks):
        y_ref[c] = y[:, c * LANES:(c + 1) * LANES]
    for p_ref, dil in zip(perm_refs, dilations):
        for r in range(dil):
            for c in range(n_lane_blocks):
                p_ref[r, :, c * LANES:(c + 1) * LANES] = (
                    y_ref[c, pl.ds(r, tm // dil, stride=dil), :].astype(BF16))


def _norm(x3, w, dilations):
    b, seq, d = x3.shape
    tm = 1024
    out_shape = [jax.ShapeDtypeStruct((b, seq, d), BF16)]
    out_specs = [pl.BlockSpec((None, tm, d), lambda bi, i: (bi, i, 0))]
    for dil in dilations:
        out_shape.append(jax.ShapeDtypeStruct((b, dil, seq // dil, d), BF16))
        out_specs.append(pl.BlockSpec((None, dil, tm // dil, d), lambda bi, i: (bi, 0, i, 0)))
    return pl.pallas_call(
        functools.partial(_norm_kernel, tm=tm, dilations=dilations),
        out_shape=out_shape,
        grid=(b, seq // tm),
        in_specs=[pl.BlockSpec((None, tm, d), lambda bi, i: (bi, i, 0)),
                  pl.BlockSpec((1, d), lambda bi, i: (0, 0))],
        out_specs=out_specs,
        scratch_shapes=[pltpu.VMEM((d // LANES, tm, LANES), F32)],
        compiler_params=_cparams(("parallel", "parallel")),
        name="rmsnorm",
    )(x3, w)


def _proj_kernel(x_ref, w_ref, o_ref):
    o_ref[...] = jnp.dot(x_ref[...], w_ref[...], preferred_element_type=F32).astype(o_ref.dtype)


def _proj_rest(xn, w):
    n, d = xn.shape
    c = w.shape[1]
    tm, tn = 2048, 1536
    return pl.pallas_call(
        _proj_kernel,
        out_shape=jax.ShapeDtypeStruct((n, c), BF16),
        grid=(n // tm, c // tn),
        in_specs=[pl.BlockSpec((tm, d), lambda i, j: (i, 0)),
                  pl.BlockSpec((d, tn), lambda i, j: (0, j))],
        out_specs=pl.BlockSpec((tm, tn), lambda i, j: (i, j)),
        compiler_params=_cparams(("parallel", "arbitrary")),
        name="proj_rest",
    )(xn, w)


def _proj_qkv_kernel(x_ref, w_ref, nw_ref, seg_ref, o_ref):
    j = pl.program_id(1)

    @pl.when(j < 2)
    def _():
        sub = 512
        units = [dict(rows=slice(r * sub, (r + 1) * sub)) for r in range(x_ref.shape[0] // sub)]

        def project(u):
            u["z"] = jnp.dot(x_ref[u["rows"], :], w_ref[...], preferred_element_type=F32)

        def square(u):
            u["zz"] = (u["z"] * u["z"]).astype(BF16)

        def mean_square(u):
            u["ms"] = jnp.dot(u.pop("zz"), seg_ref[...], preferred_element_type=F32)

        def normalise(u):
            o_ref[u["rows"], :] = (u.pop("z") * lax.rsqrt(u.pop("ms") + EPS) * nw_ref[...]
                                   ).astype(BF16)

        _emit_skewed(units, (project, square, mean_square, normalise))

    @pl.when(j == 2)
    def _():
        o_ref[...] = jnp.dot(x_ref[...], w_ref[...], preferred_element_type=F32).astype(BF16)


def _proj_qkv(xn2, w, nw, name):
    n, d = xn2.shape
    tm, tn = 2048, ATTN_WIDTH
    head = np.arange(tn) // HEAD_DIM
    seg = jnp.asarray((head[:, None] == head[None, :]) / HEAD_DIM, dtype=BF16)
    return pl.pallas_call(
        _proj_qkv_kernel,
        out_shape=jax.ShapeDtypeStruct((n, 3 * tn), BF16),
        grid=(n // tm, 3),
        in_specs=[pl.BlockSpec((tm, d), lambda i, j: (i, 0)),
                  pl.BlockSpec((d, tn), lambda i, j: (0, j)),
                  pl.BlockSpec((None, 1, tn), lambda i, j: (jnp.minimum(j, 1), 0, 0)),
                  pl.BlockSpec((tn, tn), lambda i, j: (0, 0))],
        out_specs=pl.BlockSpec((tm, tn), lambda i, j: (i, j)),
        compiler_params=_cparams(("parallel", "arbitrary")),
        name=name,
    )(xn2, w, nw, seg)


def _t5_bucket(rel):
    half = NUM_BUCKETS // 2
    max_exact = half // 2
    n = np.abs(rel)
    large = max_exact + (np.log(np.maximum(n, 1) / max_exact)
                         / np.log(REL_MAX_DISTANCE / max_exact) * (half - max_exact)).astype(np.int32)
    large = np.minimum(large, half - 1)
    return np.where(rel > 0, half, 0) + np.where(n < max_exact, n, large)


def _attn_bias(bias_tab, dilation):
    t = np.arange(ATTN_TQ)[:, None]
    j = np.arange(ATTN_TK)[None, :]
    rel = np.stack([j - off - t for off in (0, ATTN_SIDE, 2 * ATTN_SIDE)])
    valid = (np.abs(rel) <= ATTN_SIDE).reshape(-1)
    bucket = _t5_bucket(rel * dilation).reshape(-1).astype(np.int32)
    onehot = (jnp.asarray(bucket)[None, :] == jnp.arange(NUM_BUCKETS, dtype=jnp.int32)[:, None])
    vals = jnp.dot(bias_tab.T, onehot.astype(F32), precision=lax.Precision.HIGHEST)
    vals = jnp.where(jnp.asarray(valid)[None, :], vals, NEG_INF)
    return vals.reshape(HEADS_PER_GROUP, 3, ATTN_TQ, ATTN_TK)


def _attn_kernel(q_ref, k_ref, v_ref, bias_ref, o_ref, lse_ref, *, seq, n_sub, n_pairs, blocks):
    nb = seq // ATTN_TQ
    lane = lax.broadcasted_iota(jnp.int32, (ATTN_TQ, LANES), 1)
    first = lane < HEAD_DIM
    nt = (((1,), (1,)), ((), ()))

    def both_heads(col):
        return jnp.where(first, jnp.broadcast_to(col[:ATTN_TQ], (ATTN_TQ, LANES)),
                         jnp.broadcast_to(col[ATTN_TQ:], (ATTN_TQ, LANES)))

    def run(subs, i0):
        units = []
        for sub in subs:
            for bo in range(blocks):
                i = i0 + bo
                q0 = pl.multiple_of(i * ATTN_TQ, ATTN_TQ)
                ks = pl.multiple_of(jnp.clip(i * ATTN_TQ - ATTN_SIDE, 0, seq - ATTN_TK), ATTN_SIDE)
                var = jnp.where(i == 0, 0, jnp.where(i == nb - 1, 2, 1))
                for pr in range(n_pairs):
                    units.append(dict(sub=sub, q0=q0, ks=ks, var=var, pr=pr,
                                      cols=slice(pr * LANES, (pr + 1) * LANES)))

        def scores(u):
            q2 = q_ref[u["sub"], pl.ds(u["q0"], ATTN_TQ), u["cols"]]
            k2 = k_ref[u["sub"], pl.ds(u["ks"], ATTN_TK), u["cols"]]
            zero = jnp.zeros_like(q2)
            qq = jnp.concatenate([jnp.where(first, q2, zero), jnp.where(first, zero, q2)], axis=0)
            u["s"] = lax.dot_general(qq, k2, nt, preferred_element_type=F32)

        def row_max(u):
            pr, var = u["pr"], u["var"]
            u["s"] = u["s"] + jnp.concatenate([bias_ref[2 * pr, var], bias_ref[2 * pr + 1, var]],
                                              axis=0)
            u["m"] = jnp.max(u["s"], axis=-1, keepdims=True)

        def probs(u):
            p = jnp.exp2(u.pop("s") - u["m"])
            u["den"] = jnp.sum(p, axis=-1, keepdims=True)
            u["p"] = p.astype(BF16)

        def values(u):
            v2 = v_ref[u["sub"], pl.ds(u["ks"], ATTN_TK), u["cols"]]
            u["pv"] = jnp.dot(u.pop("p"), v2, preferred_element_type=F32)

        def finish(u):
            den = both_heads(u.pop("den"))
            pv = u.pop("pv")
            rows = pl.ds(u["q0"], ATTN_TQ)
            o_ref[u["sub"], rows, u["cols"]] = (jnp.where(first, pv[:ATTN_TQ], pv[ATTN_TQ:])
                                                * (1.0 / den)).astype(BF16)
            lse_ref[u["sub"], rows, u["cols"]] = (both_heads(u.pop("m"))
                                                  + jnp.log(den) * (1.0 / np.log(2.0)))

        _emit_skewed(units, (scores, row_max, probs, values, finish))

    if nb == blocks:
        run(range(n_sub), 0)
    else:
        def body(j, carry):
            run(range(n_sub), j * blocks)
            return carry

        lax.fori_loop(0, nb // blocks, body, 0)


def _attention(qkv, bias, name):
    n_seq, seq, _ = qkv.shape
    assert seq >= 2 * ATTN_TQ and seq % ATTN_TQ == 0
    if seq >= 4096:
        n_sub, n_pairs = 1, 2
    else:
        n_pairs = ATTN_WIDTH // LANES
        n_sub = max(1, min(n_seq, 1024 // seq))
    blocks = max(2, min(seq // ATTN_TQ, 16 // (n_sub * n_pairs)))
    assert (seq // ATTN_TQ) % blocks == 0
    cw = n_pairs * LANES
    pb = ATTN_WIDTH // cw
    kern = functools.partial(_attn_kernel, seq=seq, n_sub=n_sub, n_pairs=n_pairs, blocks=blocks)
    blk = (n_sub, seq, cw)
    return pl.pallas_call(
        kern,
        out_shape=(jax.ShapeDtypeStruct((n_seq, seq, ATTN_WIDTH), BF16),
                   jax.ShapeDtypeStruct((n_seq, seq, ATTN_WIDTH), F32)),
        grid=(n_seq // n_sub, pb),
        in_specs=[pl.BlockSpec(blk, lambda r, p: (r, 0, p)),
                  pl.BlockSpec(blk, lambda r, p: (r, 0, pb + p)),
                  pl.BlockSpec(blk, lambda r, p: (r, 0, 2 * pb + p)),
                  pl.BlockSpec((2 * n_pairs, 3, ATTN_TQ, ATTN_TK), lambda r, p: (p, 0, 0, 0))],
        out_specs=(pl.BlockSpec(blk, lambda r, p: (r, 0, p)),
                   pl.BlockSpec(blk, lambda r, p: (r, 0, p))),
        compiler_params=_cparams(("parallel", "parallel")),
        name=name,
    )(qkv, qkv, qkv, bias)


def _hgrn_mats():
    t = np.arange(HG_TILE)[:, None]
    u = np.arange(HG_TILE)[None, :]
    same = (t // HG_CHUNK) == (u // HG_CHUNK)
    return (jnp.asarray(same & (u <= t), dtype=BF16), jnp.asarray(same & (u >= t), dtype=BF16))


def _hgrn_kernel(q_ref, zf_ref, zb_ref, i_ref, g_ref, lbf_ref, lbb_ref, nw_ref, mf_ref, mb_ref,
                 o_ref, acc_ref, qd_ref, kv_ref, dec_ref, st_ref, *, seq):
    n_tiles = seq // HG_TILE
    n_chunks = seq // HG_CHUNK
    cpt = HG_TILE // HG_CHUNK
    mid = HG_CHUNK // 2
    nt = (((1,), (1,)), ((), ()))

    def per_chunk_rows(rows):
        return jnp.concatenate([jnp.broadcast_to(r, (HG_CHUNK, HG_DK)) for r in rows], axis=0)

    scans = ((zf_ref, lbf_ref, mf_ref, False), (zb_ref, lbb_ref, mb_ref, True))

    def a_gate(u):
        u["kk"], u["ghl"] = [], []
        for z_ref, lb_ref, _, _ in scans:
            kk = (1.0 - lb_ref[...]) / (1.0 + jnp.exp(z_ref[u["rows"], :].astype(F32)))
            g = jnp.log(1.0 - kk) * (1.0 / np.log(2.0))
            g_hi = g.astype(BF16)
            g_lo = (g - g_hi.astype(F32)).astype(BF16)
            u["kk"].append(kk)
            u["ghl"].append(jnp.concatenate([g_hi, g_lo], axis=1))

    def a_cumsum(u):
        u["sums"] = [jnp.dot(m_ref[...], ghl, preferred_element_type=F32)
                     for (_, _, m_ref, _), ghl in zip(scans, u.pop("ghl"))]

    def a_decay(u):
        q = q_ref[u["rows"], :].astype(F32)
        q_c, k_c, q_d, k_s, decs = [], [], [], [], []
        for (_, _, _, reverse), sums, kk in zip(scans, u.pop("sums"), u.pop("kk")):
            beta = sums[:, :HG_DK] + sums[:, HG_DK:]
            tot_row = 0 if reverse else HG_CHUNK - 1
            cen_row = mid if reverse else mid - 1
            tot = [beta[c * HG_CHUNK + tot_row:c * HG_CHUNK + tot_row + 1] for c in range(cpt)]
            cen = [beta[c * HG_CHUNK + cen_row:c * HG_CHUNK + cen_row + 1] for c in range(cpt)]
            d = beta - per_chunk_rows(cen)
            e_q = jnp.exp2(d)
            qc = (q * e_q).astype(BF16)
            kc = (kk * (1.0 / e_q)).astype(BF16)
            q_d.append(qc * per_chunk_rows([jnp.exp2(c_).astype(BF16) for c_ in cen]))
            k_s.append(kc * per_chunk_rows([jnp.exp2(t_ - c_).astype(BF16)
                                            for t_, c_ in zip(tot, cen)]))
            q_c.append(qc)
            k_c.append(kc)
            decs.append([jnp.exp2(t_) for t_ in tot])
        qd_ref[u["rows"], :] = jnp.concatenate(q_d, axis=1)
        u["ks"] = jnp.concatenate(k_s, axis=1)
        u["qc"], u["kc"] = q_c, k_c
        for c in range(cpt):
            dec_ref[u["t"] * cpt + c] = jnp.concatenate(
                [jnp.broadcast_to(decs[0][c], (8, HG_DK)), jnp.broadcast_to(decs[1][c], (8, HG_DK))],
                axis=1)

    chunk_rows = [slice(c * HG_CHUNK, (c + 1) * HG_CHUNK) for c in range(cpt)]
    ti = lax.broadcasted_iota(jnp.int32, (HG_CHUNK, HG_CHUNK), 0)
    ui = lax.broadcasted_iota(jnp.int32, (HG_CHUNK, HG_CHUNK), 1)
    past, future = ui <= ti, ui >= ti

    def a_scores(u):
        u["a"] = [[lax.dot_general(qc[cr], kc[cr], nt, preferred_element_type=F32)
                   for qc, kc in zip(u["qc"], u["kc"])] for cr in chunk_rows]
        del u["qc"], u["kc"]

    def a_mask(u):
        u["a"] = [(jnp.where(past, a_f, 0.0) + jnp.where(future, a_b, 0.0)).astype(BF16)
                  for a_f, a_b in u.pop("a")]

    def a_values(u):
        v = i_ref[u["rows"], :]
        acc_ref[u["rows"], :] = jnp.concatenate(
            [jnp.dot(a, v[cr], preferred_element_type=F32) for a, cr in zip(u.pop("a"), chunk_rows)],
            axis=0)
        ks = u.pop("ks")
        for c, cr in enumerate(chunk_rows):
            kv_ref[u["t"] * cpt + c] = lax.dot_general(v[cr], ks[cr], (((0,), (0,)), ((), ())),
                                                       preferred_element_type=F32)

    a_stages = (a_gate, a_cumsum, a_decay, a_scores, a_mask, a_values)

    def phase_a(j, carry):
        units = []
        for k in range(HG_TILES_PER_TRIP):
            t = j * HG_TILES_PER_TRIP + k
            units.append(dict(t=t, rows=pl.ds(pl.multiple_of(t * HG_TILE, HG_TILE), HG_TILE)))
        _emit_skewed(units, a_stages)
        return carry

    lax.fori_loop(0, n_tiles // HG_TILES_PER_TRIP, phase_a, 0)

    fw = slice(0, HG_DK)
    bw = slice(HG_DK, 2 * HG_DK)

    def phase_b(n, carry):
        sf, sb = carry
        m = n_chunks - 1 - n
        st_ref[n, :, fw] = sf.astype(BF16)
        sf = dec_ref[n][0:1, fw] * sf + kv_ref[n, :, fw]
        st_ref[m, :, bw] = sb.astype(BF16)
        sb = dec_ref[m][0:1, bw] * sb + kv_ref[m, :, bw]
        return sf, sb

    s0 = jnp.zeros((HG_DK, HG_DK), F32)
    lax.fori_loop(0, n_chunks, phase_b, (s0, s0), unroll=4)

    def c_inter(u):
        u["inter"] = [lax.dot_general(qd_ref[pl.ds(u["r0"] + c * HG_CHUNK, HG_CHUNK), :],
                                      st_ref[u["t"] * cpt + c], nt, preferred_element_type=F32)
                      for c in range(cpt)]

    def c_sum(u):
        u["o"] = acc_ref[u["rows"], :] + jnp.concatenate(u.pop("inter"), axis=0)
        u["ms"] = jnp.mean(u["o"] * u["o"], axis=-1, keepdims=True)

    def c_out(u):
        gate = g_ref[u["rows"], :].astype(F32)
        o_ref[u["rows"], :] = (u.pop("o") * lax.rsqrt(u.pop("ms") + EPS) * nw_ref[...]
                               * (gate * jax.nn.sigmoid(gate))).astype(BF16)

    def phase_c(j, carry):
        units = []
        for k in range(HG_TILES_PER_TRIP):
            t = j * HG_TILES_PER_TRIP + k
            r0 = pl.multiple_of(t * HG_TILE, HG_TILE)
            units.append(dict(t=t, r0=r0, rows=pl.ds(r0, HG_TILE)))
        _emit_skewed(units, (c_inter, c_sum, c_out))
        return carry

    lax.fori_loop(0, n_tiles // HG_TILES_PER_TRIP, phase_c, 0)


def _hgrn(z_rest3, lb_f, lb_b, nw, col_q, col_zf, col_zb, col_i, col_g):
    b, seq, _ = z_rest3.shape
    mf, mb = _hgrn_mats()
    n_chunks = seq // HG_CHUNK

    def zspec(col):
        return pl.BlockSpec((None, seq, HG_DK), lambda bi, h, col=col: (bi, 0, col + h))

    vec = pl.BlockSpec((1, HG_DK), lambda bi, h: (0, h))
    const = pl.BlockSpec((HG_TILE, HG_TILE), lambda bi, h: (0, 0))
    return pl.pallas_call(
        functools.partial(_hgrn_kernel, seq=seq),
        out_shape=jax.ShapeDtypeStruct((b, seq, HG_HEADS * HG_DK), BF16),
        grid=(b, HG_HEADS),
        in_specs=[zspec(col_q), zspec(col_zf), zspec(col_zb), zspec(col_i), zspec(col_g),
                  vec, vec, pl.BlockSpec((1, HG_DK), lambda bi, h: (0, 0)), const, const],
        out_specs=pl.BlockSpec((None, seq, HG_DK), lambda bi, h: (bi, 0, h)),
        scratch_shapes=[pltpu.VMEM((seq, HG_DK), F32),
                        pltpu.VMEM((seq, 2 * HG_DK), BF16),
                        pltpu.VMEM((n_chunks, HG_DK, 2 * HG_DK), F32),
                        pltpu.VMEM((n_chunks, 8, 2 * HG_DK), F32),
                        pltpu.VMEM((n_chunks, HG_DK, 2 * HG_DK), BF16)],
        compiler_params=_cparams(("parallel", "parallel")),
        name="hgrn2",
    )(z_rest3, z_rest3, z_rest3, z_rest3, z_rest3, lb_f, lb_b, nw, mf, mb)


def _final_kernel(o0_ref, l0_ref, o1_ref, l1_ref, o2_ref, l2_ref, ga_ref, ob_ref, za_ref, zb_ref,
                  x_ref, wa_ref, wb_ref, wo_ref, out_ref, so1_ref, sl1_ref, so2_ref, sl2_ref,
                  *, tm, dilations):
    for src, dst, dil in ((o1_ref, so1_ref, dilations[0]), (l1_ref, sl1_ref, dilations[0]),
                          (o2_ref, so2_ref, dilations[1]), (l2_ref, sl2_ref, dilations[1])):
        for r in range(dil):
            val = src[r].astype(F32)
            for c in range(ATTN_WIDTH // LANES):
                dst[c, pl.ds(r, tm // dil, stride=dil), :] = val[:, c * LANES:(c + 1) * LANES]

    def natural(ref, rows):
        return jnp.concatenate([ref[c, rows, :] for c in range(ATTN_WIDTH // LANES)], axis=1)

    def merge(u):
        rows = u["rows"]
        l0, l1, l2 = l0_ref[rows, :], natural(sl1_ref, rows), natural(sl2_ref, rows)
        mx = jnp.maximum(jnp.maximum(l0, l1), l2)
        w0, w1, w2 = jnp.exp2(l0 - mx), jnp.exp2(l1 - mx), jnp.exp2(l2 - mx)
        num = (w0 * o0_ref[rows, :].astype(F32) + w1 * natural(so1_ref, rows)
               + w2 * natural(so2_ref, rows))
        ga = ga_ref[rows, :].astype(F32)
        u["a"] = (num / (w0 + w1 + w2) * (ga * jax.nn.sigmoid(ga))).astype(BF16)

    def branch_proj(u):
        u["y_a"] = jnp.dot(u.pop("a"), wa_ref[...], preferred_element_type=F32)
        u["y_b"] = jnp.dot(ob_ref[u["rows"], :], wb_ref[...], preferred_element_type=F32)

    def gate(u):
        rows = u["rows"]
        u["merged"] = (jax.nn.sigmoid(za_ref[rows, :].astype(F32)) * u.pop("y_a")
                       + jax.nn.sigmoid(zb_ref[rows, :].astype(F32)) * u.pop("y_b")).astype(BF16)

    def out_proj(u):
        rows = u["rows"]
        out_ref[rows, :] = x_ref[rows, :] + jnp.dot(u.pop("merged"), wo_ref[...],
                                                    preferred_element_type=F32)

    tile = dict(rows=slice(0, tm))
    for stage in (merge, branch_proj, gate, out_proj):
        stage(tile)


def _final(o_g, lse_g, z_rest, o_b, x3, wa, wb, wo, dilations, col_ga, col_za, col_zb):
    b, seq, d = x3.shape
    tm = 512
    aw = ATTN_WIDTH

    def rows(width, col=0):
        return pl.BlockSpec((None, tm, width), lambda bi, i, col=col: (bi, i, col))

    def perm(dil):
        return pl.BlockSpec((None, dil, tm // dil, aw), lambda bi, i: (bi, 0, i, 0))

    def full(shape):
        return pl.BlockSpec(shape, lambda bi, i: (0, 0))

    d1, d2 = dilations
    return pl.pallas_call(
        functools.partial(_final_kernel, tm=tm, dilations=dilations),
        out_shape=jax.ShapeDtypeStruct((b, seq, d), F32),
        grid=(b, seq // tm),
        in_specs=[rows(aw), rows(aw), perm(d1), perm(d1), perm(d2), perm(d2),
                  rows(aw, col_ga), rows(d), rows(d, col_za), rows(d, col_zb), rows(d),
                  full(wa.shape), full(wb.shape), full(wo.shape)],
        out_specs=rows(d),
        scratch_shapes=[pltpu.VMEM((aw // LANES, tm, LANES), F32)] * 4,
        compiler_params=_cparams(("parallel", "parallel")),
        name="merge_out",
    )(o_g[0], lse_g[0], o_g[1], lse_g[1], o_g[2], lse_g[2], z_rest, o_b, z_rest, z_rest, x3,
      wa, wb, wo)


def kernel(x, norm_w, w_in, q_norm_w, k_norm_w, rel_bias, lb_fwd, lb_bwd, hg_norm_w,
           w_proj_a, w_proj_b, w_out):
    b, seq, d = x.shape
    n = b * seq
    layer = 0
    w = w_in[layer].astype(BF16)
    qkv_cols = 3 * len(ATTN_GROUPS) * ATTN_WIDTH
    dilations = tuple(dil for _, dil in ATTN_GROUPS)
    assert dilations[0] == 1

    xn_all = _norm(x.astype(F32), norm_w[layer].reshape(1, d).astype(F32), dilations[1:])
    xn = xn_all[0].reshape(n, d)

    w_rest = jnp.concatenate([w[:, qkv_cols + ATTN_WIDTH:], w[:, qkv_cols:qkv_cols + ATTN_WIDTH]],
                             axis=1)
    z_rest = _proj_rest(xn, w_rest).reshape(b, seq, -1)
    hw = HG_HEADS * HG_DK

    o_g, lse_g = [], []
    for g, dilation in enumerate(dilations):
        sub_len = seq // dilation
        wg = w[:, g * 3 * ATTN_WIDTH:(g + 1) * 3 * ATTN_WIDTH]
        q_scale = HEAD_DIM ** -0.5 * LOG2E
        nw = jnp.stack([jnp.tile(q_norm_w[layer, g].astype(F32), HEADS_PER_GROUP) * q_scale,
                        jnp.tile(k_norm_w[layer, g].astype(F32), HEADS_PER_GROUP)]
                       ).reshape(2, 1, ATTN_WIDTH)
        qkv = _proj_qkv(xn_all[g].reshape(n, d), wg, nw, f"proj_qkv_d{dilation}")
        heads = slice(g * HEADS_PER_GROUP, (g + 1) * HEADS_PER_GROUP)
        bias = _attn_bias(rel_bias.astype(F32)[:, heads] * LOG2E, dilation)
        o, lse = _attention(qkv.reshape(b * dilation, sub_len, 3 * ATTN_WIDTH), bias,
                            f"attn_d{dilation}")
        shape = (b, seq, ATTN_WIDTH) if dilation == 1 else (b, dilation, sub_len, ATTN_WIDTH)
        o_g.append(o.reshape(shape))
        lse_g.append(lse.reshape(shape))

    lb_f = jnp.cumsum(jax.nn.softmax(lb_fwd.astype(F32), axis=0), axis=0)[layer].reshape(1, hw)
    lb_b = jnp.cumsum(jax.nn.softmax(lb_bwd.astype(F32), axis=0), axis=0)[layer].reshape(1, hw)
    hb = hw // LANES
    o_b = _hgrn(z_rest, lb_f, lb_b, hg_norm_w[layer].reshape(1, HG_DK).astype(F32),
                col_q=0, col_zf=hb, col_zb=2 * hb, col_i=3 * hb, col_g=4 * hb)

    out = _final(o_g, lse_g, z_rest, o_b, x.astype(F32),
                 w_proj_a[layer].astype(BF16), w_proj_b[layer].astype(BF16),
                 w_out[layer].astype(BF16), dilations[1:],
                 col_ga=7 * hw // ATTN_WIDTH, col_za=5, col_zb=6)
    return out.astype(x.dtype)
```

```python
import functools

import numpy as np
import jax
import jax.numpy as jnp
from jax import lax
from jax.experimental import pallas as pl
from jax.experimental.pallas import tpu as pltpu

F32 = jnp.float32
BF16 = jnp.bfloat16

EPS = 1e-6
LOG2E = float(np.log2(np.e))
NEG_INF = -1e30
ATTN_GROUPS = ((128, 1), (512, 4), (2048, 16))
HEAD_DIM = 64
HEADS_PER_GROUP = 8
ATTN_WIDTH = HEADS_PER_GROUP * HEAD_DIM
NUM_BUCKETS = 32
REL_MAX_DISTANCE = 1024
HG_HEADS = 8
HG_DK = 128

LANES = 128
VMEM_LIMIT = 52 * 1024 * 1024

ATTN_TQ = 128
ATTN_SIDE = 64
ATTN_TK = ATTN_TQ + 2 * ATTN_SIDE

HG_CHUNK = 64
HG_TILE = 256
HG_TILES_PER_TRIP = 16


def _cparams(sem):
    return pltpu.CompilerParams(dimension_semantics=sem, vmem_limit_bytes=VMEM_LIMIT)


def _emit_skewed(units, stages):
    for step in range(len(units) + len(stages) - 1):
        for k, stage in enumerate(stages):
            if 0 <= step - k < len(units):
                stage(units[step - k])


def _norm_kernel(x_ref, w_ref, o_ref, *rest, tm, dilations):
    perm_refs, y_ref = rest[:-1], rest[-1]
    x = x_ref[...]
    ms = jnp.mean(x * x, axis=-1, keepdims=True)
    y = x * lax.rsqrt(ms + EPS) * w_ref[...]
    o_ref[...] = y.astype(BF16)
    n_lane_blocks = y.shape[1] // LANES
    for c in range(n_lane_blocks):
        y_ref[c] = y[:, c * LANES:(c + 1) * LANES]
    for p_ref, dil in zip(perm_refs, dilations):
        for r in range(dil):
            for c in range(n_lane_blocks):
                p_ref[r, :, c * LANES:(c + 1) * LANES] = (
                    y_ref[c, pl.ds(r, tm // dil, stride=dil), :].astype(BF16))


def _norm(x3, w, dilations):
    b, seq, d = x3.shape
    tm = 1024
    out_shape = [jax.ShapeDtypeStruct((b, seq, d), BF16)]
    out_specs = [pl.BlockSpec((None, tm, d), lambda bi, i: (bi, i, 0))]
    for dil in dilations:
        out_shape.append(jax.ShapeDtypeStruct((b, dil, seq // dil, d), BF16))
        out_specs.append(pl.BlockSpec((None, dil, tm // dil, d), lambda bi, i: (bi, 0, i, 0)))
    return pl.pallas_call(
        functools.partial(_norm_kernel, tm=tm, dilations=dilations),
        out_shape=out_shape,
        grid=(b, seq // tm),
        in_specs=[pl.BlockSpec((None, tm, d), lambda bi, i: (bi, i, 0)),
                  pl.BlockSpec((1, d), lambda bi, i: (0, 0))],
        out_specs=out_specs,
        scratch_shapes=[pltpu.VMEM((d // LANES, tm, LANES), F32)],
        compiler_params=_cparams(("parallel", "parallel")),
        name="rmsnorm",
    )(x3, w)


def _proj_kernel(x_ref, w_ref, o_ref):
    o_ref[...] = jnp.dot(x_ref[...], w_ref[...], preferred_element_type=F32).astype(o_ref.dtype)


def _proj_rest(xn, w, col0):
    n, d = xn.shape
    c = w.shape[1] - col0
    tm, tn = 2048, 1536
    assert col0 % tn == 0 and c % tn == 0
    return pl.pallas_call(
        _proj_kernel,
        out_shape=jax.ShapeDtypeStruct((n, c), BF16),
        grid=(n // tm, c // tn),
        in_specs=[pl.BlockSpec((tm, d), lambda i, j: (i, 0)),
                  pl.BlockSpec((d, tn), lambda i, j: (0, col0 // tn + j))],
        out_specs=pl.BlockSpec((tm, tn), lambda i, j: (i, j)),
        compiler_params=_cparams(("parallel", "arbitrary")),
        name="proj_rest",
    )(xn, w)


def _proj_qkv_kernel(x_ref, w_ref, nw_ref, seg_ref, o_ref):
    j = pl.program_id(1)

    @pl.when(j < 2)
    def _():
        sub = 512
        units = [dict(rows=slice(r * sub, (r + 1) * sub)) for r in range(x_ref.shape[0] // sub)]

        def project(u):
            u["z"] = jnp.dot(x_ref[u["rows"], :], w_ref[...], preferred_element_type=F32)

        def square(u):
            u["zz"] = (u["z"] * u["z"]).astype(BF16)

        def mean_square(u):
            u["ms"] = jnp.dot(u.pop("zz"), seg_ref[...], preferred_element_type=F32)

        def normalise(u):
            o_ref[u["rows"], :] = (u.pop("z") * lax.rsqrt(u.pop("ms") + EPS) * nw_ref[...]
                                   ).astype(BF16)

        _emit_skewed(units, (project, square, mean_square, normalise))

    @pl.when(j == 2)
    def _():
        o_ref[...] = jnp.dot(x_ref[...], w_ref[...], preferred_element_type=F32).astype(BF16)


def _proj_qkv(xn2, w, group, nw, name):
    n, d = xn2.shape
    tm, tn = 2048, ATTN_WIDTH
    head = np.arange(tn) // HEAD_DIM
    seg = jnp.asarray((head[:, None] == head[None, :]) / HEAD_DIM, dtype=BF16)
    return pl.pallas_call(
        _proj_qkv_kernel,
        out_shape=jax.ShapeDtypeStruct((n, 3 * tn), BF16),
        grid=(n // tm, 3),
        in_specs=[pl.BlockSpec((tm, d), lambda i, j: (i, 0)),
                  pl.BlockSpec((d, tn), lambda i, j: (0, 3 * group + j)),
                  pl.BlockSpec((None, 1, tn), lambda i, j: (jnp.minimum(j, 1), 0, 0)),
                  pl.BlockSpec((tn, tn), lambda i, j: (0, 0))],
        out_specs=pl.BlockSpec((tm, tn), lambda i, j: (i, j)),
        compiler_params=_cparams(("parallel", "arbitrary")),
        name=name,
    )(xn2, w, nw, seg)


def _t5_bucket(rel):
    half = NUM_BUCKETS // 2
    max_exact = half // 2
    n = np.abs(rel)
    large = max_exact + (np.log(np.maximum(n, 1) / max_exact)
                         / np.log(REL_MAX_DISTANCE / max_exact) * (half - max_exact)).astype(np.int32)
    large = np.minimum(large, half - 1)
    return np.where(rel > 0, half, 0) + np.where(n < max_exact, n, large)


def _attn_bias(bias_tab, dilation):
    t = np.arange(ATTN_TQ)[:, None]
    j = np.arange(ATTN_TK)[None, :]
    rel = np.stack([j - off - t for off in (0, ATTN_SIDE, 2 * ATTN_SIDE)])
    valid = (np.abs(rel) <= ATTN_SIDE).reshape(-1)
    bucket = _t5_bucket(rel * dilation).reshape(-1).astype(np.int32)
    onehot = (jnp.asarray(bucket)[None, :] == jnp.arange(NUM_BUCKETS, dtype=jnp.int32)[:, None])
    vals = jnp.dot(bias_tab.T, onehot.astype(F32), precision=lax.Precision.HIGHEST)
    vals = jnp.where(jnp.asarray(valid)[None, :], vals, NEG_INF)
    return vals.reshape(HEADS_PER_GROUP, 3, ATTN_TQ, ATTN_TK)


def _attn_kernel(q_ref, k_ref, v_ref, bias_ref, o_ref, lse_ref, *, seq, n_sub, n_pairs, blocks):
    nb = seq // ATTN_TQ
    lane = lax.broadcasted_iota(jnp.int32, (ATTN_TQ, LANES), 1)
    first = lane < HEAD_DIM
    nt = (((1,), (1,)), ((), ()))

    def both_heads(col):
        return jnp.where(first, jnp.broadcast_to(col[:ATTN_TQ], (ATTN_TQ, LANES)),
                         jnp.broadcast_to(col[ATTN_TQ:], (ATTN_TQ, LANES)))

    def run(subs, i0):
        units = []
        for sub in subs:
            for bo in range(blocks):
                i = i0 + bo
                q0 = pl.multiple_of(i * ATTN_TQ, ATTN_TQ)
                ks = pl.multiple_of(jnp.clip(i * ATTN_TQ - ATTN_SIDE, 0, seq - ATTN_TK), ATTN_SIDE)
                var = jnp.where(i == 0, 0, jnp.where(i == nb - 1, 2, 1))
                for pr in range(n_pairs):
                    units.append(dict(sub=sub, q0=q0, ks=ks, var=var, pr=pr,
                                      cols=slice(pr * LANES, (pr + 1) * LANES)))

        def scores(u):
            q2 = q_ref[u["sub"], pl.ds(u["q0"], ATTN_TQ), u["cols"]]
            k2 = k_ref[u["sub"], pl.ds(u["ks"], ATTN_TK), u["cols"]]
            zero = jnp.zeros_like(q2)
            qq = jnp.concatenate([jnp.where(first, q2, zero), jnp.where(first, zero, q2)], axis=0)
            u["s"] = lax.dot_general(qq, k2, nt, preferred_element_type=F32)

        def row_max(u):
            pr, var = u["pr"], u["var"]
            u["s"] = u["s"] + jnp.concatenate([bias_ref[2 * pr, var], bias_ref[2 * pr + 1, var]],
                                              axis=0)
            u["m"] = jnp.max(u["s"], axis=-1, keepdims=True)

        def probs(u):
            p = jnp.exp2(u.pop("s") - u["m"])
            u["den"] = jnp.sum(p, axis=-1, keepdims=True)
            u["p"] = p.astype(BF16)

        def values(u):
            v2 = v_ref[u["sub"], pl.ds(u["ks"], ATTN_TK), u["cols"]]
            u["pv"] = jnp.dot(u.pop("p"), v2, preferred_element_type=F32)

        def finish(u):
            den = both_heads(u.pop("den"))
            pv = u.pop("pv")
            rows = pl.ds(u["q0"], ATTN_TQ)
            o_ref[u["sub"], rows, u["cols"]] = (jnp.where(first, pv[:ATTN_TQ], pv[ATTN_TQ:])
                                                * (1.0 / den)).astype(BF16)
            lse_ref[u["sub"], rows, u["cols"]] = (both_heads(u.pop("m"))
                                                  + jnp.log(den) * (1.0 / np.log(2.0)))

        _emit_skewed(units, (scores, row_max, probs, values, finish))

    if nb == blocks:
        run(range(n_sub), 0)
    else:
        def body(j, carry):
            run(range(n_sub), j * blocks)
            return carry

        lax.fori_loop(0, nb // blocks, body, 0)


def _attention(qkv, bias, name):
    n_seq, seq, _ = qkv.shape
    assert seq >= 2 * ATTN_TQ and seq % ATTN_TQ == 0
    if seq >= 4096:
        n_sub, n_pairs = 1, 2
    else:
        n_pairs = ATTN_WIDTH // LANES
        n_sub = max(1, min(n_seq, 1024 // seq))
    blocks = max(2, min(seq // ATTN_TQ, 16 // (n_sub * n_pairs)))
    assert (seq // ATTN_TQ) % blocks == 0
    cw = n_pairs * LANES
    pb = ATTN_WIDTH // cw
    kern = functools.partial(_attn_kernel, seq=seq, n_sub=n_sub, n_pairs=n_pairs, blocks=blocks)
    blk = (n_sub, seq, cw)
    return pl.pallas_call(
        kern,
        out_shape=(jax.ShapeDtypeStruct((n_seq, seq, ATTN_WIDTH), BF16),
                   jax.ShapeDtypeStruct((n_seq, seq, ATTN_WIDTH), F32)),
        grid=(n_seq // n_sub, pb),
        in_specs=[pl.BlockSpec(blk, lambda r, p: (r, 0, p)),
                  pl.BlockSpec(blk, lambda r, p: (r, 0, pb + p)),
                  pl.BlockSpec(blk, lambda r, p: (r, 0, 2 * pb + p)),
                  pl.BlockSpec((2 * n_pairs, 3, ATTN_TQ, ATTN_TK), lambda r, p: (p, 0, 0, 0))],
        out_specs=(pl.BlockSpec(blk, lambda r, p: (r, 0, p)),
                   pl.BlockSpec(blk, lambda r, p: (r, 0, p))),
        compiler_params=_cparams(("parallel", "parallel")),
        name=name,
    )(qkv, qkv, qkv, bias)


def _hgrn_mats():
    t = np.arange(HG_TILE)[:, None]
    u = np.arange(HG_TILE)[None, :]
    same = (t // HG_CHUNK) == (u // HG_CHUNK)
    return (jnp.asarray(same & (u <= t), dtype=BF16), jnp.asarray(same & (u >= t), dtype=BF16))


def _hgrn_kernel(q_ref, zf_ref, zb_ref, i_ref, g_ref, lbf_ref, lbb_ref, nw_ref, mf_ref, mb_ref,
                 o_ref, acc_ref, qd_ref, kv_ref, dec_ref, st_ref, *, seq):
    n_tiles = seq // HG_TILE
    n_chunks = seq // HG_CHUNK
    cpt = HG_TILE // HG_CHUNK
    mid = HG_CHUNK // 2
    nt = (((1,), (1,)), ((), ()))

    def per_chunk_rows(rows):
        return jnp.concatenate([jnp.broadcast_to(r, (HG_CHUNK, HG_DK)) for r in rows], axis=0)

    scans = ((zf_ref, lbf_ref, mf_ref, False), (zb_ref, lbb_ref, mb_ref, True))

    def a_gate(u):
        u["kk"], u["ghl"] = [], []
        for z_ref, lb_ref, _, _ in scans:
            kk = (1.0 - lb_ref[...]) / (1.0 + jnp.exp(z_ref[u["rows"], :].astype(F32)))
            g = jnp.log(1.0 - kk) * (1.0 / np.log(2.0))
            g_hi = g.astype(BF16)
            g_lo = (g - g_hi.astype(F32)).astype(BF16)
            u["kk"].append(kk)
            u["ghl"].append(jnp.concatenate([g_hi, g_lo], axis=1))

    def a_cumsum(u):
        u["sums"] = [jnp.dot(m_ref[...], ghl, preferred_element_type=F32)
                     for (_, _, m_ref, _), ghl in zip(scans, u.pop("ghl"))]

    def a_decay(u):
        q = q_ref[u["rows"], :].astype(F32)
        q_c, k_c, q_d, k_s, decs = [], [], [], [], []
        for (_, _, _, reverse), sums, kk in zip(scans, u.pop("sums"), u.pop("kk")):
            beta = sums[:, :HG_DK] + sums[:, HG_DK:]
            tot_row = 0 if reverse else HG_CHUNK - 1
            cen_row = mid if reverse else mid - 1
            tot = [beta[c * HG_CHUNK + tot_row:c * HG_CHUNK + tot_row + 1] for c in range(cpt)]
            cen = [beta[c * HG_CHUNK + cen_row:c * HG_CHUNK + cen_row + 1] for c in range(cpt)]
            d = beta - per_chunk_rows(cen)
            e_q = jnp.exp2(d)
            qc = (q * e_q).astype(BF16)
            kc = (kk * (1.0 / e_q)).astype(BF16)
            q_d.append(qc * per_chunk_rows([jnp.exp2(c_).astype(BF16) for c_ in cen]))
            k_s.append(kc * per_chunk_rows([jnp.exp2(t_ - c_).astype(BF16)
                                            for t_, c_ in zip(tot, cen)]))
            q_c.append(qc)
            k_c.append(kc)
            decs.append([jnp.exp2(t_) for t_ in tot])
        qd_ref[u["rows"], :] = jnp.concatenate(q_d, axis=1)
        u["ks"] = jnp.concatenate(k_s, axis=1)
        u["qc"], u["kc"] = q_c, k_c
        for c in range(cpt):
            dec_ref[u["t"] * cpt + c] = jnp.concatenate(
                [jnp.broadcast_to(decs[0][c], (8, HG_DK)), jnp.broadcast_to(decs[1][c], (8, HG_DK))],
                axis=1)

    chunk_rows = [slice(c * HG_CHUNK, (c + 1) * HG_CHUNK) for c in range(cpt)]
    ti = lax.broadcasted_iota(jnp.int32, (HG_CHUNK, HG_CHUNK), 0)
    ui = lax.broadcasted_iota(jnp.int32, (HG_CHUNK, HG_CHUNK), 1)
    past, future = ui <= ti, ui >= ti

    def a_scores(u):
        u["a"] = [[lax.dot_general(qc[cr], kc[cr], nt, preferred_element_type=F32)
                   for qc, kc in zip(u["qc"], u["kc"])] for cr in chunk_rows]
        del u["qc"], u["kc"]

    def a_mask(u):
        u["a"] = [(jnp.where(past, a_f, 0.0) + jnp.where(future, a_b, 0.0)).astype(BF16)
                  for a_f, a_b in u.pop("a")]

    def a_values(u):
        v = i_ref[u["rows"], :]
        acc_ref[u["rows"], :] = jnp.concatenate(
            [jnp.dot(a, v[cr], preferred_element_type=F32) for a, cr in zip(u.pop("a"), chunk_rows)],
            axis=0)
        ks = u.pop("ks")
        for c, cr in enumerate(chunk_rows):
            kv_ref[u["t"] * cpt + c] = lax.dot_general(v[cr], ks[cr], (((0,), (0,)), ((), ())),
                                                       preferred_element_type=F32)

    a_stages = (a_gate, a_cumsum, a_decay, a_scores, a_mask, a_values)

    def phase_a(j, carry):
        units = []
        for k in range(HG_TILES_PER_TRIP):
            t = j * HG_TILES_PER_TRIP + k
            units.append(dict(t=t, rows=pl.ds(pl.multiple_of(t * HG_TILE, HG_TILE), HG_TILE)))
        _emit_skewed(units, a_stages)
        return carry

    lax.fori_loop(0, n_tiles // HG_TILES_PER_TRIP, phase_a, 0)

    fw = slice(0, HG_DK)
    bw = slice(HG_DK, 2 * HG_DK)

    def phase_b(n, carry):
        sf, sb = carry
        m = n_chunks - 1 - n
        st_ref[n, :, fw] = sf.astype(BF16)
        sf = dec_ref[n][0:1, fw] * sf + kv_ref[n, :, fw]
        st_ref[m, :, bw] = sb.astype(BF16)
        sb = dec_ref[m][0:1, bw] * sb + kv_ref[m, :, bw]
        return sf, sb

    s0 = jnp.zeros((HG_DK, HG_DK), F32)
    lax.fori_loop(0, n_chunks, phase_b, (s0, s0), unroll=4)

    def c_inter(u):
        u["inter"] = [lax.dot_general(qd_ref[pl.ds(u["r0"] + c * HG_CHUNK, HG_CHUNK), :],
                                      st_ref[u["t"] * cpt + c], nt, preferred_element_type=F32)
                      for c in range(cpt)]

    def c_sum(u):
        u["o"] = acc_ref[u["rows"], :] + jnp.concatenate(u.pop("inter"), axis=0)
        u["ms"] = jnp.mean(u["o"] * u["o"], axis=-1, keepdims=True)

    def c_out(u):
        gate = g_ref[u["rows"], :].astype(F32)
        o_ref[u["rows"], :] = (u.pop("o") * lax.rsqrt(u.pop("ms") + EPS) * nw_ref[...]
                               * (gate * jax.nn.sigmoid(gate))).astype(BF16)

    def phase_c(j, carry):
        units = []
        for k in range(HG_TILES_PER_TRIP):
            t = j * HG_TILES_PER_TRIP + k
            r0 = pl.multiple_of(t * HG_TILE, HG_TILE)
            units.append(dict(t=t, r0=r0, rows=pl.ds(r0, HG_TILE)))
        _emit_skewed(units, (c_inter, c_sum, c_out))
        return carry

    lax.fori_loop(0, n_tiles // HG_TILES_PER_TRIP, phase_c, 0)


def _hgrn(z_rest3, lb_f, lb_b, nw, col_q, col_zf, col_zb, col_i, col_g):
    b, seq, _ = z_rest3.shape
    mf, mb = _hgrn_mats()
    n_chunks = seq // HG_CHUNK

    def zspec(col):
        return pl.BlockSpec((None, seq, HG_DK), lambda bi, h, col=col: (bi, 0, col + h))

    vec = pl.BlockSpec((1, HG_DK), lambda bi, h: (0, h))
    const = pl.BlockSpec((HG_TILE, HG_TILE), lambda bi, h: (0, 0))
    return pl.pallas_call(
        functools.partial(_hgrn_kernel, seq=seq),
        out_shape=jax.ShapeDtypeStruct((b, seq, HG_HEADS * HG_DK), BF16),
        grid=(b, HG_HEADS),
        in_specs=[zspec(col_q), zspec(col_zf), zspec(col_zb), zspec(col_i), zspec(col_g),
                  vec, vec, pl.BlockSpec((1, HG_DK), lambda bi, h: (0, 0)), const, const],
        out_specs=pl.BlockSpec((None, seq, HG_DK), lambda bi, h: (bi, 0, h)),
        scratch_shapes=[pltpu.VMEM((seq, HG_DK), F32),
                        pltpu.VMEM((seq, 2 * HG_DK), BF16),
                        pltpu.VMEM((n_chunks, HG_DK, 2 * HG_DK), F32),
                        pltpu.VMEM((n_chunks, 8, 2 * HG_DK), F32),
                        pltpu.VMEM((n_chunks, HG_DK, 2 * HG_DK), BF16)],
        compiler_params=_cparams(("parallel", "parallel")),
        name="hgrn2",
    )(z_rest3, z_rest3, z_rest3, z_rest3, z_rest3, lb_f, lb_b, nw, mf, mb)


def _final_kernel(o0_ref, l0_ref, o1_ref, l1_ref, o2_ref, l2_ref, ga_ref, ob_ref,
                  za0_ref, za1_ref, zb0_ref, zb1_ref,
                  x_ref, wa_ref, wb_ref, wo_ref, out_ref, so1_ref, sl1_ref, so2_ref, sl2_ref,
                  *, tm, dilations):
    for src, dst, dil in ((o1_ref, so1_ref, dilations[0]), (l1_ref, sl1_ref, dilations[0]),
                          (o2_ref, so2_ref, dilations[1]), (l2_ref, sl2_ref, dilations[1])):
        for r in range(dil):
            val = src[r].astype(F32)
            for c in range(ATTN_WIDTH // LANES):
                dst[c, pl.ds(r, tm // dil, stride=dil), :] = val[:, c * LANES:(c + 1) * LANES]

    def natural(ref, rows):
        return jnp.concatenate([ref[c, rows, :] for c in range(ATTN_WIDTH // LANES)], axis=1)

    def merge(u):
        rows = u["rows"]
        l0, l1, l2 = l0_ref[rows, :], natural(sl1_ref, rows), natural(sl2_ref, rows)
        mx = jnp.maximum(jnp.maximum(l0, l1), l2)
        w0, w1, w2 = jnp.exp2(l0 - mx), jnp.exp2(l1 - mx), jnp.exp2(l2 - mx)
        num = (w0 * o0_ref[rows, :].astype(F32) + w1 * natural(so1_ref, rows)
               + w2 * natural(so2_ref, rows))
        ga = ga_ref[rows, :].astype(F32)
        u["a"] = (num / (w0 + w1 + w2) * (ga * jax.nn.sigmoid(ga))).astype(BF16)

    def branch_proj(u):
        u["y_a"] = jnp.dot(u.pop("a"), wa_ref[...], preferred_element_type=F32)
        u["y_b"] = jnp.dot(ob_ref[u["rows"], :], wb_ref[...], preferred_element_type=F32)

    def gate(u):
        rows = u["rows"]
        za = jnp.concatenate([za0_ref[rows, :], za1_ref[rows, :]], axis=1).astype(F32)
        zb = jnp.concatenate([zb0_ref[rows, :], zb1_ref[rows, :]], axis=1).astype(F32)
        u["merged"] = (jax.nn.sigmoid(za) * u.pop("y_a")
                       + jax.nn.sigmoid(zb) * u.pop("y_b")).astype(BF16)

    def out_proj(u):
        rows = u["rows"]
        out_ref[rows, :] = x_ref[rows, :] + jnp.dot(u.pop("merged"), wo_ref[...],
                                                    preferred_element_type=F32)

    tile = dict(rows=slice(0, tm))
    for stage in (merge, branch_proj, gate, out_proj):
        stage(tile)


def _final(o_g, lse_g, z_rest, o_b, x3, wa, wb, wo, dilations, col_ga, col_za, col_zb):
    b, seq, d = x3.shape
    tm = 512
    aw = ATTN_WIDTH

    def rows(width, col=0):
        return pl.BlockSpec((None, tm, width), lambda bi, i, col=col: (bi, i, col))

    def perm(dil):
        return pl.BlockSpec((None, dil, tm // dil, aw), lambda bi, i: (bi, 0, i, 0))

    def full(shape):
        return pl.BlockSpec(shape, lambda bi, i: (0, 0))

    d1, d2 = dilations
    return pl.pallas_call(
        functools.partial(_final_kernel, tm=tm, dilations=dilations),
        out_shape=jax.ShapeDtypeStruct((b, seq, d), F32),
        grid=(b, seq // tm),
        in_specs=[rows(aw), rows(aw), perm(d1), perm(d1), perm(d2), perm(d2),
                  rows(aw, col_ga), rows(d), rows(aw, col_za), rows(aw, col_za + 1),
                  rows(aw, col_zb), rows(aw, col_zb + 1), rows(d),
                  full(wa.shape), full(wb.shape), full(wo.shape)],
        out_specs=rows(d),
        scratch_shapes=[pltpu.VMEM((aw // LANES, tm, LANES), F32)] * 4,
        compiler_params=_cparams(("parallel", "parallel")),
        name="merge_out",
    )(o_g[0], lse_g[0], o_g[1], lse_g[1], o_g[2], lse_g[2], z_rest, o_b,
      z_rest, z_rest, z_rest, z_rest, x3, wa, wb, wo)


def kernel(x, norm_w, w_in, q_norm_w, k_norm_w, rel_bias, lb_fwd, lb_bwd, hg_norm_w,
           w_proj_a, w_proj_b, w_out):
    b, seq, d = x.shape
    n = b * seq
    layer = 0
    w = w_in[layer].astype(BF16)
    qkv_cols = 3 * len(ATTN_GROUPS) * ATTN_WIDTH
    dilations = tuple(dil for _, dil in ATTN_GROUPS)
    assert dilations[0] == 1

    xn_all = _norm(x.astype(F32), norm_w[layer].reshape(1, d).astype(F32), dilations[1:])
    xn = xn_all[0].reshape(n, d)

    z_rest = _proj_rest(xn, w, qkv_cols).reshape(b, seq, -1)
    hw = HG_HEADS * HG_DK

    o_g, lse_g = [], []
    for g, dilation in enumerate(dilations):
        sub_len = seq // dilation
        q_scale = HEAD_DIM ** -0.5 * LOG2E
        nw = jnp.stack([jnp.tile(q_norm_w[layer, g].astype(F32), HEADS_PER_GROUP) * q_scale,
                        jnp.tile(k_norm_w[layer, g].astype(F32), HEADS_PER_GROUP)]
                       ).reshape(2, 1, ATTN_WIDTH)
        qkv = _proj_qkv(xn_all[g].reshape(n, d), w, g, nw, f"proj_qkv_d{dilation}")
        heads = slice(g * HEADS_PER_GROUP, (g + 1) * HEADS_PER_GROUP)
        bias = _attn_bias(rel_bias.astype(F32)[:, heads] * LOG2E, dilation)
        o, lse = _attention(qkv.reshape(b * dilation, sub_len, 3 * ATTN_WIDTH), bias,
                            f"attn_d{dilation}")
        shape = (b, seq, ATTN_WIDTH) if dilation == 1 else (b, dilation, sub_len, ATTN_WIDTH)
        o_g.append(o.reshape(shape))
        lse_g.append(lse.reshape(shape))

    lb_f = jnp.cumsum(jax.nn.softmax(lb_fwd.astype(F32), axis=0), axis=0)[layer].reshape(1, hw)
    lb_b = jnp.cumsum(jax.nn.softmax(lb_bwd.astype(F32), axis=0), axis=0)[layer].reshape(1, hw)
    hb = hw // LANES
    ga = ATTN_WIDTH // LANES
    o_b = _hgrn(z_rest, lb_f, lb_b, hg_norm_w[layer].reshape(1, HG_DK).astype(F32),
                col_q=ga, col_zf=ga + hb, col_zb=ga + 2 * hb, col_i=ga + 3 * hb, col_g=ga + 4 * hb)

    gate_a = 1 + 5 * hw // ATTN_WIDTH
    out = _final(o_g, lse_g, z_rest, o_b, x.astype(F32),
                 w_proj_a[layer].astype(BF16), w_proj_b[layer].astype(BF16),
                 w_out[layer].astype(BF16), dilations[1:],
                 col_ga=0, col_za=gate_a, col_zb=gate_a + hw // ATTN_WIDTH)
    return out.astype(x.dtype)
```

```python
import functools

import numpy as np
import jax
import jax.numpy as jnp
from jax import lax
from jax.experimental import pallas as pl
from jax.experimental.pallas import tpu as pltpu

F32 = jnp.float32
BF16 = jnp.bfloat16

EPS = 1e-6
LOG2E = float(np.log2(np.e))
NEG_INF = -1e30
ATTN_GROUPS = ((128, 1), (512, 4), (2048, 16))
HEAD_DIM = 64
HEADS_PER_GROUP = 8
ATTN_WIDTH = HEADS_PER_GROUP * HEAD_DIM
NUM_BUCKETS = 32
REL_MAX_DISTANCE = 1024
HG_HEADS = 8
HG_DK = 128

LANES = 128
VMEM_LIMIT = 52 * 1024 * 1024

ATTN_TQ = 128
ATTN_SIDE = 64
ATTN_TK = ATTN_TQ + 2 * ATTN_SIDE

HG_CHUNK = 64
HG_TILE = 256
HG_TILES_PER_TRIP = 16


def _cparams(sem):
    return pltpu.CompilerParams(dimension_semantics=sem, vmem_limit_bytes=VMEM_LIMIT)


def _emit_skewed(units, stages):
    for step in range(len(units) + len(stages) - 1):
        for k, stage in enumerate(stages):
            if 0 <= step - k < len(units):
                stage(units[step - k])


def _norm_kernel(x_ref, w_ref, o_ref, *rest, tm, dilations):
    perm_refs, y_ref = rest[:-1], rest[-1]
    x = x_ref[...]
    ms = jnp.mean(x * x, axis=-1, keepdims=True)
    y = x * lax.rsqrt(ms + EPS) * w_ref[...]
    o_ref[...] = y.astype(BF16)
    n_lane_blocks = y.shape[1] // LANES
    for c in range(n_lane_blocks):
        y_ref[c] = y[:, c * LANES:(c + 1) * LANES]
    for p_ref, dil in zip(perm_refs, dilations):
        for r in range(dil):
            for c in range(n_lane_blocks):
                p_ref[r, :, c * LANES:(c + 1) * LANES] = (
                    y_ref[c, pl.ds(r, tm // dil, stride=dil), :].astype(BF16))


def _norm(x3, w, dilations):
    b, seq, d = x3.shape
    tm = 1024
    out_shape = [jax.ShapeDtypeStruct((b, seq, d), BF16)]
    out_specs = [pl.BlockSpec((None, tm, d), lambda bi, i: (bi, i, 0))]
    for dil in dilations:
        out_shape.append(jax.ShapeDtypeStruct((b, dil, seq // dil, d), BF16))
        out_specs.append(pl.BlockSpec((None, dil, tm // dil, d), lambda bi, i: (bi, 0, i, 0)))
    return pl.pallas_call(
        functools.partial(_norm_kernel, tm=tm, dilations=dilations),
        out_shape=out_shape,
        grid=(b, seq // tm),
        in_specs=[pl.BlockSpec((None, tm, d), lambda bi, i: (bi, i, 0)),
                  pl.BlockSpec((1, d), lambda bi, i: (0, 0))],
        out_specs=out_specs,
        scratch_shapes=[pltpu.VMEM((d // LANES, tm, LANES), F32)],
        compiler_params=_cparams(("parallel", "parallel")),
        name="rmsnorm",
    )(x3, w)


def _proj_kernel(x_ref, w_ref, o_ref):
    o_ref[...] = jnp.dot(x_ref[...], w_ref[...], preferred_element_type=F32).astype(o_ref.dtype)


def _proj_gates(xn, w, col_a, col_b, n_b):
    n, d = xn.shape
    tm, tn = 2048, ATTN_WIDTH
    return pl.pallas_call(
        _proj_kernel,
        out_shape=jax.ShapeDtypeStruct((n, (1 + n_b) * tn), BF16),
        grid=(n // tm, 1 + n_b),
        in_specs=[pl.BlockSpec((tm, d), lambda i, j: (i, 0)),
                  pl.BlockSpec((d, tn), lambda i, j: (0, jnp.where(j == 0, col_a, col_b - 1 + j)))],
        out_specs=pl.BlockSpec((tm, tn), lambda i, j: (i, j)),
        compiler_params=_cparams(("parallel", "arbitrary")),
        name="proj_gates",
    )(xn, w)


def _proj_qkv_kernel(x_ref, w_ref, nw_ref, seg_ref, o_ref):
    j = pl.program_id(1)

    @pl.when(j < 2)
    def _():
        sub = 512
        units = [dict(rows=slice(r * sub, (r + 1) * sub)) for r in range(x_ref.shape[0] // sub)]

        def project(u):
            u["z"] = jnp.dot(x_ref[u["rows"], :], w_ref[...], preferred_element_type=F32)

        def square(u):
            u["zz"] = (u["z"] * u["z"]).astype(BF16)

        def mean_square(u):
            u["ms"] = jnp.dot(u.pop("zz"), seg_ref[...], preferred_element_type=F32)

        def normalise(u):
            o_ref[u["rows"], :] = (u.pop("z") * lax.rsqrt(u.pop("ms") + EPS) * nw_ref[...]
                                   ).astype(BF16)

        _emit_skewed(units, (project, square, mean_square, normalise))

    @pl.when(j == 2)
    def _():
        o_ref[...] = jnp.dot(x_ref[...], w_ref[...], preferred_element_type=F32).astype(BF16)


def _proj_qkv(xn2, w, group, nw, name):
    n, d = xn2.shape
    tm, tn = 2048, ATTN_WIDTH
    head = np.arange(tn) // HEAD_DIM
    seg = jnp.asarray((head[:, None] == head[None, :]) / HEAD_DIM, dtype=BF16)
    return pl.pallas_call(
        _proj_qkv_kernel,
        out_shape=jax.ShapeDtypeStruct((n, 3 * tn), BF16),
        grid=(n // tm, 3),
        in_specs=[pl.BlockSpec((tm, d), lambda i, j: (i, 0)),
                  pl.BlockSpec((d, tn), lambda i, j: (0, 3 * group + j)),
                  pl.BlockSpec((None, 1, tn), lambda i, j: (jnp.minimum(j, 1), 0, 0)),
                  pl.BlockSpec((tn, tn), lambda i, j: (0, 0))],
        out_specs=pl.BlockSpec((tm, tn), lambda i, j: (i, j)),
        compiler_params=_cparams(("parallel", "arbitrary")),
        name=name,
    )(xn2, w, nw, seg)


def _t5_bucket(rel):
    half = NUM_BUCKETS // 2
    max_exact = half // 2
    n = np.abs(rel)
    large = max_exact + (np.log(np.maximum(n, 1) / max_exact)
                         / np.log(REL_MAX_DISTANCE / max_exact) * (half - max_exact)).astype(np.int32)
    large = np.minimum(large, half - 1)
    return np.where(rel > 0, half, 0) + np.where(n < max_exact, n, large)


def _attn_bias(bias_tab, dilation):
    t = np.arange(ATTN_TQ)[:, None]
    j = np.arange(ATTN_TK)[None, :]
    rel = np.stack([j - off - t for off in (0, ATTN_SIDE, 2 * ATTN_SIDE)])
    valid = (np.abs(rel) <= ATTN_SIDE).reshape(-1)
    bucket = _t5_bucket(rel * dilation).reshape(-1).astype(np.int32)
    onehot = (jnp.asarray(bucket)[None, :] == jnp.arange(NUM_BUCKETS, dtype=jnp.int32)[:, None])
    vals = jnp.dot(bias_tab.T, onehot.astype(F32), precision=lax.Precision.HIGHEST)
    vals = jnp.where(jnp.asarray(valid)[None, :], vals, NEG_INF)
    return vals.reshape(HEADS_PER_GROUP, 3, ATTN_TQ, ATTN_TK)


def _attn_kernel(q_ref, k_ref, v_ref, bias_ref, o_ref, lse_ref, *, seq, n_sub, n_pairs, blocks):
    nb = seq // ATTN_TQ
    lane = lax.broadcasted_iota(jnp.int32, (ATTN_TQ, LANES), 1)
    first = lane < HEAD_DIM
    nt = (((1,), (1,)), ((), ()))

    def both_heads(col):
        return jnp.where(first, jnp.broadcast_to(col[:ATTN_TQ], (ATTN_TQ, LANES)),
                         jnp.broadcast_to(col[ATTN_TQ:], (ATTN_TQ, LANES)))

    def run(subs, i0):
        units = []
        for sub in subs:
            for bo in range(blocks):
                i = i0 + bo
                q0 = pl.multiple_of(i * ATTN_TQ, ATTN_TQ)
                ks = pl.multiple_of(jnp.clip(i * ATTN_TQ - ATTN_SIDE, 0, seq - ATTN_TK), ATTN_SIDE)
                var = jnp.where(i == 0, 0, jnp.where(i == nb - 1, 2, 1))
                for pr in range(n_pairs):
                    units.append(dict(sub=sub, q0=q0, ks=ks, var=var, pr=pr,
                                      cols=slice(pr * LANES, (pr + 1) * LANES)))

        def scores(u):
            q2 = q_ref[u["sub"], pl.ds(u["q0"], ATTN_TQ), u["cols"]]
            k2 = k_ref[u["sub"], pl.ds(u["ks"], ATTN_TK), u["cols"]]
            zero = jnp.zeros_like(q2)
            qq = jnp.concatenate([jnp.where(first, q2, zero), jnp.where(first, zero, q2)], axis=0)
            u["s"] = lax.dot_general(qq, k2, nt, preferred_element_type=F32)

        def row_max(u):
            pr, var = u["pr"], u["var"]
            u["s"] = u["s"] + jnp.concatenate([bias_ref[2 * pr, var], bias_ref[2 * pr + 1, var]],
                                              axis=0)
            u["m"] = jnp.max(u["s"], axis=-1, keepdims=True)

        def probs(u):
            p = jnp.exp2(u.pop("s") - u["m"])
            u["den"] = jnp.sum(p, axis=-1, keepdims=True)
            u["p"] = p.astype(BF16)

        def values(u):
            v2 = v_ref[u["sub"], pl.ds(u["ks"], ATTN_TK), u["cols"]]
            u["pv"] = jnp.dot(u.pop("p"), v2, preferred_element_type=F32)

        def finish(u):
            den = both_heads(u.pop("den"))
            pv = u.pop("pv")
            rows = pl.ds(u["q0"], ATTN_TQ)
            o_ref[u["sub"], rows, u["cols"]] = (jnp.where(first, pv[:ATTN_TQ], pv[ATTN_TQ:])
                                                * (1.0 / den)).astype(BF16)
            lse_ref[u["sub"], rows, u["cols"]] = (both_heads(u.pop("m"))
                                                  + jnp.log(den) * (1.0 / np.log(2.0)))

        _emit_skewed(units, (scores, row_max, probs, values, finish))

    if nb == blocks:
        run(range(n_sub), 0)
    else:
        def body(j, carry):
            run(range(n_sub), j * blocks)
            return carry

        lax.fori_loop(0, nb // blocks, body, 0)


def _attention(qkv, bias, name):
    n_seq, seq, _ = qkv.shape
    assert seq >= 2 * ATTN_TQ and seq % ATTN_TQ == 0
    if seq >= 4096:
        n_sub, n_pairs = 1, 2
    else:
        n_pairs = ATTN_WIDTH // LANES
        n_sub = max(1, min(n_seq, 1024 // seq))
    blocks = max(2, min(seq // ATTN_TQ, 16 // (n_sub * n_pairs)))
    assert (seq // ATTN_TQ) % blocks == 0
    cw = n_pairs * LANES
    pb = ATTN_WIDTH // cw
    kern = functools.partial(_attn_kernel, seq=seq, n_sub=n_sub, n_pairs=n_pairs, blocks=blocks)
    blk = (n_sub, seq, cw)
    return pl.pallas_call(
        kern,
        out_shape=(jax.ShapeDtypeStruct((n_seq, seq, ATTN_WIDTH), BF16),
                   jax.ShapeDtypeStruct((n_seq, seq, ATTN_WIDTH), F32)),
        grid=(n_seq // n_sub, pb),
        in_specs=[pl.BlockSpec(blk, lambda r, p: (r, 0, p)),
                  pl.BlockSpec(blk, lambda r, p: (r, 0, pb + p)),
                  pl.BlockSpec(blk, lambda r, p: (r, 0, 2 * pb + p)),
                  pl.BlockSpec((2 * n_pairs, 3, ATTN_TQ, ATTN_TK), lambda r, p: (p, 0, 0, 0))],
        out_specs=(pl.BlockSpec(blk, lambda r, p: (r, 0, p)),
                   pl.BlockSpec(blk, lambda r, p: (r, 0, p))),
        compiler_params=_cparams(("parallel", "parallel")),
        name=name,
    )(qkv, qkv, qkv, bias)


def _hgrn_mats():
    t = np.arange(HG_TILE)[:, None]
    u = np.arange(HG_TILE)[None, :]
    same = (t // HG_CHUNK) == (u // HG_CHUNK)
    return (jnp.asarray(same & (u <= t), dtype=BF16), jnp.asarray(same & (u >= t), dtype=BF16))


def _hgrn_kernel(xn_ref, wq_ref, wzf_ref, wzb_ref, wi_ref, wg_ref, wx_ref, lbf_ref, lbb_ref,
                 nw_ref, mf_ref, mb_ref, o_ref, x_out_ref, wcat_ref, gate_ref, acc_ref, qd_ref,
                 kv_ref, dec_ref, st_ref, *, seq):
    n_tiles = seq // HG_TILE
    n_chunks = seq // HG_CHUNK
    cpt = HG_TILE // HG_CHUNK
    mid = HG_CHUNK // 2
    nt = (((1,), (1,)), ((), ()))
    w_refs = (wq_ref, wzf_ref, wzb_ref, wi_ref, wg_ref, wx_ref)

    def per_chunk_rows(rows):
        return jnp.concatenate([jnp.broadcast_to(r, (HG_CHUNK, HG_DK)) for r in rows], axis=0)

    scans = ((lbf_ref, mf_ref, False), (lbb_ref, mb_ref, True))

    for k, w_ref in enumerate(w_refs):
        wcat_ref[:, k * HG_DK:(k + 1) * HG_DK] = w_ref[...]

    def a_project(u):
        z = jnp.dot(xn_ref[u["rows"], :], wcat_ref[...], preferred_element_type=F32)
        u["q"] = z[:, 0:HG_DK]
        u["z"] = [z[:, HG_DK:2 * HG_DK], z[:, 2 * HG_DK:3 * HG_DK]]
        u["v"] = z[:, 3 * HG_DK:4 * HG_DK].astype(BF16)
        gate_ref[u["rows"], :] = z[:, 4 * HG_DK:5 * HG_DK]
        x_out_ref[u["rows"], :] = z[:, 5 * HG_DK:].astype(BF16)

    def a_gate(u):
        u["kk"], u["ghl"] = [], []
        for (lb_ref, _, _), z in zip(scans, u.pop("z")):
            kk = (1.0 - lb_ref[...]) / (1.0 + jnp.exp(z))
            g = jnp.log(1.0 - kk) * (1.0 / np.log(2.0))
            g_hi = g.astype(BF16)
            g_lo = (g - g_hi.astype(F32)).astype(BF16)
            u["kk"].append(kk)
            u["ghl"].append(jnp.concatenate([g_hi, g_lo], axis=1))

    def a_cumsum(u):
        u["sums"] = [jnp.dot(m_ref[...], ghl, preferred_element_type=F32)
                     for (_, m_ref, _), ghl in zip(scans, u.pop("ghl"))]

    def a_decay(u):
        q = u.pop("q")
        q_c, k_c, q_d, k_s, decs = [], [], [], [], []
        for (_, _, reverse), sums, kk in zip(scans, u.pop("sums"), u.pop("kk")):
            beta = sums[:, :HG_DK] + sums[:, HG_DK:]
            tot_row = 0 if reverse else HG_CHUNK - 1
            cen_row = mid if reverse else mid - 1
            tot = [beta[c * HG_CHUNK + tot_row:c * HG_CHUNK + tot_row + 1] for c in range(cpt)]
            cen = [beta[c * HG_CHUNK + cen_row:c * HG_CHUNK + cen_row + 1] for c in range(cpt)]
            d = beta - per_chunk_rows(cen)
            e_q = jnp.exp2(d)
            qc = (q * e_q).astype(BF16)
            kc = (kk * (1.0 / e_q)).astype(BF16)
            q_d.append(qc * per_chunk_rows([jnp.exp2(c_).astype(BF16) for c_ in cen]))
            k_s.append(kc * per_chunk_rows([jnp.exp2(t_ - c_).astype(BF16)
                                            for t_, c_ in zip(tot, cen)]))
            q_c.append(qc)
            k_c.append(kc)
            decs.append([jnp.exp2(t_) for t_ in tot])
        qd_ref[u["rows"], :] = jnp.concatenate(q_d, axis=1)
        u["ks"] = jnp.concatenate(k_s, axis=1)
        u["qc"], u["kc"] = q_c, k_c
        for c in range(cpt):
            dec_ref[u["t"] * cpt + c] = jnp.concatenate(
                [jnp.broadcast_to(decs[0][c], (8, HG_DK)), jnp.broadcast_to(decs[1][c], (8, HG_DK))],
                axis=1)

    chunk_rows = [slice(c * HG_CHUNK, (c + 1) * HG_CHUNK) for c in range(cpt)]
    ti = lax.broadcasted_iota(jnp.int32, (HG_CHUNK, HG_CHUNK), 0)
    ui = lax.broadcasted_iota(jnp.int32, (HG_CHUNK, HG_CHUNK), 1)
    past, future = ui <= ti, ui >= ti

    def a_scores(u):
        u["a"] = [[lax.dot_general(qc[cr], kc[cr], nt, preferred_element_type=F32)
                   for qc, kc in zip(u["qc"], u["kc"])] for cr in chunk_rows]
        del u["qc"], u["kc"]

    def a_mask(u):
        u["a"] = [(jnp.where(past, a_f, 0.0) + jnp.where(future, a_b, 0.0)).astype(BF16)
                  for a_f, a_b in u.pop("a")]

    def a_values(u):
        v = u.pop("v")
        acc_ref[u["rows"], :] = jnp.concatenate(
            [jnp.dot(a, v[cr], preferred_element_type=F32) for a, cr in zip(u.pop("a"), chunk_rows)],
            axis=0)
        ks = u.pop("ks")
        for c, cr in enumerate(chunk_rows):
            kv_ref[u["t"] * cpt + c] = lax.dot_general(v[cr], ks[cr], (((0,), (0,)), ((), ())),
                                                       preferred_element_type=F32)

    a_stages = (a_project, a_gate, a_cumsum, a_decay, a_scores, a_mask, a_values)

    def phase_a(j, carry):
        units = []
        for k in range(HG_TILES_PER_TRIP):
            t = j * HG_TILES_PER_TRIP + k
            units.append(dict(t=t, rows=pl.ds(pl.multiple_of(t * HG_TILE, HG_TILE), HG_TILE)))
        _emit_skewed(units, a_stages)
        return carry

    lax.fori_loop(0, n_tiles // HG_TILES_PER_TRIP, phase_a, 0)

    fw = slice(0, HG_DK)
    bw = slice(HG_DK, 2 * HG_DK)

    def phase_b(n, carry):
        sf, sb = carry
        m = n_chunks - 1 - n
        st_ref[n, :, fw] = sf.astype(BF16)
        sf = dec_ref[n][0:1, fw] * sf + kv_ref[n, :, fw]
        st_ref[m, :, bw] = sb.astype(BF16)
        sb = dec_ref[m][0:1, bw] * sb + kv_ref[m, :, bw]
        return sf, sb

    s0 = jnp.zeros((HG_DK, HG_DK), F32)
    lax.fori_loop(0, n_chunks, phase_b, (s0, s0), unroll=4)

    def c_inter(u):
        u["inter"] = [lax.dot_general(qd_ref[pl.ds(u["r0"] + c * HG_CHUNK, HG_CHUNK), :],
                                      st_ref[u["t"] * cpt + c], nt, preferred_element_type=F32)
                      for c in range(cpt)]

    def c_sum(u):
        u["o"] = acc_ref[u["rows"], :] + jnp.concatenate(u.pop("inter"), axis=0)
        u["ms"] = jnp.mean(u["o"] * u["o"], axis=-1, keepdims=True)

    def c_out(u):
        gate = gate_ref[u["rows"], :]
        o_ref[u["rows"], :] = (u.pop("o") * lax.rsqrt(u.pop("ms") + EPS) * nw_ref[...]
                               * (gate * jax.nn.sigmoid(gate))).astype(BF16)

    def phase_c(j, carry):
        units = []
        for k in range(HG_TILES_PER_TRIP):
            t = j * HG_TILES_PER_TRIP + k
            r0 = pl.multiple_of(t * HG_TILE, HG_TILE)
            units.append(dict(t=t, r0=r0, rows=pl.ds(r0, HG_TILE)))
        _emit_skewed(units, (c_inter, c_sum, c_out))
        return carry

    lax.fori_loop(0, n_tiles // HG_TILES_PER_TRIP, phase_c, 0)


def _hgrn(xn3, w, lb_f, lb_b, nw, col_q, col_zf, col_zb, col_i, col_g, col_x):
    b, seq, d = xn3.shape
    mf, mb = _hgrn_mats()
    n_chunks = seq // HG_CHUNK

    def wspec(col):
        return pl.BlockSpec((d, HG_DK), lambda bi, h, col=col: (0, col + h))

    vec = pl.BlockSpec((1, HG_DK), lambda bi, h: (0, h))
    const = pl.BlockSpec((HG_TILE, HG_TILE), lambda bi, h: (0, 0))
    return pl.pallas_call(
        functools.partial(_hgrn_kernel, seq=seq),
        out_shape=[jax.ShapeDtypeStruct((b, seq, HG_HEADS * HG_DK), BF16)] * 2,
        grid=(b, HG_HEADS),
        in_specs=[pl.BlockSpec((None, seq, d), lambda bi, h: (bi, 0, 0)),
                  wspec(col_q), wspec(col_zf), wspec(col_zb), wspec(col_i), wspec(col_g),
                  wspec(col_x),
                  vec, vec, pl.BlockSpec((1, HG_DK), lambda bi, h: (0, 0)), const, const],
        out_specs=[pl.BlockSpec((None, seq, HG_DK), lambda bi, h: (bi, 0, h))] * 2,
        scratch_shapes=[pltpu.VMEM((d, 6 * HG_DK), BF16),
                        pltpu.VMEM((seq, HG_DK), F32),
                        pltpu.VMEM((seq, HG_DK), F32),
                        pltpu.VMEM((seq, 2 * HG_DK), BF16),
                        pltpu.VMEM((n_chunks, HG_DK, 2 * HG_DK), F32),
                        pltpu.VMEM((n_chunks, 8, 2 * HG_DK), F32),
                        pltpu.VMEM((n_chunks, HG_DK, 2 * HG_DK), BF16)],
        compiler_params=_cparams(("parallel", "parallel")),
        name="hgrn2",
    )(xn3, w, w, w, w, w, w, lb_f, lb_b, nw, mf, mb)


def _final_kernel(o0_ref, l0_ref, o1_ref, l1_ref, o2_ref, l2_ref, ga_ref, ob_ref,
                  za0_ref, za1_ref, zb0_ref, zb1_ref,
                  x_ref, wa_ref, wb_ref, wo_ref, out_ref, so1_ref, sl1_ref, so2_ref, sl2_ref,
                  *, tm, dilations):
    for src, dst, dil in ((o1_ref, so1_ref, dilations[0]), (l1_ref, sl1_ref, dilations[0]),
                          (o2_ref, so2_ref, dilations[1]), (l2_ref, sl2_ref, dilations[1])):
        for r in range(dil):
            val = src[r].astype(F32)
            for c in range(ATTN_WIDTH // LANES):
                dst[c, pl.ds(r, tm // dil, stride=dil), :] = val[:, c * LANES:(c + 1) * LANES]

    def natural(ref):
        return jnp.concatenate([ref[c] for c in range(ATTN_WIDTH // LANES)], axis=1)

    l0, l1, l2 = l0_ref[...], natural(sl1_ref), natural(sl2_ref)
    mx = jnp.maximum(jnp.maximum(l0, l1), l2)
    w0, w1, w2 = jnp.exp2(l0 - mx), jnp.exp2(l1 - mx), jnp.exp2(l2 - mx)
    num = w0 * o0_ref[...].astype(F32) + w1 * natural(so1_ref) + w2 * natural(so2_ref)
    ga = ga_ref[...].astype(F32)
    a = (num / (w0 + w1 + w2) * (ga * jax.nn.sigmoid(ga))).astype(BF16)

    y_a = jnp.dot(a, wa_ref[...], preferred_element_type=F32)
    y_b = jnp.dot(ob_ref[...], wb_ref[...], preferred_element_type=F32)
    za = jnp.concatenate([za0_ref[...], za1_ref[...]], axis=1).astype(F32)
    zb = jnp.concatenate([zb0_ref[...], zb1_ref[...]], axis=1).astype(F32)
    merged = (jax.nn.sigmoid(za) * y_a + jax.nn.sigmoid(zb) * y_b).astype(BF16)
    out_ref[...] = x_ref[...] + jnp.dot(merged, wo_ref[...], preferred_element_type=F32)


def _final(o_g, lse_g, z_gate, z_ga, o_b, x3, wa, wb, wo, dilations, col_ga, col_zb):
    b, seq, d = x3.shape
    tm = 512
    aw = ATTN_WIDTH

    def rows(width, col=0):
        return pl.BlockSpec((None, tm, width), lambda bi, i, col=col: (bi, i, col))

    def perm(dil):
        return pl.BlockSpec((None, dil, tm // dil, aw), lambda bi, i: (bi, 0, i, 0))

    def full(shape):
        return pl.BlockSpec(shape, lambda bi, i: (0, 0))

    d1, d2 = dilations
    return pl.pallas_call(
        functools.partial(_final_kernel, tm=tm, dilations=dilations),
        out_shape=jax.ShapeDtypeStruct((b, seq, d), F32),
        grid=(b, seq // tm),
        in_specs=[rows(aw), rows(aw), perm(d1), perm(d1), perm(d2), perm(d2),
                  rows(aw, col_ga), rows(d), rows(aw, 0), rows(aw, 1),
                  rows(aw, col_zb), rows(aw, col_zb + 1), rows(d),
                  full(wa.shape), full(wb.shape), full(wo.shape)],
        out_specs=rows(d),
        scratch_shapes=[pltpu.VMEM((aw // LANES, tm, LANES), F32)] * 4,
        compiler_params=_cparams(("parallel", "parallel")),
        name="merge_out",
    )(o_g[0], lse_g[0], o_g[1], lse_g[1], o_g[2], lse_g[2], z_gate, o_b,
      z_ga, z_ga, z_gate, z_gate, x3, wa, wb, wo)


def kernel(x, norm_w, w_in, q_norm_w, k_norm_w, rel_bias, lb_fwd, lb_bwd, hg_norm_w,
           w_proj_a, w_proj_b, w_out):
    b, seq, d = x.shape
    n = b * seq
    layer = 0
    w = w_in[layer].astype(BF16)
    qkv_cols = 3 * len(ATTN_GROUPS) * ATTN_WIDTH
    dilations = tuple(dil for _, dil in ATTN_GROUPS)
    assert dilations[0] == 1

    xn_all = _norm(x.astype(F32), norm_w[layer].reshape(1, d).astype(F32), dilations[1:])
    xn = xn_all[0].reshape(n, d)

    hw = HG_HEADS * HG_DK
    c_ga = qkv_cols // ATTN_WIDTH
    c_gate_b = c_ga + 1 + 6 * hw // ATTN_WIDTH
    z_gate = _proj_gates(xn, w, c_ga, c_gate_b, hw // ATTN_WIDTH).reshape(b, seq, -1)

    o_g, lse_g = [], []
    for g, dilation in enumerate(dilations):
        sub_len = seq // dilation
        q_scale = HEAD_DIM ** -0.5 * LOG2E
        nw = jnp.stack([jnp.tile(q_norm_w[layer, g].astype(F32), HEADS_PER_GROUP) * q_scale,
                        jnp.tile(k_norm_w[layer, g].astype(F32), HEADS_PER_GROUP)]
                       ).reshape(2, 1, ATTN_WIDTH)
        qkv = _proj_qkv(xn_all[g].reshape(n, d), w, g, nw, f"proj_qkv_d{dilation}")
        heads = slice(g * HEADS_PER_GROUP, (g + 1) * HEADS_PER_GROUP)
        bias = _attn_bias(rel_bias.astype(F32)[:, heads] * LOG2E, dilation)
        o, lse = _attention(qkv.reshape(b * dilation, sub_len, 3 * ATTN_WIDTH), bias,
                            f"attn_d{dilation}")
        shape = (b, seq, ATTN_WIDTH) if dilation == 1 else (b, dilation, sub_len, ATTN_WIDTH)
        o_g.append(o.reshape(shape))
        lse_g.append(lse.reshape(shape))

    lb_f = jnp.cumsum(jax.nn.softmax(lb_fwd.astype(F32), axis=0), axis=0)[layer].reshape(1, hw)
    lb_b = jnp.cumsum(jax.nn.softmax(lb_bwd.astype(F32), axis=0), axis=0)[layer].reshape(1, hw)
    hb = hw // LANES
    c_qb = (qkv_cols + ATTN_WIDTH) // LANES
    o_b, z_ga = _hgrn(xn_all[0], w, lb_f, lb_b, hg_norm_w[layer].reshape(1, HG_DK).astype(F32),
                      col_q=c_qb, col_zf=c_qb + hb, col_zb=c_qb + 2 * hb, col_i=c_qb + 3 * hb,
                      col_g=c_qb + 4 * hb, col_x=c_qb + 5 * hb)

    out = _final(o_g, lse_g, z_gate, z_ga, o_b, x.astype(F32),
                 w_proj_a[layer].astype(BF16), w_proj_b[layer].astype(BF16),
                 w_out[layer].astype(BF16), dilations[1:], col_ga=0, col_zb=1)
    return out.astype(x.dtype)
```

```python
import functools

import numpy as np
import jax
import jax.numpy as jnp
from jax import lax
from jax.experimental import pallas as pl
from jax.experimental.pallas import tpu as pltpu

F32 = jnp.float32
BF16 = jnp.bfloat16

EPS = 1e-6
LOG2E = float(np.log2(np.e))
NEG_INF = -1e30
ATTN_GROUPS = ((128, 1), (512, 4), (2048, 16))
HEAD_DIM = 64
HEADS_PER_GROUP = 8
ATTN_WIDTH = HEADS_PER_GROUP * HEAD_DIM
NUM_BUCKETS = 32
REL_MAX_DISTANCE = 1024
HG_HEADS = 8
HG_DK = 128

LANES = 128
VMEM_LIMIT = 52 * 1024 * 1024

ATTN_TQ = 128
ATTN_SIDE = 64
ATTN_TK = ATTN_TQ + 2 * ATTN_SIDE

HG_CHUNK = 64
HG_TILE = 256
HG_TILES_PER_TRIP = 16


def _cparams(sem):
    return pltpu.CompilerParams(dimension_semantics=sem, vmem_limit_bytes=VMEM_LIMIT)


def _emit_skewed(units, stages):
    for step in range(len(units) + len(stages) - 1):
        for k, stage in enumerate(stages):
            if 0 <= step - k < len(units):
                stage(units[step - k])


def _norm_gates_kernel(x_ref, nw_ref, w_ref, z_ref, o_ref, *rest, tm, dilations):
    perm_refs, y_ref, xn_ref = rest[:-2], rest[-2], rest[-1]

    @pl.when(pl.program_id(2) == 0)
    def _():
        x = x_ref[...]
        ms = jnp.mean(x * x, axis=-1, keepdims=True)
        y = x * lax.rsqrt(ms + EPS) * nw_ref[...]
        xn = y.astype(BF16)
        o_ref[...] = xn
        xn_ref[...] = xn
        n_lane_blocks = y.shape[1] // LANES
        for c in range(n_lane_blocks):
            y_ref[c] = y[:, c * LANES:(c + 1) * LANES]
        for p_ref, dil in zip(perm_refs, dilations):
            for r in range(dil):
                for c in range(n_lane_blocks):
                    p_ref[r, :, c * LANES:(c + 1) * LANES] = (
                        y_ref[c, pl.ds(r, tm // dil, stride=dil), :].astype(BF16))

    z_ref[...] = jnp.dot(xn_ref[...], w_ref[...], preferred_element_type=F32).astype(BF16)


def _norm_gates(x3, nw, w, dilations, col_a, col_b, n_b):
    b, seq, d = x3.shape
    tm, tn = 1024, ATTN_WIDTH
    out_shape = [jax.ShapeDtypeStruct((b, seq, (1 + n_b) * tn), BF16),
                 jax.ShapeDtypeStruct((b, seq, d), BF16)]
    out_specs = [pl.BlockSpec((None, tm, tn), lambda bi, i, j: (bi, i, j)),
                 pl.BlockSpec((None, tm, d), lambda bi, i, j: (bi, i, 0))]
    for dil in dilations:
        out_shape.append(jax.ShapeDtypeStruct((b, dil, seq // dil, d), BF16))
        out_specs.append(pl.BlockSpec((None, dil, tm // dil, d), lambda bi, i, j: (bi, 0, i, 0)))
    return pl.pallas_call(
        functools.partial(_norm_gates_kernel, tm=tm, dilations=dilations),
        out_shape=out_shape,
        grid=(b, seq // tm, 1 + n_b),
        in_specs=[pl.BlockSpec((None, tm, d), lambda bi, i, j: (bi, i, 0)),
                  pl.BlockSpec((1, d), lambda bi, i, j: (0, 0)),
                  pl.BlockSpec((d, tn),
                               lambda bi, i, j: (0, jnp.where(j == 0, col_a, col_b - 1 + j)))],
        out_specs=out_specs,
        scratch_shapes=[pltpu.VMEM((d // LANES, tm, LANES), F32),
                        pltpu.VMEM((tm, d), BF16)],
        compiler_params=_cparams(("parallel", "parallel", "arbitrary")),
        name="norm_gates",
    )(x3, nw, w)


def _proj_qkv_kernel(x_ref, w_ref, nw_ref, seg_ref, o_ref):
    j = pl.program_id(1)

    @pl.when(j < 2)
    def _():
        sub = 512
        units = [dict(rows=slice(r * sub, (r + 1) * sub)) for r in range(x_ref.shape[0] // sub)]

        def project(u):
            u["z"] = jnp.dot(x_ref[u["rows"], :], w_ref[...], preferred_element_type=F32)

        def square(u):
            u["zz"] = (u["z"] * u["z"]).astype(BF16)

        def mean_square(u):
            u["ms"] = jnp.dot(u.pop("zz"), seg_ref[...], preferred_element_type=F32)

        def normalise(u):
            o_ref[u["rows"], :] = (u.pop("z") * lax.rsqrt(u.pop("ms") + EPS) * nw_ref[...]
                                   ).astype(BF16)

        _emit_skewed(units, (project, square, mean_square, normalise))

    @pl.when(j == 2)
    def _():
        o_ref[...] = jnp.dot(x_ref[...], w_ref[...], preferred_element_type=F32).astype(BF16)


def _proj_qkv(xn2, w, group, nw, name):
    n, d = xn2.shape
    tm, tn = 2048, ATTN_WIDTH
    head = np.arange(tn) // HEAD_DIM
    seg = jnp.asarray((head[:, None] == head[None, :]) / HEAD_DIM, dtype=BF16)
    return pl.pallas_call(
        _proj_qkv_kernel,
        out_shape=jax.ShapeDtypeStruct((n, 3 * tn), BF16),
        grid=(n // tm, 3),
        in_specs=[pl.BlockSpec((tm, d), lambda i, j: (i, 0)),
                  pl.BlockSpec((d, tn), lambda i, j: (0, 3 * group + j)),
                  pl.BlockSpec((None, 1, tn), lambda i, j: (jnp.minimum(j, 1), 0, 0)),
                  pl.BlockSpec((tn, tn), lambda i, j: (0, 0))],
        out_specs=pl.BlockSpec((tm, tn), lambda i, j: (i, j)),
        compiler_params=_cparams(("parallel", "arbitrary")),
        name=name,
    )(xn2, w, nw, seg)


def _t5_bucket(rel):
    half = NUM_BUCKETS // 2
    max_exact = half // 2
    n = np.abs(rel)
    large = max_exact + (np.log(np.maximum(n, 1) / max_exact)
                         / np.log(REL_MAX_DISTANCE / max_exact) * (half - max_exact)).astype(np.int32)
    large = np.minimum(large, half - 1)
    return np.where(rel > 0, half, 0) + np.where(n < max_exact, n, large)


def _attn_bias(bias_tab, dilation):
    t = np.arange(ATTN_TQ)[:, None]
    j = np.arange(ATTN_TK)[None, :]
    rel = np.stack([j - off - t for off in (0, ATTN_SIDE, 2 * ATTN_SIDE)])
    valid = (np.abs(rel) <= ATTN_SIDE).reshape(-1)
    bucket = _t5_bucket(rel * dilation).reshape(-1).astype(np.int32)
    onehot = (jnp.asarray(bucket)[None, :] == jnp.arange(NUM_BUCKETS, dtype=jnp.int32)[:, None])
    vals = jnp.dot(bias_tab.T, onehot.astype(F32), precision=lax.Precision.HIGHEST)
    vals = jnp.where(jnp.asarray(valid)[None, :], vals, NEG_INF)
    return vals.reshape(HEADS_PER_GROUP, 3, ATTN_TQ, ATTN_TK)


def _attn_kernel(q_ref, k_ref, v_ref, bias_ref, o_ref, lse_ref, *, seq, n_sub, n_pairs, blocks):
    nb = seq // ATTN_TQ
    lane = lax.broadcasted_iota(jnp.int32, (ATTN_TQ, LANES), 1)
    first = lane < HEAD_DIM
    nt = (((1,), (1,)), ((), ()))

    def both_heads(col):
        return jnp.where(first, jnp.broadcast_to(col[:ATTN_TQ], (ATTN_TQ, LANES)),
                         jnp.broadcast_to(col[ATTN_TQ:], (ATTN_TQ, LANES)))

    def run(subs, i0):
        units = []
        for sub in subs:
            for bo in range(blocks):
                i = i0 + bo
                q0 = pl.multiple_of(i * ATTN_TQ, ATTN_TQ)
                ks = pl.multiple_of(jnp.clip(i * ATTN_TQ - ATTN_SIDE, 0, seq - ATTN_TK), ATTN_SIDE)
                var = jnp.where(i == 0, 0, jnp.where(i == nb - 1, 2, 1))
                for pr in range(n_pairs):
                    units.append(dict(sub=sub, q0=q0, ks=ks, var=var, pr=pr,
                                      cols=slice(pr * LANES, (pr + 1) * LANES)))

        def scores(u):
            q2 = q_ref[u["sub"], pl.ds(u["q0"], ATTN_TQ), u["cols"]]
            k2 = k_ref[u["sub"], pl.ds(u["ks"], ATTN_TK), u["cols"]]
            zero = jnp.zeros_like(q2)
            qq = jnp.concatenate([jnp.where(first, q2, zero), jnp.where(first, zero, q2)], axis=0)
            u["s"] = lax.dot_general(qq, k2, nt, preferred_element_type=F32)

        def row_max(u):
            pr, var = u["pr"], u["var"]
            u["s"] = u["s"] + jnp.concatenate([bias_ref[2 * pr, var], bias_ref[2 * pr + 1, var]],
                                              axis=0)
            u["m"] = jnp.max(u["s"], axis=-1, keepdims=True)

        def probs(u):
            p = jnp.exp2(u.pop("s") - u["m"])
            u["den"] = jnp.sum(p, axis=-1, keepdims=True)
            u["p"] = p.astype(BF16)

        def values(u):
            v2 = v_ref[u["sub"], pl.ds(u["ks"], ATTN_TK), u["cols"]]
            u["pv"] = jnp.dot(u.pop("p"), v2, preferred_element_type=F32)

        def finish(u):
            den = both_heads(u.pop("den"))
            pv = u.pop("pv")
            rows = pl.ds(u["q0"], ATTN_TQ)
            o_ref[u["sub"], rows, u["cols"]] = (jnp.where(first, pv[:ATTN_TQ], pv[ATTN_TQ:])
                                                * (1.0 / den)).astype(BF16)
            lse_ref[u["sub"], rows, u["cols"]] = (both_heads(u.pop("m"))
                                                  + jnp.log(den) * (1.0 / np.log(2.0)))

        _emit_skewed(units, (scores, row_max, probs, values, finish))

    if nb == blocks:
        run(range(n_sub), 0)
    else:
        def body(j, carry):
            run(range(n_sub), j * blocks)
            return carry

        lax.fori_loop(0, nb // blocks, body, 0)


def _attention(qkv, bias, name):
    n_seq, seq, _ = qkv.shape
    assert seq >= 2 * ATTN_TQ and seq % ATTN_TQ == 0
    if seq >= 4096:
        n_sub, n_pairs = 1, 2
    else:
        n_pairs = ATTN_WIDTH // LANES
        n_sub = max(1, min(n_seq, 1024 // seq))
    blocks = max(2, min(seq // ATTN_TQ, 16 // (n_sub * n_pairs)))
    assert (seq // ATTN_TQ) % blocks == 0
    cw = n_pairs * LANES
    pb = ATTN_WIDTH // cw
    kern = functools.partial(_attn_kernel, seq=seq, n_sub=n_sub, n_pairs=n_pairs, blocks=blocks)
    blk = (n_sub, seq, cw)
    return pl.pallas_call(
        kern,
        out_shape=(jax.ShapeDtypeStruct((n_seq, seq, ATTN_WIDTH), BF16),
                   jax.ShapeDtypeStruct((n_seq, seq, ATTN_WIDTH), F32)),
        grid=(n_seq // n_sub, pb),
        in_specs=[pl.BlockSpec(blk, lambda r, p: (r, 0, p)),
                  pl.BlockSpec(blk, lambda r, p: (r, 0, pb + p)),
                  pl.BlockSpec(blk, lambda r, p: (r, 0, 2 * pb + p)),
                  pl.BlockSpec((2 * n_pairs, 3, ATTN_TQ, ATTN_TK), lambda r, p: (p, 0, 0, 0))],
        out_specs=(pl.BlockSpec(blk, lambda r, p: (r, 0, p)),
                   pl.BlockSpec(blk, lambda r, p: (r, 0, p))),
        compiler_params=_cparams(("parallel", "parallel")),
        name=name,
    )(qkv, qkv, qkv, bias)


def _hgrn_mats():
    t = np.arange(HG_TILE)[:, None]
    u = np.arange(HG_TILE)[None, :]
    same = (t // HG_CHUNK) == (u // HG_CHUNK)
    return (jnp.asarray(same & (u <= t), dtype=BF16), jnp.asarray(same & (u >= t), dtype=BF16))


def _hgrn_kernel(xn_ref, wq_ref, wzf_ref, wzb_ref, wi_ref, wg_ref, wx_ref, lbf_ref, lbb_ref,
                 nw_ref, mf_ref, mb_ref, o_ref, x_out_ref, wcat_ref, gate_ref, acc_ref, qd_ref,
                 kv_ref, dec_ref, st_ref, *, seq):
    n_tiles = seq // HG_TILE
    n_chunks = seq // HG_CHUNK
    cpt = HG_TILE // HG_CHUNK
    mid = HG_CHUNK // 2
    nt = (((1,), (1,)), ((), ()))
    w_refs = (wq_ref, wzf_ref, wzb_ref, wi_ref, wg_ref, wx_ref)

    def per_chunk_rows(rows):
        return jnp.concatenate([jnp.broadcast_to(r, (HG_CHUNK, HG_DK)) for r in rows], axis=0)

    scans = ((lbf_ref, mf_ref, False), (lbb_ref, mb_ref, True))

    for k, w_ref in enumerate(w_refs):
        wcat_ref[:, k * HG_DK:(k + 1) * HG_DK] = w_ref[...]

    def a_project(u):
        z = jnp.dot(xn_ref[u["rows"], :], wcat_ref[...], preferred_element_type=F32)
        u["q"] = z[:, 0:HG_DK]
        u["z"] = [z[:, HG_DK:2 * HG_DK], z[:, 2 * HG_DK:3 * HG_DK]]
        u["v"] = z[:, 3 * HG_DK:4 * HG_DK].astype(BF16)
        gate_ref[u["rows"], :] = z[:, 4 * HG_DK:5 * HG_DK]
        x_out_ref[u["rows"], :] = z[:, 5 * HG_DK:].astype(BF16)

    def a_gate(u):
        u["kk"], u["ghl"] = [], []
        for (lb_ref, _, _), z in zip(scans, u.pop("z")):
            kk = (1.0 - lb_ref[...]) / (1.0 + jnp.exp(z))
            g = jnp.log(1.0 - kk) * (1.0 / np.log(2.0))
            g_hi = g.astype(BF16)
            g_lo = (g - g_hi.astype(F32)).astype(BF16)
            u["kk"].append(kk)
            u["ghl"].append(jnp.concatenate([g_hi, g_lo], axis=1))

    def a_cumsum(u):
        u["sums"] = [jnp.dot(m_ref[...], ghl, preferred_element_type=F32)
                     for (_, m_ref, _), ghl in zip(scans, u.pop("ghl"))]

    def a_decay(u):
        q = u.pop("q")
        q_c, k_c, q_d, k_s, decs = [], [], [], [], []
        for (_, _, reverse), sums, kk in zip(scans, u.pop("sums"), u.pop("kk")):
            beta = sums[:, :HG_DK] + sums[:, HG_DK:]
            tot_row = 0 if reverse else HG_CHUNK - 1
            cen_row = mid if reverse else mid - 1
            tot = [beta[c * HG_CHUNK + tot_row:c * HG_CHUNK + tot_row + 1] for c in range(cpt)]
            cen = [beta[c * HG_CHUNK + cen_row:c * HG_CHUNK + cen_row + 1] for c in range(cpt)]
            d = beta - per_chunk_rows(cen)
            e_q = jnp.exp2(d)
            qc = (q * e_q).astype(BF16)
            kc = (kk * (1.0 / e_q)).astype(BF16)
            q_d.append(qc * per_chunk_rows([jnp.exp2(c_).astype(BF16) for c_ in cen]))
            k_s.append(kc * per_chunk_rows([jnp.exp2(t_ - c_).astype(BF16)
                                            for t_, c_ in zip(tot, cen)]))
            q_c.append(qc)
            k_c.append(kc)
            decs.append([jnp.exp2(t_) for t_ in tot])
        qd_ref[u["rows"], :] = jnp.concatenate(q_d, axis=1)
        u["ks"] = jnp.concatenate(k_s, axis=1)
        u["qc"], u["kc"] = q_c, k_c
        for c in range(cpt):
            dec_ref[u["t"] * cpt + c] = jnp.concatenate(
                [jnp.broadcast_to(decs[0][c], (8, HG_DK)), jnp.broadcast_to(decs[1][c], (8, HG_DK))],
                axis=1)

    chunk_rows = [slice(c * HG_CHUNK, (c + 1) * HG_CHUNK) for c in range(cpt)]
    ti = lax.broadcasted_iota(jnp.int32, (HG_CHUNK, HG_CHUNK), 0)
    ui = lax.broadcasted_iota(jnp.int32, (HG_CHUNK, HG_CHUNK), 1)
    past, future = ui <= ti, ui >= ti

    def a_scores(u):
        u["a"] = [[lax.dot_general(qc[cr], kc[cr], nt, preferred_element_type=F32)
                   for qc, kc in zip(u["qc"], u["kc"])] for cr in chunk_rows]
        del u["qc"], u["kc"]

    def a_mask(u):
        u["a"] = [(jnp.where(past, a_f, 0.0) + jnp.where(future, a_b, 0.0)).astype(BF16)
                  for a_f, a_b in u.pop("a")]

    def a_values(u):
        v = u.pop("v")
        acc_ref[u["rows"], :] = jnp.concatenate(
            [jnp.dot(a, v[cr], preferred_element_type=F32) for a, cr in zip(u.pop("a"), chunk_rows)],
            axis=0)
        ks = u.pop("ks")
        for c, cr in enumerate(chunk_rows):
            kv_ref[u["t"] * cpt + c] = lax.dot_general(v[cr], ks[cr], (((0,), (0,)), ((), ())),
                                                       preferred_element_type=F32)

    a_stages = (a_project, a_gate, a_cumsum, a_decay, a_scores, a_mask, a_values)

    def phase_a(j, carry):
        units = []
        for k in range(HG_TILES_PER_TRIP):
            t = j * HG_TILES_PER_TRIP + k
            units.append(dict(t=t, rows=pl.ds(pl.multiple_of(t * HG_TILE, HG_TILE), HG_TILE)))
        _emit_skewed(units, a_stages)
        return carry

    lax.fori_loop(0, n_tiles // HG_TILES_PER_TRIP, phase_a, 0)

    fw = slice(0, HG_DK)
    bw = slice(HG_DK, 2 * HG_DK)

    def phase_b(n, carry):
        sf, sb = carry
        m = n_chunks - 1 - n
        st_ref[n, :, fw] = sf.astype(BF16)
        sf = dec_ref[n][0:1, fw] * sf + kv_ref[n, :, fw]
        st_ref[m, :, bw] = sb.astype(BF16)
        sb = dec_ref[m][0:1, bw] * sb + kv_ref[m, :, bw]
        return sf, sb

    s0 = jnp.zeros((HG_DK, HG_DK), F32)
    lax.fori_loop(0, n_chunks, phase_b, (s0, s0), unroll=4)

    def c_inter(u):
        u["inter"] = [lax.dot_general(qd_ref[pl.ds(u["r0"] + c * HG_CHUNK, HG_CHUNK), :],
                                      st_ref[u["t"] * cpt + c], nt, preferred_element_type=F32)
                      for c in range(cpt)]

    def c_sum(u):
        u["o"] = acc_ref[u["rows"], :] + jnp.concatenate(u.pop("inter"), axis=0)
        u["ms"] = jnp.mean(u["o"] * u["o"], axis=-1, keepdims=True)

    def c_out(u):
        gate = gate_ref[u["rows"], :]
        o_ref[u["rows"], :] = (u.pop("o") * lax.rsqrt(u.pop("ms") + EPS) * nw_ref[...]
                               * (gate * jax.nn.sigmoid(gate))).astype(BF16)

    def phase_c(j, carry):
        units = []
        for k in range(HG_TILES_PER_TRIP):
            t = j * HG_TILES_PER_TRIP + k
            r0 = pl.multiple_of(t * HG_TILE, HG_TILE)
            units.append(dict(t=t, r0=r0, rows=pl.ds(r0, HG_TILE)))
        _emit_skewed(units, (c_inter, c_sum, c_out))
        return carry

    lax.fori_loop(0, n_tiles // HG_TILES_PER_TRIP, phase_c, 0)


def _hgrn(xn3, w, lb_f, lb_b, nw, col_q, col_zf, col_zb, col_i, col_g, col_x):
    b, seq, d = xn3.shape
    mf, mb = _hgrn_mats()
    n_chunks = seq // HG_CHUNK

    def wspec(col):
        return pl.BlockSpec((d, HG_DK), lambda bi, h, col=col: (0, col + h))

    vec = pl.BlockSpec((1, HG_DK), lambda bi, h: (0, h))
    const = pl.BlockSpec((HG_TILE, HG_TILE), lambda bi, h: (0, 0))
    return pl.pallas_call(
        functools.partial(_hgrn_kernel, seq=seq),
        out_shape=[jax.ShapeDtypeStruct((b, seq, HG_HEADS * HG_DK), BF16)] * 2,
        grid=(b, HG_HEADS),
        in_specs=[pl.BlockSpec((None, seq, d), lambda bi, h: (bi, 0, 0)),
                  wspec(col_q), wspec(col_zf), wspec(col_zb), wspec(col_i), wspec(col_g),
                  wspec(col_x),
                  vec, vec, pl.BlockSpec((1, HG_DK), lambda bi, h: (0, 0)), const, const],
        out_specs=[pl.BlockSpec((None, seq, HG_DK), lambda bi, h: (bi, 0, h))] * 2,
        scratch_shapes=[pltpu.VMEM((d, 6 * HG_DK), BF16),
                        pltpu.VMEM((seq, HG_DK), F32),
                        pltpu.VMEM((seq, HG_DK), F32),
                        pltpu.VMEM((seq, 2 * HG_DK), BF16),
                        pltpu.VMEM((n_chunks, HG_DK, 2 * HG_DK), F32),
                        pltpu.VMEM((n_chunks, 8, 2 * HG_DK), F32),
                        pltpu.VMEM((n_chunks, HG_DK, 2 * HG_DK), BF16)],
        compiler_params=_cparams(("parallel", "parallel")),
        name="hgrn2",
    )(xn3, w, w, w, w, w, w, lb_f, lb_b, nw, mf, mb)


def _final_kernel(o0_ref, l0_ref, o1_ref, l1_ref, o2_ref, l2_ref, ga_ref, ob_ref,
                  za0_ref, za1_ref, zb0_ref, zb1_ref,
                  x_ref, wa_ref, wb_ref, wo_ref, out_ref, so1_ref, sl1_ref, so2_ref, sl2_ref,
                  *, tm, dilations):
    for src, dst, dil in ((o1_ref, so1_ref, dilations[0]), (l1_ref, sl1_ref, dilations[0]),
                          (o2_ref, so2_ref, dilations[1]), (l2_ref, sl2_ref, dilations[1])):
        for r in range(dil):
            val = src[r].astype(F32)
            for c in range(ATTN_WIDTH // LANES):
                dst[c, pl.ds(r, tm // dil, stride=dil), :] = val[:, c * LANES:(c + 1) * LANES]

    def natural(ref):
        return jnp.concatenate([ref[c] for c in range(ATTN_WIDTH // LANES)], axis=1)

    l0, l1, l2 = l0_ref[...], natural(sl1_ref), natural(sl2_ref)
    mx = jnp.maximum(jnp.maximum(l0, l1), l2)
    w0, w1, w2 = jnp.exp2(l0 - mx), jnp.exp2(l1 - mx), jnp.exp2(l2 - mx)
    num = w0 * o0_ref[...].astype(F32) + w1 * natural(so1_ref) + w2 * natural(so2_ref)
    ga = ga_ref[...].astype(F32)
    a = (num / (w0 + w1 + w2) * (ga * jax.nn.sigmoid(ga))).astype(BF16)

    y_a = jnp.dot(a, wa_ref[...], preferred_element_type=F32)
    y_b = jnp.dot(ob_ref[...], wb_ref[...], preferred_element_type=F32)
    za = jnp.concatenate([za0_ref[...], za1_ref[...]], axis=1).astype(F32)
    zb = jnp.concatenate([zb0_ref[...], zb1_ref[...]], axis=1).astype(F32)
    merged = (jax.nn.sigmoid(za) * y_a + jax.nn.sigmoid(zb) * y_b).astype(BF16)
    out_ref[...] = x_ref[...] + jnp.dot(merged, wo_ref[...], preferred_element_type=F32)


def _final(o_g, lse_g, z_gate, z_ga, o_b, x3, wa, wb, wo, dilations, col_ga, col_zb):
    b, seq, d = x3.shape
    tm = 512
    aw = ATTN_WIDTH

    def rows(width, col=0):
        return pl.BlockSpec((None, tm, width), lambda bi, i, col=col: (bi, i, col))

    def perm(dil):
        return pl.BlockSpec((None, dil, tm // dil, aw), lambda bi, i: (bi, 0, i, 0))

    def full(shape):
        return pl.BlockSpec(shape, lambda bi, i: (0, 0))

    d1, d2 = dilations
    return pl.pallas_call(
        functools.partial(_final_kernel, tm=tm, dilations=dilations),
        out_shape=jax.ShapeDtypeStruct((b, seq, d), F32),
        grid=(b, seq // tm),
        in_specs=[rows(aw), rows(aw), perm(d1), perm(d1), perm(d2), perm(d2),
                  rows(aw, col_ga), rows(d), rows(aw, 0), rows(aw, 1),
                  rows(aw, col_zb), rows(aw, col_zb + 1), rows(d),
                  full(wa.shape), full(wb.shape), full(wo.shape)],
        out_specs=rows(d),
        scratch_shapes=[pltpu.VMEM((aw // LANES, tm, LANES), F32)] * 4,
        compiler_params=_cparams(("parallel", "parallel")),
        name="merge_out",
    )(o_g[0], lse_g[0], o_g[1], lse_g[1], o_g[2], lse_g[2], z_gate, o_b,
      z_ga, z_ga, z_gate, z_gate, x3, wa, wb, wo)


def kernel(x, norm_w, w_in, q_norm_w, k_norm_w, rel_bias, lb_fwd, lb_bwd, hg_norm_w,
           w_proj_a, w_proj_b, w_out):
    b, seq, d = x.shape
    n = b * seq
    layer = 0
    w = w_in[layer].astype(BF16)
    qkv_cols = 3 * len(ATTN_GROUPS) * ATTN_WIDTH
    dilations = tuple(dil for _, dil in ATTN_GROUPS)
    assert dilations[0] == 1

    hw = HG_HEADS * HG_DK
    c_ga = qkv_cols // ATTN_WIDTH
    c_gate_b = c_ga + 1 + 6 * hw // ATTN_WIDTH
    z_gate, *xn_all = _norm_gates(x.astype(F32), norm_w[layer].reshape(1, d).astype(F32), w,
                                  dilations[1:], c_ga, c_gate_b, hw // ATTN_WIDTH)

    o_g, lse_g = [], []
    for g, dilation in enumerate(dilations):
        sub_len = seq // dilation
        q_scale = HEAD_DIM ** -0.5 * LOG2E
        nw = jnp.stack([jnp.tile(q_norm_w[layer, g].astype(F32), HEADS_PER_GROUP) * q_scale,
                        jnp.tile(k_norm_w[layer, g].astype(F32), HEADS_PER_GROUP)]
                       ).reshape(2, 1, ATTN_WIDTH)
        qkv = _proj_qkv(xn_all[g].reshape(n, d), w, g, nw, f"proj_qkv_d{dilation}")
        heads = slice(g * HEADS_PER_GROUP, (g + 1) * HEADS_PER_GROUP)
        bias = _attn_bias(rel_bias.astype(F32)[:, heads] * LOG2E, dilation)
        o, lse = _attention(qkv.reshape(b * dilation, sub_len, 3 * ATTN_WIDTH), bias,
                            f"attn_d{dilation}")
        shape = (b, seq, ATTN_WIDTH) if dilation == 1 else (b, dilation, sub_len, ATTN_WIDTH)
        o_g.append(o.reshape(shape))
        lse_g.append(lse.reshape(shape))

    lb_f = jnp.cumsum(jax.nn.softmax(lb_fwd.astype(F32), axis=0), axis=0)[layer].reshape(1, hw)
    lb_b = jnp.cumsum(jax.nn.softmax(lb_bwd.astype(F32), axis=0), axis=0)[layer].reshape(1, hw)
    hb = hw // LANES
    c_qb = (qkv_cols + ATTN_WIDTH) // LANES
    o_b, z_ga = _hgrn(xn_all[0], w, lb_f, lb_b, hg_norm_w[layer].reshape(1, HG_DK).astype(F32),
                      col_q=c_qb, col_zf=c_qb + hb, col_zb=c_qb + 2 * hb, col_i=c_qb + 3 * hb,
                      col_g=c_qb + 4 * hb, col_x=c_qb + 5 * hb)

    out = _final(o_g, lse_g, z_gate, z_ga, o_b, x.astype(F32),
                 w_proj_a[layer].astype(BF16), w_proj_b[layer].astype(BF16),
                 w_out[layer].astype(BF16), dilations[1:], col_ga=0, col_zb=1)
    return out.astype(x.dtype)
```

```python
import functools

import numpy as np
import jax
import jax.numpy as jnp
from jax import lax
from jax.experimental import pallas as pl
from jax.experimental.pallas import tpu as pltpu

F32 = jnp.float32
BF16 = jnp.bfloat16

EPS = 1e-6
LOG2E = float(np.log2(np.e))
NEG_INF = -1e30
ATTN_GROUPS = ((128, 1), (512, 4), (2048, 16))
HEAD_DIM = 64
HEADS_PER_GROUP = 8
ATTN_WIDTH = HEADS_PER_GROUP * HEAD_DIM
NUM_BUCKETS = 32
REL_MAX_DISTANCE = 1024
HG_HEADS = 8
HG_DK = 128

LANES = 128
SUBLANES = 8
PROJ_SUB = 512
VMEM_LIMIT = 52 * 1024 * 1024

ATTN_TQ = 128
ATTN_SIDE = 64
ATTN_TK = ATTN_TQ + 2 * ATTN_SIDE

HG_CHUNK = 64
HG_TILE = 256
HG_TILES_PER_TRIP = 16


def _cparams(sem):
    return pltpu.CompilerParams(dimension_semantics=sem, vmem_limit_bytes=VMEM_LIMIT)


def _emit_skewed(units, stages):
    for step in range(len(units) + len(stages) - 1):
        for k, stage in enumerate(stages):
            if 0 <= step - k < len(units):
                stage(units[step - k])


def _norm_kernel(x_ref, w_ref, o_ref, *rest, tm, dilations):
    perm_refs, y_ref = rest[:-1], rest[-1]
    x = x_ref[...]
    ms = jnp.mean(x * x, axis=-1, keepdims=True)
    y = x * lax.rsqrt(ms + EPS) * w_ref[...]
    o_ref[...] = y.astype(BF16)
    n_lane_blocks = y.shape[1] // LANES
    for c in range(n_lane_blocks):
        y_ref[c] = y[:, c * LANES:(c + 1) * LANES]
    for p_ref, dil in zip(perm_refs, dilations):
        for r in range(dil):
            for c in range(n_lane_blocks):
                p_ref[r, :, c * LANES:(c + 1) * LANES] = (
                    y_ref[c, pl.ds(r, tm // dil, stride=dil), :].astype(BF16))


def _norm(x3, w, dilations):
    b, seq, d = x3.shape
    tm = 1024
    out_shape = [jax.ShapeDtypeStruct((b, seq, d), BF16)]
    out_specs = [pl.BlockSpec((None, tm, d), lambda bi, i: (bi, i, 0))]
    for dil in dilations:
        out_shape.append(jax.ShapeDtypeStruct((b, dil, seq // dil, d), BF16))
        out_specs.append(pl.BlockSpec((None, dil, tm // dil, d), lambda bi, i: (bi, 0, i, 0)))
    return pl.pallas_call(
        functools.partial(_norm_kernel, tm=tm, dilations=dilations),
        out_shape=out_shape,
        grid=(b, seq // tm),
        in_specs=[pl.BlockSpec((None, tm, d), lambda bi, i: (bi, i, 0)),
                  pl.BlockSpec((1, d), lambda bi, i: (0, 0))],
        out_specs=out_specs,
        scratch_shapes=[pltpu.VMEM((d // LANES, tm, LANES), F32)],
        compiler_params=_cparams(("parallel", "parallel")),
        name="rmsnorm",
    )(x3, w)


def _proj_kernel(x_ref, w_ref, o_ref):
    o_ref[...] = jnp.dot(x_ref[...], w_ref[...], preferred_element_type=F32).astype(o_ref.dtype)


def _proj_gates(xn, w, col_a, col_b, n_b):
    n, d = xn.shape
    tm, tn = 4096, ATTN_WIDTH
    return pl.pallas_call(
        _proj_kernel,
        out_shape=jax.ShapeDtypeStruct((n, (1 + n_b) * tn), BF16),
        grid=(n // tm, 1 + n_b),
        in_specs=[pl.BlockSpec((tm, d), lambda i, j: (i, 0)),
                  pl.BlockSpec((d, tn), lambda i, j: (0, jnp.where(j == 0, col_a, col_b - 1 + j)))],
        out_specs=pl.BlockSpec((tm, tn), lambda i, j: (i, j)),
        compiler_params=_cparams(("parallel", "arbitrary")),
        name="proj_gates",
    )(xn, w)


def _proj_qkv_kernel(x_ref, w_ref, nw_ref, seg_ref, o_ref):
    j = pl.program_id(1)

    @pl.when(j < 2)
    def _():
        units = [dict(rows=slice(r * PROJ_SUB, (r + 1) * PROJ_SUB))
                 for r in range(x_ref.shape[0] // PROJ_SUB)]

        def project(u):
            u["z"] = jnp.dot(x_ref[u["rows"], :], w_ref[...], preferred_element_type=F32)

        def square(u):
            u["zz"] = (u["z"] * u["z"]).astype(BF16)

        def mean_square(u):
            u["ms"] = jnp.dot(u.pop("zz"), seg_ref[...], preferred_element_type=F32)

        def normalise(u):
            o_ref[u["rows"], :] = (u.pop("z") * lax.rsqrt(u.pop("ms") + EPS) * nw_ref[...]
                                   ).astype(BF16)

        _emit_skewed(units, (project, square, mean_square, normalise))

    @pl.when(j == 2)
    def _():
        o_ref[...] = jnp.dot(x_ref[...], w_ref[...], preferred_element_type=F32).astype(BF16)


def _proj_qkv(xn2, w, group, nw, name):
    n, d = xn2.shape
    tm, tn = 4096, ATTN_WIDTH
    head = np.arange(tn) // HEAD_DIM
    seg = jnp.asarray((head[:, None] == head[None, :]) / HEAD_DIM, dtype=BF16)
    return pl.pallas_call(
        _proj_qkv_kernel,
        out_shape=jax.ShapeDtypeStruct((n, 3 * tn), BF16),
        grid=(n // tm, 3),
        in_specs=[pl.BlockSpec((tm, d), lambda i, j: (i, 0)),
                  pl.BlockSpec((d, tn), lambda i, j: (0, 3 * group + j)),
                  pl.BlockSpec((None, 1, tn), lambda i, j: (jnp.minimum(j, 1), 0, 0)),
                  pl.BlockSpec((tn, tn), lambda i, j: (0, 0))],
        out_specs=pl.BlockSpec((tm, tn), lambda i, j: (i, j)),
        compiler_params=_cparams(("parallel", "arbitrary")),
        name=name,
    )(xn2, w, nw, seg)


def _t5_bucket(rel):
    half = NUM_BUCKETS // 2
    max_exact = half // 2
    n = np.abs(rel)
    large = max_exact + (np.log(np.maximum(n, 1) / max_exact)
                         / np.log(REL_MAX_DISTANCE / max_exact) * (half - max_exact)).astype(np.int32)
    large = np.minimum(large, half - 1)
    return np.where(rel > 0, half, 0) + np.where(n < max_exact, n, large)


def _attn_bias(bias_tab, dilation):
    t = np.arange(ATTN_TQ)[:, None]
    j = np.arange(ATTN_TK)[None, :]
    rel = np.stack([j - off - t for off in (0, ATTN_SIDE, 2 * ATTN_SIDE)])
    valid = (np.abs(rel) <= ATTN_SIDE).reshape(-1)
    bucket = _t5_bucket(rel * dilation).reshape(-1).astype(np.int32)
    onehot = (jnp.asarray(bucket)[None, :] == jnp.arange(NUM_BUCKETS, dtype=jnp.int32)[:, None])
    vals = jnp.dot(bias_tab.T, onehot.astype(F32), precision=lax.Precision.HIGHEST)
    vals = jnp.where(jnp.asarray(valid)[None, :], vals, NEG_INF)
    return vals.reshape(HEADS_PER_GROUP, 3, ATTN_TQ, ATTN_TK)


def _attn_kernel(q_ref, k_ref, v_ref, bias_ref, o_ref, lse_ref, *, seq, n_sub, n_pairs, blocks):
    nb = seq // ATTN_TQ
    lane = lax.broadcasted_iota(jnp.int32, (ATTN_TQ, LANES), 1)
    first = lane < HEAD_DIM
    nt = (((1,), (1,)), ((), ()))

    def both_heads(col):
        return jnp.where(first, jnp.broadcast_to(col[:ATTN_TQ], (ATTN_TQ, LANES)),
                         jnp.broadcast_to(col[ATTN_TQ:], (ATTN_TQ, LANES)))

    def run(subs, i0):
        units = []
        for sub in subs:
            for bo in range(blocks):
                i = i0 + bo
                q0 = pl.multiple_of(i * ATTN_TQ, ATTN_TQ)
                ks = pl.multiple_of(jnp.clip(i * ATTN_TQ - ATTN_SIDE, 0, seq - ATTN_TK), ATTN_SIDE)
                var = jnp.where(i == 0, 0, jnp.where(i == nb - 1, 2, 1))
                for pr in range(n_pairs):
                    units.append(dict(sub=sub, q0=q0, ks=ks, var=var, pr=pr,
                                      cols=slice(pr * LANES, (pr + 1) * LANES)))

        def scores(u):
            q2 = q_ref[u["sub"], pl.ds(u["q0"], ATTN_TQ), u["cols"]]
            k2 = k_ref[u["sub"], pl.ds(u["ks"], ATTN_TK), u["cols"]]
            zero = jnp.zeros_like(q2)
            qq = jnp.concatenate([jnp.where(first, q2, zero), jnp.where(first, zero, q2)], axis=0)
            u["s"] = lax.dot_general(qq, k2, nt, preferred_element_type=F32)

        def row_max(u):
            pr, var = u["pr"], u["var"]
            u["s"] = u["s"] + jnp.concatenate([bias_ref[2 * pr, var], bias_ref[2 * pr + 1, var]],
                                              axis=0)
            u["m"] = jnp.max(u["s"], axis=-1, keepdims=True)

        def probs(u):
            p = jnp.exp2(u.pop("s") - u["m"])
            u["den"] = jnp.sum(p, axis=-1, keepdims=True)
            u["p"] = p.astype(BF16)

        def values(u):
            v2 = v_ref[u["sub"], pl.ds(u["ks"], ATTN_TK), u["cols"]]
            u["pv"] = jnp.dot(u.pop("p"), v2, preferred_element_type=F32)

        def finish(u):
            den = both_heads(u.pop("den"))
            pv = u.pop("pv")
            rows = pl.ds(u["q0"], ATTN_TQ)
            o_ref[u["sub"], rows, u["cols"]] = (jnp.where(first, pv[:ATTN_TQ], pv[ATTN_TQ:])
                                                * (1.0 / den)).astype(BF16)
            lse_ref[u["sub"], rows, u["cols"]] = (both_heads(u.pop("m"))
                                                  + jnp.log(den) * (1.0 / np.log(2.0)))

        _emit_skewed(units, (scores, row_max, probs, values, finish))

    if nb == blocks:
        run(range(n_sub), 0)
    else:
        def body(j, carry):
            run(range(n_sub), j * blocks)
            return carry

        lax.fori_loop(0, nb // blocks, body, 0)


def _attention(qkv, bias, name):
    n_seq, seq, _ = qkv.shape
    assert seq >= 2 * ATTN_TQ and seq % ATTN_TQ == 0
    if seq >= 4096:
        n_sub, n_pairs = 1, 2
    else:
        n_pairs = ATTN_WIDTH // LANES
        n_sub = max(1, min(n_seq, 1024 // seq))
    blocks = max(2, min(seq // ATTN_TQ, 16 // (n_sub * n_pairs)))
    assert (seq // ATTN_TQ) % blocks == 0
    cw = n_pairs * LANES
    pb = ATTN_WIDTH // cw
    kern = functools.partial(_attn_kernel, seq=seq, n_sub=n_sub, n_pairs=n_pairs, blocks=blocks)
    blk = (n_sub, seq, cw)
    return pl.pallas_call(
        kern,
        out_shape=(jax.ShapeDtypeStruct((n_seq, seq, ATTN_WIDTH), BF16),
                   jax.ShapeDtypeStruct((n_seq, seq, ATTN_WIDTH), F32)),
        grid=(n_seq // n_sub, pb),
        in_specs=[pl.BlockSpec(blk, lambda r, p: (r, 0, p)),
                  pl.BlockSpec(blk, lambda r, p: (r, 0, pb + p)),
                  pl.BlockSpec(blk, lambda r, p: (r, 0, 2 * pb + p)),
                  pl.BlockSpec((2 * n_pairs, 3, ATTN_TQ, ATTN_TK), lambda r, p: (p, 0, 0, 0))],
        out_specs=(pl.BlockSpec(blk, lambda r, p: (r, 0, p)),
                   pl.BlockSpec(blk, lambda r, p: (r, 0, p))),
        compiler_params=_cparams(("parallel", "parallel")),
        name=name,
    )(qkv, qkv, qkv, bias)


def _hgrn_mats():
    t = np.arange(HG_TILE)[:, None]
    u = np.arange(HG_TILE)[None, :]
    same = (t // HG_CHUNK) == (u // HG_CHUNK)
    return (jnp.asarray(same & (u <= t), dtype=BF16), jnp.asarray(same & (u >= t), dtype=BF16))


def _hgrn_kernel(xn_ref, wq_ref, wzf_ref, wzb_ref, wi_ref, wg_ref, wx_ref, lbf_ref, lbb_ref,
                 nw_ref, mf_ref, mb_ref, o_ref, x_out_ref, wcat_ref, gate_ref, acc_ref, qd_ref,
                 kv_ref, dec_ref, st_ref, *, seq):
    n_tiles = seq // HG_TILE
    n_chunks = seq // HG_CHUNK
    cpt = HG_TILE // HG_CHUNK
    mid = HG_CHUNK // 2
    nt = (((1,), (1,)), ((), ()))
    w_refs = (wq_ref, wzf_ref, wzb_ref, wi_ref, wg_ref, wx_ref)

    def per_chunk_rows(rows):
        return jnp.concatenate([jnp.broadcast_to(r, (HG_CHUNK, HG_DK)) for r in rows], axis=0)

    scans = ((lbf_ref, mf_ref, False), (lbb_ref, mb_ref, True))

    for k, w_ref in enumerate(w_refs):
        wcat_ref[:, k * HG_DK:(k + 1) * HG_DK] = w_ref[...]

    def a_project(u):
        z = jnp.dot(xn_ref[u["rows"], :], wcat_ref[...], preferred_element_type=F32)
        u["q"] = z[:, 0:HG_DK]
        u["z"] = [z[:, HG_DK:2 * HG_DK], z[:, 2 * HG_DK:3 * HG_DK]]
        u["v"] = z[:, 3 * HG_DK:4 * HG_DK].astype(BF16)
        gate_ref[u["rows"], :] = z[:, 4 * HG_DK:5 * HG_DK]
        x_out_ref[u["rows"], :] = z[:, 5 * HG_DK:].astype(BF16)

    def a_gate(u):
        u["kk"], u["ghl"] = [], []
        for (lb_ref, _, _), z in zip(scans, u.pop("z")):
            kk = (1.0 - lb_ref[...]) / (1.0 + jnp.exp(z))
            g = jnp.log(1.0 - kk) * (1.0 / np.log(2.0))
            g_hi = g.astype(BF16)
            g_lo = (g - g_hi.astype(F32)).astype(BF16)
            u["kk"].append(kk)
            u["ghl"].append(jnp.concatenate([g_hi, g_lo], axis=1))

    def a_cumsum(u):
        u["sums"] = [jnp.dot(m_ref[...], ghl, preferred_element_type=F32)
                     for (_, m_ref, _), ghl in zip(scans, u.pop("ghl"))]

    def a_decay(u):
        q = u.pop("q")
        q_c, k_c, q_d, k_s, decs = [], [], [], [], []
        for (_, _, reverse), sums, kk in zip(scans, u.pop("sums"), u.pop("kk")):
            beta = sums[:, :HG_DK] + sums[:, HG_DK:]
            tot_row = 0 if reverse else HG_CHUNK - 1
            cen_row = mid if reverse else mid - 1
            tot = [beta[c * HG_CHUNK + tot_row:c * HG_CHUNK + tot_row + 1] for c in range(cpt)]
            cen = [beta[c * HG_CHUNK + cen_row:c * HG_CHUNK + cen_row + 1] for c in range(cpt)]
            d = beta - per_chunk_rows(cen)
            e_q = jnp.exp2(d)
            qc = (q * e_q).astype(BF16)
            kc = (kk * (1.0 / e_q)).astype(BF16)
            q_d.append(qc * per_chunk_rows([jnp.exp2(c_).astype(BF16) for c_ in cen]))
            k_s.append(kc * per_chunk_rows([jnp.exp2(t_ - c_).astype(BF16)
                                            for t_, c_ in zip(tot, cen)]))
            q_c.append(qc)
            k_c.append(kc)
            decs.append([jnp.exp2(t_) for t_ in tot])
        qd_ref[u["rows"], :] = jnp.concatenate(q_d, axis=1)
        u["ks"] = jnp.concatenate(k_s, axis=1)
        u["qc"], u["kc"] = q_c, k_c
        for c in range(cpt):
            dec_ref[u["t"] * cpt + c] = jnp.concatenate(
                [jnp.broadcast_to(decs[0][c], (SUBLANES, HG_DK)),
                 jnp.broadcast_to(decs[1][c], (SUBLANES, HG_DK))],
                axis=1)

    chunk_rows = [slice(c * HG_CHUNK, (c + 1) * HG_CHUNK) for c in range(cpt)]
    ti = lax.broadcasted_iota(jnp.int32, (HG_CHUNK, HG_CHUNK), 0)
    ui = lax.broadcasted_iota(jnp.int32, (HG_CHUNK, HG_CHUNK), 1)
    past, future = ui <= ti, ui >= ti

    def a_scores(u):
        u["a"] = [[lax.dot_general(qc[cr], kc[cr], nt, preferred_element_type=F32)
                   for qc, kc in zip(u["qc"], u["kc"])] for cr in chunk_rows]
        del u["qc"], u["kc"]

    def a_mask(u):
        u["a"] = [(jnp.where(past, a_f, 0.0) + jnp.where(future, a_b, 0.0)).astype(BF16)
                  for a_f, a_b in u.pop("a")]

    def a_values(u):
        v = u.pop("v")
        acc_ref[u["rows"], :] = jnp.concatenate(
            [jnp.dot(a, v[cr], preferred_element_type=F32) for a, cr in zip(u.pop("a"), chunk_rows)],
            axis=0)
        ks = u.pop("ks")
        for c, cr in enumerate(chunk_rows):
            kv_ref[u["t"] * cpt + c] = lax.dot_general(v[cr], ks[cr], (((0,), (0,)), ((), ())),
                                                       preferred_element_type=F32)

    a_stages = (a_project, a_gate, a_cumsum, a_decay, a_scores, a_mask, a_values)

    def phase_a(j, carry):
        units = []
        for k in range(HG_TILES_PER_TRIP):
            t = j * HG_TILES_PER_TRIP + k
            units.append(dict(t=t, rows=pl.ds(pl.multiple_of(t * HG_TILE, HG_TILE), HG_TILE)))
        _emit_skewed(units, a_stages)
        return carry

    lax.fori_loop(0, n_tiles // HG_TILES_PER_TRIP, phase_a, 0)

    fw = slice(0, HG_DK)
    bw = slice(HG_DK, 2 * HG_DK)

    def phase_b(n, carry):
        sf, sb = carry
        m = n_chunks - 1 - n
        st_ref[n, :, fw] = sf.astype(BF16)
        sf = dec_ref[n][0:1, fw] * sf + kv_ref[n, :, fw]
        st_ref[m, :, bw] = sb.astype(BF16)
        sb = dec_ref[m][0:1, bw] * sb + kv_ref[m, :, bw]
        return sf, sb

    s0 = jnp.zeros((HG_DK, HG_DK), F32)
    lax.fori_loop(0, n_chunks, phase_b, (s0, s0), unroll=4)

    def c_inter(u):
        u["inter"] = [lax.dot_general(qd_ref[pl.ds(u["r0"] + c * HG_CHUNK, HG_CHUNK), :],
                                      st_ref[u["t"] * cpt + c], nt, preferred_element_type=F32)
                      for c in range(cpt)]

    def c_sum(u):
        u["o"] = acc_ref[u["rows"], :] + jnp.concatenate(u.pop("inter"), axis=0)
        u["ms"] = jnp.mean(u["o"] * u["o"], axis=-1, keepdims=True)

    def c_out(u):
        gate = gate_ref[u["rows"], :]
        o_ref[u["rows"], :] = (u.pop("o") * lax.rsqrt(u.pop("ms") + EPS) * nw_ref[...]
                               * (gate * jax.nn.sigmoid(gate))).astype(BF16)

    def phase_c(j, carry):
        units = []
        for k in range(HG_TILES_PER_TRIP):
            t = j * HG_TILES_PER_TRIP + k
            r0 = pl.multiple_of(t * HG_TILE, HG_TILE)
            units.append(dict(t=t, r0=r0, rows=pl.ds(r0, HG_TILE)))
        _emit_skewed(units, (c_inter, c_sum, c_out))
        return carry

    lax.fori_loop(0, n_tiles // HG_TILES_PER_TRIP, phase_c, 0)


def _hgrn(xn3, w, lb_f, lb_b, nw, col_q, col_zf, col_zb, col_i, col_g, col_x):
    b, seq, d = xn3.shape
    mf, mb = _hgrn_mats()
    n_chunks = seq // HG_CHUNK

    def wspec(col):
        return pl.BlockSpec((d, HG_DK), lambda bi, h, col=col: (0, col + h))

    vec = pl.BlockSpec((1, HG_DK), lambda bi, h: (0, h))
    const = pl.BlockSpec((HG_TILE, HG_TILE), lambda bi, h: (0, 0))
    return pl.pallas_call(
        functools.partial(_hgrn_kernel, seq=seq),
        out_shape=[jax.ShapeDtypeStruct((b, seq, HG_HEADS * HG_DK), BF16)] * 2,
        grid=(b, HG_HEADS),
        in_specs=[pl.BlockSpec((None, seq, d), lambda bi, h: (bi, 0, 0)),
                  wspec(col_q), wspec(col_zf), wspec(col_zb), wspec(col_i), wspec(col_g),
                  wspec(col_x),
                  vec, vec, pl.BlockSpec((1, HG_DK), lambda bi, h: (0, 0)), const, const],
        out_specs=[pl.BlockSpec((None, seq, HG_DK), lambda bi, h: (bi, 0, h))] * 2,
        scratch_shapes=[pltpu.VMEM((d, 6 * HG_DK), BF16),
                        pltpu.VMEM((seq, HG_DK), F32),
                        pltpu.VMEM((seq, HG_DK), F32),
                        pltpu.VMEM((seq, 2 * HG_DK), BF16),
                        pltpu.VMEM((n_chunks, HG_DK, 2 * HG_DK), F32),
                        pltpu.VMEM((n_chunks, SUBLANES, 2 * HG_DK), F32),
                        pltpu.VMEM((n_chunks, HG_DK, 2 * HG_DK), BF16)],
        compiler_params=_cparams(("parallel", "parallel")),
        name="hgrn2",
    )(xn3, w, w, w, w, w, w, lb_f, lb_b, nw, mf, mb)


def _final_kernel(o0_ref, l0_ref, o1_ref, l1_ref, o2_ref, l2_ref, ga_ref, ob_ref,
                  za0_ref, za1_ref, zb0_ref, zb1_ref,
                  x_ref, wa_ref, wb_ref, wo_ref, out_ref, so1_ref, sl1_ref, so2_ref, sl2_ref,
                  *, tm, dilations):
    for src, dst, dil in ((o1_ref, so1_ref, dilations[0]), (l1_ref, sl1_ref, dilations[0]),
                          (o2_ref, so2_ref, dilations[1]), (l2_ref, sl2_ref, dilations[1])):
        for r in range(dil):
            val = src[r].astype(F32)
            for c in range(ATTN_WIDTH // LANES):
                dst[c, pl.ds(r, tm // dil, stride=dil), :] = val[:, c * LANES:(c + 1) * LANES]

    def natural(ref):
        return jnp.concatenate([ref[c] for c in range(ATTN_WIDTH // LANES)], axis=1)

    l0, l1, l2 = l0_ref[...], natural(sl1_ref), natural(sl2_ref)
    mx = jnp.maximum(jnp.maximum(l0, l1), l2)
    w0, w1, w2 = jnp.exp2(l0 - mx), jnp.exp2(l1 - mx), jnp.exp2(l2 - mx)
    num = w0 * o0_ref[...].astype(F32) + w1 * natural(so1_ref) + w2 * natural(so2_ref)
    ga = ga_ref[...].astype(F32)
    a = (num / (w0 + w1 + w2) * (ga * jax.nn.sigmoid(ga))).astype(BF16)

    y_a = jnp.dot(a, wa_ref[...], preferred_element_type=F32)
    y_b = jnp.dot(ob_ref[...], wb_ref[...], preferred_element_type=F32)
    za = jnp.concatenate([za0_ref[...], za1_ref[...]], axis=1).astype(F32)
    zb = jnp.concatenate([zb0_ref[...], zb1_ref[...]], axis=1).astype(F32)
    merged = (jax.nn.sigmoid(za) * y_a + jax.nn.sigmoid(zb) * y_b).astype(BF16)
    out_ref[...] = x_ref[...] + jnp.dot(merged, wo_ref[...], preferred_element_type=F32)


def _final(o_g, lse_g, z_gate, z_ga, o_b, x3, wa, wb, wo, dilations, col_ga, col_zb):
    b, seq, d = x3.shape
    tm = 512
    aw = ATTN_WIDTH

    def rows(width, col=0):
        return pl.BlockSpec((None, tm, width), lambda bi, i, col=col: (bi, i, col))

    def perm(dil):
        return pl.BlockSpec((None, dil, tm // dil, aw), lambda bi, i: (bi, 0, i, 0))

    def full(shape):
        return pl.BlockSpec(shape, lambda bi, i: (0, 0))

    d1, d2 = dilations
    return pl.pallas_call(
        functools.partial(_final_kernel, tm=tm, dilations=dilations),
        out_shape=jax.ShapeDtypeStruct((b, seq, d), F32),
        grid=(b, seq // tm),
        in_specs=[rows(aw), rows(aw), perm(d1), perm(d1), perm(d2), perm(d2),
                  rows(aw, col_ga), rows(d), rows(aw, 0), rows(aw, 1),
                  rows(aw, col_zb), rows(aw, col_zb + 1), rows(d),
                  full(wa.shape), full(wb.shape), full(wo.shape)],
        out_specs=rows(d),
        scratch_shapes=[pltpu.VMEM((aw // LANES, tm, LANES), F32)] * 4,
        compiler_params=_cparams(("parallel", "parallel")),
        name="merge_out",
    )(o_g[0], lse_g[0], o_g[1], lse_g[1], o_g[2], lse_g[2], z_gate, o_b,
      z_ga, z_ga, z_gate, z_gate, x3, wa, wb, wo)


def kernel(x, norm_w, w_in, q_norm_w, k_norm_w, rel_bias, lb_fwd, lb_bwd, hg_norm_w,
           w_proj_a, w_proj_b, w_out):
    b, seq, d = x.shape
    n = b * seq
    layer = 0
    w = w_in[layer].astype(BF16)
    qkv_cols = 3 * len(ATTN_GROUPS) * ATTN_WIDTH
    dilations = tuple(dil for _, dil in ATTN_GROUPS)
    assert dilations[0] == 1

    xn_all = _norm(x.astype(F32), norm_w[layer].reshape(1, d).astype(F32), dilations[1:])
    xn = xn_all[0].reshape(n, d)

    hw = HG_HEADS * HG_DK
    c_ga = qkv_cols // ATTN_WIDTH
    c_gate_b = c_ga + 1 + 6 * hw // ATTN_WIDTH
    z_gate = _proj_gates(xn, w, c_ga, c_gate_b, hw // ATTN_WIDTH).reshape(b, seq, -1)

    o_g, lse_g = [], []
    for g, dilation in enumerate(dilations):
        sub_len = seq // dilation
        q_scale = HEAD_DIM ** -0.5 * LOG2E
        nw = jnp.stack([jnp.tile(q_norm_w[layer, g].astype(F32), HEADS_PER_GROUP) * q_scale,
                        jnp.tile(k_norm_w[layer, g].astype(F32), HEADS_PER_GROUP)]
                       ).reshape(2, 1, ATTN_WIDTH)
        qkv = _proj_qkv(xn_all[g].reshape(n, d), w, g, nw, f"proj_qkv_d{dilation}")
        heads = slice(g * HEADS_PER_GROUP, (g + 1) * HEADS_PER_GROUP)
        bias = _attn_bias(rel_bias.astype(F32)[:, heads] * LOG2E, dilation)
        o, lse = _attention(qkv.reshape(b * dilation, sub_len, 3 * ATTN_WIDTH), bias,
                            f"attn_d{dilation}")
        shape = (b, seq, ATTN_WIDTH) if dilation == 1 else (b, dilation, sub_len, ATTN_WIDTH)
        o_g.append(o.reshape(shape))
        lse_g.append(lse.reshape(shape))

    lb_f = jnp.cumsum(jax.nn.softmax(lb_fwd.astype(F32), axis=0), axis=0)[layer].reshape(1, hw)
    lb_b = jnp.cumsum(jax.nn.softmax(lb_bwd.astype(F32), axis=0), axis=0)[layer].reshape(1, hw)
    hb = hw // LANES
    c_qb = (qkv_cols + ATTN_WIDTH) // LANES
    o_b, z_ga = _hgrn(xn_all[0], w, lb_f, lb_b, hg_norm_w[layer].reshape(1, HG_DK).astype(F32),
                      col_q=c_qb, col_zf=c_qb + hb, col_zb=c_qb + 2 * hb, col_i=c_qb + 3 * hb,
                      col_g=c_qb + 4 * hb, col_x=c_qb + 5 * hb)

    out = _final(o_g, lse_g, z_gate, z_ga, o_b, x.astype(F32),
                 w_proj_a[layer].astype(BF16), w_proj_b[layer].astype(BF16),
                 w_out[layer].astype(BF16), dilations[1:], col_ga=0, col_zb=1)
    return out.astype(x.dtype)
```

```python
import functools

import numpy as np
import jax
import jax.numpy as jnp
from jax import lax
from jax.experimental import pallas as pl
from jax.experimental.pallas import tpu as pltpu

F32 = jnp.float32
BF16 = jnp.bfloat16

EPS = 1e-6
LOG2E = float(np.log2(np.e))
NEG_INF = -1e30
ATTN_GROUPS = ((128, 1), (512, 4), (2048, 16))
HEAD_DIM = 64
HEADS_PER_GROUP = 8
ATTN_WIDTH = HEADS_PER_GROUP * HEAD_DIM
NUM_BUCKETS = 32
REL_MAX_DISTANCE = 1024
HG_HEADS = 8
HG_DK = 128

LANES = 128
SUBLANES = 8
PROJ_SUB = 512
VMEM_LIMIT = 52 * 1024 * 1024

ATTN_TQ = 128
ATTN_SIDE = 64
ATTN_TK = ATTN_TQ + 2 * ATTN_SIDE

HG_CHUNK = 64
HG_TILE = 256
HG_TILES_PER_TRIP = 16


def _cparams(sem):
    return pltpu.CompilerParams(dimension_semantics=sem, vmem_limit_bytes=VMEM_LIMIT)


def _emit_skewed(units, stages):
    for step in range(len(units) + len(stages) - 1):
        for k, stage in enumerate(stages):
            if 0 <= step - k < len(units):
                stage(units[step - k])


def _norm_kernel(x_ref, w_ref, o_ref, *rest, tm, dilations):
    perm_refs, y_ref = rest[:-1], rest[-1]
    x = x_ref[...]
    ms = jnp.mean(x * x, axis=-1, keepdims=True)
    y = x * lax.rsqrt(ms + EPS) * w_ref[...]
    o_ref[...] = y.astype(BF16)
    n_lane_blocks = y.shape[1] // LANES
    for c in range(n_lane_blocks):
        y_ref[c] = y[:, c * LANES:(c + 1) * LANES]
    for p_ref, dil in zip(perm_refs, dilations):
        for r in range(dil):
            for c in range(n_lane_blocks):
                p_ref[r, :, c * LANES:(c + 1) * LANES] = (
                    y_ref[c, pl.ds(r, tm // dil, stride=dil), :].astype(BF16))


def _norm(x3, w, dilations):
    b, seq, d = x3.shape
    tm = 1024
    out_shape = [jax.ShapeDtypeStruct((b, seq, d), BF16)]
    out_specs = [pl.BlockSpec((None, tm, d), lambda bi, i: (bi, i, 0))]
    for dil in dilations:
        out_shape.append(jax.ShapeDtypeStruct((b, dil, seq // dil, d), BF16))
        out_specs.append(pl.BlockSpec((None, dil, tm // dil, d), lambda bi, i: (bi, 0, i, 0)))
    return pl.pallas_call(
        functools.partial(_norm_kernel, tm=tm, dilations=dilations),
        out_shape=out_shape,
        grid=(b, seq // tm),
        in_specs=[pl.BlockSpec((None, tm, d), lambda bi, i: (bi, i, 0)),
                  pl.BlockSpec((1, d), lambda bi, i: (0, 0))],
        out_specs=out_specs,
        scratch_shapes=[pltpu.VMEM((d // LANES, tm, LANES), F32)],
        compiler_params=_cparams(("parallel", "parallel")),
        name="rmsnorm",
    )(x3, w)


def _proj_kernel(x_ref, w_ref, o_ref):
    o_ref[...] = jnp.dot(x_ref[...], w_ref[...], preferred_element_type=F32).astype(o_ref.dtype)


def _proj_gates(xn, w, col_a, col_b, n_b):
    n, d = xn.shape
    tm, tn = 4096, ATTN_WIDTH
    return pl.pallas_call(
        _proj_kernel,
        out_shape=jax.ShapeDtypeStruct((n, (1 + n_b) * tn), BF16),
        grid=(n // tm, 1 + n_b),
        in_specs=[pl.BlockSpec((tm, d), lambda i, j: (i, 0)),
                  pl.BlockSpec((d, tn), lambda i, j: (0, jnp.where(j == 0, col_a, col_b - 1 + j)))],
        out_specs=pl.BlockSpec((tm, tn), lambda i, j: (i, j)),
        compiler_params=_cparams(("parallel", "arbitrary")),
        name="proj_gates",
    )(xn, w)


def _proj_qkv_kernel(x_ref, w_ref, nw_ref, seg_ref, o_ref):
    j = pl.program_id(1)

    @pl.when(j < 2)
    def _():
        units = [dict(rows=slice(r * PROJ_SUB, (r + 1) * PROJ_SUB))
                 for r in range(x_ref.shape[0] // PROJ_SUB)]

        def project(u):
            u["z"] = jnp.dot(x_ref[u["rows"], :], w_ref[...], preferred_element_type=F32)

        def square(u):
            u["zz"] = (u["z"] * u["z"]).astype(BF16)

        def mean_square(u):
            u["ms"] = jnp.dot(u.pop("zz"), seg_ref[...], preferred_element_type=F32)

        def normalise(u):
            o_ref[u["rows"], :] = (u.pop("z") * lax.rsqrt(u.pop("ms") + EPS) * nw_ref[...]
                                   ).astype(BF16)

        _emit_skewed(units, (project, square, mean_square, normalise))

    @pl.when(j == 2)
    def _():
        o_ref[...] = jnp.dot(x_ref[...], w_ref[...], preferred_element_type=F32).astype(BF16)


def _proj_qkv(xn2, w, group, nw, name):
    n, d = xn2.shape
    tm, tn = 4096, ATTN_WIDTH
    head = np.arange(tn) // HEAD_DIM
    seg = jnp.asarray((head[:, None] == head[None, :]) / HEAD_DIM, dtype=BF16)
    return pl.pallas_call(
        _proj_qkv_kernel,
        out_shape=jax.ShapeDtypeStruct((n, 3 * tn), BF16),
        grid=(n // tm, 3),
        in_specs=[pl.BlockSpec((tm, d), lambda i, j: (i, 0)),
                  pl.BlockSpec((d, tn), lambda i, j: (0, 3 * group + j)),
                  pl.BlockSpec((None, 1, tn), lambda i, j: (jnp.minimum(j, 1), 0, 0)),
                  pl.BlockSpec((tn, tn), lambda i, j: (0, 0))],
        out_specs=pl.BlockSpec((tm, tn), lambda i, j: (i, j)),
        compiler_params=_cparams(("parallel", "arbitrary")),
        name=name,
    )(xn2, w, nw, seg)


def _t5_bucket(rel):
    half = NUM_BUCKETS // 2
    max_exact = half // 2
    n = np.abs(rel)
    large = max_exact + (np.log(np.maximum(n, 1) / max_exact)
                         / np.log(REL_MAX_DISTANCE / max_exact) * (half - max_exact)).astype(np.int32)
    large = np.minimum(large, half - 1)
    return np.where(rel > 0, half, 0) + np.where(n < max_exact, n, large)


def _attn_bias(bias_tab, dilation):
    t = np.arange(ATTN_TQ)[:, None]
    j = np.arange(ATTN_TK)[None, :]
    rel = np.stack([j - off - t for off in (0, ATTN_SIDE, 2 * ATTN_SIDE)])
    valid = (np.abs(rel) <= ATTN_SIDE).reshape(-1)
    bucket = _t5_bucket(rel * dilation).reshape(-1).astype(np.int32)
    onehot = (jnp.asarray(bucket)[None, :] == jnp.arange(NUM_BUCKETS, dtype=jnp.int32)[:, None])
    vals = jnp.dot(bias_tab.T, onehot.astype(F32), precision=lax.Precision.HIGHEST)
    vals = jnp.where(jnp.asarray(valid)[None, :], vals, NEG_INF)
    return vals.reshape(HEADS_PER_GROUP, 3, ATTN_TQ, ATTN_TK)


def _attn_kernel(q_ref, k_ref, v_ref, bias_ref, o_ref, lse_ref, *, seq, n_sub, n_pairs, blocks):
    nb = seq // ATTN_TQ
    lane = lax.broadcasted_iota(jnp.int32, (ATTN_TQ, LANES), 1)
    first = lane < HEAD_DIM
    nt = (((1,), (1,)), ((), ()))

    def both_heads(col):
        return jnp.where(first, jnp.broadcast_to(col[:ATTN_TQ], (ATTN_TQ, LANES)),
                         jnp.broadcast_to(col[ATTN_TQ:], (ATTN_TQ, LANES)))

    def run(subs, i0):
        units = []
        for sub in subs:
            for bo in range(blocks):
                i = i0 + bo
                q0 = pl.multiple_of(i * ATTN_TQ, ATTN_TQ)
                ks = pl.multiple_of(jnp.clip(i * ATTN_TQ - ATTN_SIDE, 0, seq - ATTN_TK), ATTN_SIDE)
                var = jnp.where(i == 0, 0, jnp.where(i == nb - 1, 2, 1))
                for pr in range(n_pairs):
                    units.append(dict(sub=sub, q0=q0, ks=ks, var=var, pr=pr,
                                      cols=slice(pr * LANES, (pr + 1) * LANES)))

        def scores(u):
            q2 = q_ref[u["sub"], pl.ds(u["q0"], ATTN_TQ), u["cols"]]
            k2 = k_ref[u["sub"], pl.ds(u["ks"], ATTN_TK), u["cols"]]
            zero = jnp.zeros_like(q2)
            qq = jnp.concatenate([jnp.where(first, q2, zero), jnp.where(first, zero, q2)], axis=0)
            u["s"] = lax.dot_general(qq, k2, nt, preferred_element_type=F32)

        def row_max(u):
            pr, var = u["pr"], u["var"]
            u["s"] = u["s"] + jnp.concatenate([bias_ref[2 * pr, var], bias_ref[2 * pr + 1, var]],
                                              axis=0)
            u["m"] = jnp.max(u["s"], axis=-1, keepdims=True)

        def probs(u):
            p = jnp.exp2(u.pop("s") - u["m"])
            u["den"] = jnp.sum(p, axis=-1, keepdims=True)
            u["p"] = p.astype(BF16)

        def values(u):
            v2 = v_ref[u["sub"], pl.ds(u["ks"], ATTN_TK), u["cols"]]
            u["pv"] = jnp.dot(u.pop("p"), v2, preferred_element_type=F32)

        def finish(u):
            den = both_heads(u.pop("den"))
            pv = u.pop("pv")
            rows = pl.ds(u["q0"], ATTN_TQ)
            o_ref[u["sub"], rows, u["cols"]] = (jnp.where(first, pv[:ATTN_TQ], pv[ATTN_TQ:])
                                                * (1.0 / den)).astype(BF16)
            lse_ref[u["sub"], rows, u["cols"]] = (both_heads(u.pop("m"))
                                                  + jnp.log(den) * (1.0 / np.log(2.0)))

        _emit_skewed(units, (scores, row_max, probs, values, finish))

    if nb == blocks:
        run(range(n_sub), 0)
    else:
        def body(j, carry):
            run(range(n_sub), j * blocks)
            return carry

        lax.fori_loop(0, nb // blocks, body, 0)


def _attention(qkv, bias, name):
    n_seq, seq, _ = qkv.shape
    assert seq >= 2 * ATTN_TQ and seq % ATTN_TQ == 0
    if seq >= 4096:
        n_sub, n_pairs = 1, 2
    else:
        n_pairs = ATTN_WIDTH // LANES
        n_sub = max(1, min(n_seq, 1024 // seq))
    blocks = max(2, min(seq // ATTN_TQ, 16 // (n_sub * n_pairs)))
    assert (seq // ATTN_TQ) % blocks == 0
    cw = n_pairs * LANES
    pb = ATTN_WIDTH // cw
    kern = functools.partial(_attn_kernel, seq=seq, n_sub=n_sub, n_pairs=n_pairs, blocks=blocks)
    blk = (n_sub, seq, cw)
    return pl.pallas_call(
        kern,
        out_shape=(jax.ShapeDtypeStruct((n_seq, seq, ATTN_WIDTH), BF16),
                   jax.ShapeDtypeStruct((n_seq, seq, ATTN_WIDTH), F32)),
        grid=(n_seq // n_sub, pb),
        in_specs=[pl.BlockSpec(blk, lambda r, p: (r, 0, p)),
                  pl.BlockSpec(blk, lambda r, p: (r, 0, pb + p)),
                  pl.BlockSpec(blk, lambda r, p: (r, 0, 2 * pb + p)),
                  pl.BlockSpec((2 * n_pairs, 3, ATTN_TQ, ATTN_TK), lambda r, p: (p, 0, 0, 0))],
        out_specs=(pl.BlockSpec(blk, lambda r, p: (r, 0, p)),
                   pl.BlockSpec(blk, lambda r, p: (r, 0, p))),
        compiler_params=_cparams(("parallel", "parallel")),
        name=name,
    )(qkv, qkv, qkv, bias)


def _hgrn_mats():
    t = np.arange(HG_TILE)[:, None]
    u = np.arange(HG_TILE)[None, :]
    same = (t // HG_CHUNK) == (u // HG_CHUNK)
    return (jnp.asarray(same & (u <= t), dtype=BF16), jnp.asarray(same & (u >= t), dtype=BF16))


def _hgrn_kernel(xn_ref, wq_ref, wzf_ref, wzb_ref, wi_ref, wg_ref, wx_ref, lbf_ref, lbb_ref,
                 nw_ref, mf_ref, mb_ref, o_ref, x_out_ref, wcat_ref, gate_ref, acc_ref, qd_ref,
                 kv_ref, dec_ref, st_ref, *, seq):
    n_tiles = seq // HG_TILE
    n_chunks = seq // HG_CHUNK
    cpt = HG_TILE // HG_CHUNK
    mid = HG_CHUNK // 2
    nt = (((1,), (1,)), ((), ()))
    w_refs = (wq_ref, wzf_ref, wzb_ref, wi_ref, wg_ref, wx_ref)

    def per_chunk_rows(rows):
        return jnp.concatenate([jnp.broadcast_to(r, (HG_CHUNK, HG_DK)) for r in rows], axis=0)

    scans = ((lbf_ref, mf_ref, False), (lbb_ref, mb_ref, True))

    for k, w_ref in enumerate(w_refs):
        wcat_ref[:, k * HG_DK:(k + 1) * HG_DK] = w_ref[...]

    def a_project(u):
        z = jnp.dot(xn_ref[u["rows"], :], wcat_ref[...], preferred_element_type=F32)
        u["q"] = z[:, 0:HG_DK]
        u["z"] = [z[:, HG_DK:2 * HG_DK], z[:, 2 * HG_DK:3 * HG_DK]]
        u["v"] = z[:, 3 * HG_DK:4 * HG_DK].astype(BF16)
        gate_ref[u["rows"], :] = z[:, 4 * HG_DK:5 * HG_DK]
        x_out_ref[u["rows"], :] = jax.nn.sigmoid(z[:, 5 * HG_DK:]).astype(BF16)

    def a_gate(u):
        u["kk"], u["ghl"] = [], []
        for (lb_ref, _, _), z in zip(scans, u.pop("z")):
            kk = (1.0 - lb_ref[...]) / (1.0 + jnp.exp(z))
            g = jnp.log(1.0 - kk) * (1.0 / np.log(2.0))
            g_hi = g.astype(BF16)
            g_lo = (g - g_hi.astype(F32)).astype(BF16)
            u["kk"].append(kk)
            u["ghl"].append(jnp.concatenate([g_hi, g_lo], axis=1))

    def a_cumsum(u):
        u["sums"] = [jnp.dot(m_ref[...], ghl, preferred_element_type=F32)
                     for (_, m_ref, _), ghl in zip(scans, u.pop("ghl"))]

    def a_decay(u):
        q = u.pop("q")
        q_c, k_c, q_d, k_s, decs = [], [], [], [], []
        for (_, _, reverse), sums, kk in zip(scans, u.pop("sums"), u.pop("kk")):
            beta = sums[:, :HG_DK] + sums[:, HG_DK:]
            tot_row = 0 if reverse else HG_CHUNK - 1
            cen_row = mid if reverse else mid - 1
            tot = [beta[c * HG_CHUNK + tot_row:c * HG_CHUNK + tot_row + 1] for c in range(cpt)]
            cen = [beta[c * HG_CHUNK + cen_row:c * HG_CHUNK + cen_row + 1] for c in range(cpt)]
            d = beta - per_chunk_rows(cen)
            e_q = jnp.exp2(d)
            qc = (q * e_q).astype(BF16)
            kc = (kk * (1.0 / e_q)).astype(BF16)
            q_d.append(qc * per_chunk_rows([jnp.exp2(c_).astype(BF16) for c_ in cen]))
            k_s.append(kc * per_chunk_rows([jnp.exp2(t_ - c_).astype(BF16)
                                            for t_, c_ in zip(tot, cen)]))
            q_c.append(qc)
            k_c.append(kc)
            decs.append([jnp.exp2(t_) for t_ in tot])
        qd_ref[u["rows"], :] = jnp.concatenate(q_d, axis=1)
        u["ks"] = jnp.concatenate(k_s, axis=1)
        u["qc"], u["kc"] = q_c, k_c
        for c in range(cpt):
            dec_ref[u["t"] * cpt + c] = jnp.concatenate(
                [jnp.broadcast_to(decs[0][c], (SUBLANES, HG_DK)),
                 jnp.broadcast_to(decs[1][c], (SUBLANES, HG_DK))],
                axis=1)

    chunk_rows = [slice(c * HG_CHUNK, (c + 1) * HG_CHUNK) for c in range(cpt)]
    ti = lax.broadcasted_iota(jnp.int32, (HG_CHUNK, HG_CHUNK), 0)
    ui = lax.broadcasted_iota(jnp.int32, (HG_CHUNK, HG_CHUNK), 1)
    past, future = ui <= ti, ui >= ti

    def a_scores(u):
        u["a"] = [[lax.dot_general(qc[cr], kc[cr], nt, preferred_element_type=F32)
                   for qc, kc in zip(u["qc"], u["kc"])] for cr in chunk_rows]
        del u["qc"], u["kc"]

    def a_mask(u):
        u["a"] = [(jnp.where(past, a_f, 0.0) + jnp.where(future, a_b, 0.0)).astype(BF16)
                  for a_f, a_b in u.pop("a")]

    def a_values(u):
        v = u.pop("v")
        acc_ref[u["rows"], :] = jnp.concatenate(
            [jnp.dot(a, v[cr], preferred_element_type=F32) for a, cr in zip(u.pop("a"), chunk_rows)],
            axis=0)
        ks = u.pop("ks")
        for c, cr in enumerate(chunk_rows):
            kv_ref[u["t"] * cpt + c] = lax.dot_general(v[cr], ks[cr], (((0,), (0,)), ((), ())),
                                                       preferred_element_type=F32)

    a_stages = (a_project, a_gate, a_cumsum, a_decay, a_scores, a_mask, a_values)

    def phase_a(j, carry):
        units = []
        for k in range(HG_TILES_PER_TRIP):
            t = j * HG_TILES_PER_TRIP + k
            units.append(dict(t=t, rows=pl.ds(pl.multiple_of(t * HG_TILE, HG_TILE), HG_TILE)))
        _emit_skewed(units, a_stages)
        return carry

    lax.fori_loop(0, n_tiles // HG_TILES_PER_TRIP, phase_a, 0)

    fw = slice(0, HG_DK)
    bw = slice(HG_DK, 2 * HG_DK)

    def phase_b(n, carry):
        sf, sb = carry
        m = n_chunks - 1 - n
        st_ref[n, :, fw] = sf.astype(BF16)
        sf = dec_ref[n][0:1, fw] * sf + kv_ref[n, :, fw]
        st_ref[m, :, bw] = sb.astype(BF16)
        sb = dec_ref[m][0:1, bw] * sb + kv_ref[m, :, bw]
        return sf, sb

    s0 = jnp.zeros((HG_DK, HG_DK), F32)
    lax.fori_loop(0, n_chunks, phase_b, (s0, s0), unroll=4)

    def c_inter(u):
        u["inter"] = [lax.dot_general(qd_ref[pl.ds(u["r0"] + c * HG_CHUNK, HG_CHUNK), :],
                                      st_ref[u["t"] * cpt + c], nt, preferred_element_type=F32)
                      for c in range(cpt)]

    def c_sum(u):
        u["o"] = acc_ref[u["rows"], :] + jnp.concatenate(u.pop("inter"), axis=0)
        u["ms"] = jnp.mean(u["o"] * u["o"], axis=-1, keepdims=True)

    def c_out(u):
        gate = gate_ref[u["rows"], :]
        o_ref[u["rows"], :] = (u.pop("o") * lax.rsqrt(u.pop("ms") + EPS) * nw_ref[...]
                               * (gate * jax.nn.sigmoid(gate))).astype(BF16)

    def phase_c(j, carry):
        units = []
        for k in range(HG_TILES_PER_TRIP):
            t = j * HG_TILES_PER_TRIP + k
            r0 = pl.multiple_of(t * HG_TILE, HG_TILE)
            units.append(dict(t=t, r0=r0, rows=pl.ds(r0, HG_TILE)))
        _emit_skewed(units, (c_inter, c_sum, c_out))
        return carry

    lax.fori_loop(0, n_tiles // HG_TILES_PER_TRIP, phase_c, 0)


def _hgrn(xn3, w, lb_f, lb_b, nw, col_q, col_zf, col_zb, col_i, col_g, col_x):
    b, seq, d = xn3.shape
    mf, mb = _hgrn_mats()
    n_chunks = seq // HG_CHUNK

    def wspec(col):
        return pl.BlockSpec((d, HG_DK), lambda bi, h, col=col: (0, col + h))

    vec = pl.BlockSpec((1, HG_DK), lambda bi, h: (0, h))
    const = pl.BlockSpec((HG_TILE, HG_TILE), lambda bi, h: (0, 0))
    return pl.pallas_call(
        functools.partial(_hgrn_kernel, seq=seq),
        out_shape=[jax.ShapeDtypeStruct((b, seq, HG_HEADS * HG_DK), BF16)] * 2,
        grid=(b, HG_HEADS),
        in_specs=[pl.BlockSpec((None, seq, d), lambda bi, h: (bi, 0, 0)),
                  wspec(col_q), wspec(col_zf), wspec(col_zb), wspec(col_i), wspec(col_g),
                  wspec(col_x),
                  vec, vec, pl.BlockSpec((1, HG_DK), lambda bi, h: (0, 0)), const, const],
        out_specs=[pl.BlockSpec((None, seq, HG_DK), lambda bi, h: (bi, 0, h))] * 2,
        scratch_shapes=[pltpu.VMEM((d, 6 * HG_DK), BF16),
                        pltpu.VMEM((seq, HG_DK), F32),
                        pltpu.VMEM((seq, HG_DK), F32),
                        pltpu.VMEM((seq, 2 * HG_DK), BF16),
                        pltpu.VMEM((n_chunks, HG_DK, 2 * HG_DK), F32),
                        pltpu.VMEM((n_chunks, SUBLANES, 2 * HG_DK), F32),
                        pltpu.VMEM((n_chunks, HG_DK, 2 * HG_DK), BF16)],
        compiler_params=_cparams(("parallel", "parallel")),
        name="hgrn2",
    )(xn3, w, w, w, w, w, w, lb_f, lb_b, nw, mf, mb)


def _final_kernel(o0_ref, l0_ref, o1_ref, l1_ref, o2_ref, l2_ref, ga_ref, ob_ref,
                  za0_ref, za1_ref, zb0_ref, zb1_ref,
                  x_ref, wa_ref, wb_ref, wo_ref, out_ref, so1_ref, sl1_ref, so2_ref, sl2_ref,
                  *, tm, dilations):
    for src, dst, dil in ((o1_ref, so1_ref, dilations[0]), (l1_ref, sl1_ref, dilations[0]),
                          (o2_ref, so2_ref, dilations[1]), (l2_ref, sl2_ref, dilations[1])):
        for r in range(dil):
            val = src[r].astype(F32)
            for c in range(ATTN_WIDTH // LANES):
                dst[c, pl.ds(r, tm // dil, stride=dil), :] = val[:, c * LANES:(c + 1) * LANES]

    def natural(ref):
        return jnp.concatenate([ref[c] for c in range(ATTN_WIDTH // LANES)], axis=1)

    l0, l1, l2 = l0_ref[...], natural(sl1_ref), natural(sl2_ref)
    mx = jnp.maximum(jnp.maximum(l0, l1), l2)
    w0, w1, w2 = jnp.exp2(l0 - mx), jnp.exp2(l1 - mx), jnp.exp2(l2 - mx)
    num = w0 * o0_ref[...].astype(F32) + w1 * natural(so1_ref) + w2 * natural(so2_ref)
    ga = ga_ref[...].astype(F32)
    a = (num / (w0 + w1 + w2) * (ga * jax.nn.sigmoid(ga))).astype(BF16)

    y_a = jnp.dot(a, wa_ref[...], preferred_element_type=F32)
    y_b = jnp.dot(ob_ref[...], wb_ref[...], preferred_element_type=F32)
    gate_a = jnp.concatenate([za0_ref[...], za1_ref[...]], axis=1).astype(F32)
    zb = jnp.concatenate([zb0_ref[...], zb1_ref[...]], axis=1).astype(F32)
    merged = (gate_a * y_a + jax.nn.sigmoid(zb) * y_b).astype(BF16)
    out_ref[...] = x_ref[...] + jnp.dot(merged, wo_ref[...], preferred_element_type=F32)


def _final(o_g, lse_g, z_gate, z_ga, o_b, x3, wa, wb, wo, dilations, col_ga, col_zb):
    b, seq, d = x3.shape
    tm = 512
    aw = ATTN_WIDTH

    def rows(width, col=0):
        return pl.BlockSpec((None, tm, width), lambda bi, i, col=col: (bi, i, col))

    def perm(dil):
        return pl.BlockSpec((None, dil, tm // dil, aw), lambda bi, i: (bi, 0, i, 0))

    def full(shape):
        return pl.BlockSpec(shape, lambda bi, i: (0, 0))

    d1, d2 = dilations
    return pl.pallas_call(
        functools.partial(_final_kernel, tm=tm, dilations=dilations),
        out_shape=jax.ShapeDtypeStruct((b, seq, d), F32),
        grid=(b, seq // tm),
        in_specs=[rows(aw), rows(aw), perm(d1), perm(d1), perm(d2), perm(d2),
                  rows(aw, col_ga), rows(d), rows(aw, 0), rows(aw, 1),
                  rows(aw, col_zb), rows(aw, col_zb + 1), rows(d),
                  full(wa.shape), full(wb.shape), full(wo.shape)],
        out_specs=rows(d),
        scratch_shapes=[pltpu.VMEM((aw // LANES, tm, LANES), F32)] * 4,
        compiler_params=_cparams(("parallel", "parallel")),
        name="merge_out",
    )(o_g[0], lse_g[0], o_g[1], lse_g[1], o_g[2], lse_g[2], z_gate, o_b,
      z_ga, z_ga, z_gate, z_gate, x3, wa, wb, wo)


def kernel(x, norm_w, w_in, q_norm_w, k_norm_w, rel_bias, lb_fwd, lb_bwd, hg_norm_w,
           w_proj_a, w_proj_b, w_out):
    b, seq, d = x.shape
    n = b * seq
    layer = 0
    w = w_in[layer].astype(BF16)
    qkv_cols = 3 * len(ATTN_GROUPS) * ATTN_WIDTH
    dilations = tuple(dil for _, dil in ATTN_GROUPS)
    assert dilations[0] == 1

    xn_all = _norm(x.astype(F32), norm_w[layer].reshape(1, d).astype(F32), dilations[1:])
    xn = xn_all[0].reshape(n, d)

    hw = HG_HEADS * HG_DK
    c_ga = qkv_cols // ATTN_WIDTH
    c_gate_b = c_ga + 1 + 6 * hw // ATTN_WIDTH
    z_gate = _proj_gates(xn, w, c_ga, c_gate_b, hw // ATTN_WIDTH).reshape(b, seq, -1)

    o_g, lse_g = [], []
    for g, dilation in enumerate(dilations):
        sub_len = seq // dilation
        q_scale = HEAD_DIM ** -0.5 * LOG2E
        nw = jnp.stack([jnp.tile(q_norm_w[layer, g].astype(F32), HEADS_PER_GROUP) * q_scale,
                        jnp.tile(k_norm_w[layer, g].astype(F32), HEADS_PER_GROUP)]
                       ).reshape(2, 1, ATTN_WIDTH)
        qkv = _proj_qkv(xn_all[g].reshape(n, d), w, g, nw, f"proj_qkv_d{dilation}")
        heads = slice(g * HEADS_PER_GROUP, (g + 1) * HEADS_PER_GROUP)
        bias = _attn_bias(rel_bias.astype(F32)[:, heads] * LOG2E, dilation)
        o, lse = _attention(qkv.reshape(b * dilation, sub_len, 3 * ATTN_WIDTH), bias,
                            f"attn_d{dilation}")
        shape = (b, seq, ATTN_WIDTH) if dilation == 1 else (b, dilation, sub_len, ATTN_WIDTH)
        o_g.append(o.reshape(shape))
        lse_g.append(lse.reshape(shape))

    lb_f = jnp.cumsum(jax.nn.softmax(lb_fwd.astype(F32), axis=0), axis=0)[layer].reshape(1, hw)
    lb_b = jnp.cumsum(jax.nn.softmax(lb_bwd.astype(F32), axis=0), axis=0)[layer].reshape(1, hw)
    hb = hw // LANES
    c_qb = (qkv_cols + ATTN_WIDTH) // LANES
    o_b, z_ga = _hgrn(xn_all[0], w, lb_f, lb_b, hg_norm_w[layer].reshape(1, HG_DK).astype(F32),
                      col_q=c_qb, col_zf=c_qb + hb, col_zb=c_qb + 2 * hb, col_i=c_qb + 3 * hb,
                      col_g=c_qb + 4 * hb, col_x=c_qb + 5 * hb)

    out = _final(o_g, lse_g, z_gate, z_ga, o_b, x.astype(F32),
                 w_proj_a[layer].astype(BF16), w_proj_b[layer].astype(BF16),
                 w_out[layer].astype(BF16), dilations[1:], col_ga=0, col_zb=1)
    return out.astype(x.dtype)
```

```python
import functools

import numpy as np
import jax
import jax.numpy as jnp
from jax import lax
from jax.experimental import pallas as pl
from jax.experimental.pallas import tpu as pltpu

F32 = jnp.float32
BF16 = jnp.bfloat16

EPS = 1e-6
LOG2E = float(np.log2(np.e))
NEG_INF = -1e30
ATTN_GROUPS = ((128, 1), (512, 4), (2048, 16))
HEAD_DIM = 64
HEADS_PER_GROUP = 8
ATTN_WIDTH = HEADS_PER_GROUP * HEAD_DIM
NUM_BUCKETS = 32
REL_MAX_DISTANCE = 1024
HG_HEADS = 8
HG_DK = 128

LANES = 128
SUBLANES = 8
PROJ_SUB = 512
VMEM_LIMIT = 52 * 1024 * 1024

ATTN_TQ = 128
ATTN_SIDE = 64
ATTN_TK = ATTN_TQ + 2 * ATTN_SIDE

HG_CHUNK = 64
HG_TILE = 256
HG_TILES_PER_TRIP = 16


def _cparams(sem):
    return pltpu.CompilerParams(dimension_semantics=sem, vmem_limit_bytes=VMEM_LIMIT)


def _emit_skewed(units, stages):
    for step in range(len(units) + len(stages) - 1):
        for k, stage in enumerate(stages):
            if 0 <= step - k < len(units):
                stage(units[step - k])


def _norm_kernel(x_ref, w_ref, o_ref, *rest, tm, dilations):
    perm_refs, y_ref = rest[:-1], rest[-1]
    x = x_ref[...]
    ms = jnp.mean(x * x, axis=-1, keepdims=True)
    y = x * lax.rsqrt(ms + EPS) * w_ref[...]
    o_ref[...] = y.astype(BF16)
    n_lane_blocks = y.shape[1] // LANES
    for c in range(n_lane_blocks):
        y_ref[c] = y[:, c * LANES:(c + 1) * LANES]
    for p_ref, dil in zip(perm_refs, dilations):
        for r in range(dil):
            for c in range(n_lane_blocks):
                p_ref[r, :, c * LANES:(c + 1) * LANES] = (
                    y_ref[c, pl.ds(r, tm // dil, stride=dil), :].astype(BF16))


def _norm(x3, w, dilations):
    b, seq, d = x3.shape
    tm = 1024
    out_shape = [jax.ShapeDtypeStruct((b, seq, d), BF16)]
    out_specs = [pl.BlockSpec((None, tm, d), lambda bi, i: (bi, i, 0))]
    for dil in dilations:
        out_shape.append(jax.ShapeDtypeStruct((b, dil, seq // dil, d), BF16))
        out_specs.append(pl.BlockSpec((None, dil, tm // dil, d), lambda bi, i: (bi, 0, i, 0)))
    return pl.pallas_call(
        functools.partial(_norm_kernel, tm=tm, dilations=dilations),
        out_shape=out_shape,
        grid=(b, seq // tm),
        in_specs=[pl.BlockSpec((None, tm, d), lambda bi, i: (bi, i, 0)),
                  pl.BlockSpec((1, d), lambda bi, i: (0, 0))],
        out_specs=out_specs,
        scratch_shapes=[pltpu.VMEM((d // LANES, tm, LANES), F32)],
        compiler_params=_cparams(("parallel", "parallel")),
        name="rmsnorm",
    )(x3, w)


def _proj_kernel(x_ref, w_ref, o_ref):
    o_ref[...] = jnp.dot(x_ref[...], w_ref[...], preferred_element_type=F32).astype(o_ref.dtype)


def _proj_gates(xn, w, col_a, col_b, n_b):
    n, d = xn.shape
    tm, tn = 4096, ATTN_WIDTH
    return pl.pallas_call(
        _proj_kernel,
        out_shape=jax.ShapeDtypeStruct((n, (1 + n_b) * tn), BF16),
        grid=(n // tm, 1 + n_b),
        in_specs=[pl.BlockSpec((tm, d), lambda i, j: (i, 0)),
                  pl.BlockSpec((d, tn), lambda i, j: (0, jnp.where(j == 0, col_a, col_b - 1 + j)))],
        out_specs=pl.BlockSpec((tm, tn), lambda i, j: (i, j)),
        compiler_params=_cparams(("parallel", "arbitrary")),
        name="proj_gates",
    )(xn, w)


def _proj_qkv_kernel(x_ref, w_ref, nw_ref, seg_ref, o_ref):
    j = pl.program_id(1)

    @pl.when(j < 2)
    def _():
        units = [dict(rows=slice(r * PROJ_SUB, (r + 1) * PROJ_SUB))
                 for r in range(x_ref.shape[0] // PROJ_SUB)]

        def project(u):
            u["z"] = jnp.dot(x_ref[u["rows"], :], w_ref[...], preferred_element_type=F32)

        def square(u):
            u["zz"] = (u["z"] * u["z"]).astype(BF16)

        def mean_square(u):
            u["ms"] = jnp.dot(u.pop("zz"), seg_ref[...], preferred_element_type=F32)

        def normalise(u):
            o_ref[u["rows"], :] = (u.pop("z") * lax.rsqrt(u.pop("ms") + EPS) * nw_ref[...]
                                   ).astype(BF16)

        _emit_skewed(units, (project, square, mean_square, normalise))

    @pl.when(j == 2)
    def _():
        o_ref[...] = jnp.dot(x_ref[...], w_ref[...], preferred_element_type=F32).astype(BF16)


def _proj_qkv(xn2, w, group, nw, name):
    n, d = xn2.shape
    tm, tn = 4096, ATTN_WIDTH
    head = np.arange(tn) // HEAD_DIM
    seg = jnp.asarray((head[:, None] == head[None, :]) / HEAD_DIM, dtype=BF16)
    return pl.pallas_call(
        _proj_qkv_kernel,
        out_shape=jax.ShapeDtypeStruct((n, 3 * tn), BF16),
        grid=(n // tm, 3),
        in_specs=[pl.BlockSpec((tm, d), lambda i, j: (i, 0)),
                  pl.BlockSpec((d, tn), lambda i, j: (0, 3 * group + j)),
                  pl.BlockSpec((None, 1, tn), lambda i, j: (jnp.minimum(j, 1), 0, 0)),
                  pl.BlockSpec((tn, tn), lambda i, j: (0, 0))],
        out_specs=pl.BlockSpec((tm, tn), lambda i, j: (i, j)),
        compiler_params=_cparams(("parallel", "arbitrary")),
        name=name,
    )(xn2, w, nw, seg)


def _t5_bucket(rel):
    half = NUM_BUCKETS // 2
    max_exact = half // 2
    n = np.abs(rel)
    large = max_exact + (np.log(np.maximum(n, 1) / max_exact)
                         / np.log(REL_MAX_DISTANCE / max_exact) * (half - max_exact)).astype(np.int32)
    large = np.minimum(large, half - 1)
    return np.where(rel > 0, half, 0) + np.where(n < max_exact, n, large)


def _attn_bias(bias_tab, dilation):
    t = np.arange(ATTN_TQ)[:, None]
    j = np.arange(ATTN_TK)[None, :]
    rel = np.stack([j - off - t for off in (0, ATTN_SIDE, 2 * ATTN_SIDE)])
    valid = (np.abs(rel) <= ATTN_SIDE).reshape(-1)
    bucket = _t5_bucket(rel * dilation).reshape(-1).astype(np.int32)
    onehot = (jnp.asarray(bucket)[None, :] == jnp.arange(NUM_BUCKETS, dtype=jnp.int32)[:, None])
    vals = jnp.dot(bias_tab.T, onehot.astype(F32), precision=lax.Precision.HIGHEST)
    vals = jnp.where(jnp.asarray(valid)[None, :], vals, NEG_INF)
    return vals.reshape(HEADS_PER_GROUP, 3, ATTN_TQ, ATTN_TK)


def _attn_kernel(q_ref, k_ref, v_ref, bias_ref, o_ref, lse_ref, *, seq, n_sub, n_pairs, blocks):
    nb = seq // ATTN_TQ
    lane = lax.broadcasted_iota(jnp.int32, (ATTN_TQ, LANES), 1)
    first = lane < HEAD_DIM
    nt = (((1,), (1,)), ((), ()))

    def both_heads(col):
        return jnp.where(first, jnp.broadcast_to(col[:ATTN_TQ], (ATTN_TQ, LANES)),
                         jnp.broadcast_to(col[ATTN_TQ:], (ATTN_TQ, LANES)))

    def run(subs, i0):
        units = []
        for sub in subs:
            for bo in range(blocks):
                i = i0 + bo
                q0 = pl.multiple_of(i * ATTN_TQ, ATTN_TQ)
                ks = pl.multiple_of(jnp.clip(i * ATTN_TQ - ATTN_SIDE, 0, seq - ATTN_TK), ATTN_SIDE)
                var = jnp.where(i == 0, 0, jnp.where(i == nb - 1, 2, 1))
                for pr in range(n_pairs):
                    units.append(dict(sub=sub, q0=q0, ks=ks, var=var, pr=pr,
                                      cols=slice(pr * LANES, (pr + 1) * LANES)))

        def scores(u):
            q2 = q_ref[u["sub"], pl.ds(u["q0"], ATTN_TQ), u["cols"]]
            k2 = k_ref[u["sub"], pl.ds(u["ks"], ATTN_TK), u["cols"]]
            zero = jnp.zeros_like(q2)
            qq = jnp.concatenate([jnp.where(first, q2, zero), jnp.where(first, zero, q2)], axis=0)
            u["s"] = lax.dot_general(qq, k2, nt, preferred_element_type=F32)

        def row_max(u):
            pr, var = u["pr"], u["var"]
            u["s"] = u["s"] + jnp.concatenate([bias_ref[2 * pr, var], bias_ref[2 * pr + 1, var]],
                                              axis=0)
            u["m"] = jnp.max(u["s"], axis=-1, keepdims=True)

        def probs(u):
            p = jnp.exp2(u.pop("s") - u["m"])
            u["den"] = jnp.sum(p, axis=-1, keepdims=True)
            u["p"] = p.astype(BF16)

        def values(u):
            v2 = v_ref[u["sub"], pl.ds(u["ks"], ATTN_TK), u["cols"]]
            u["pv"] = jnp.dot(u.pop("p"), v2, preferred_element_type=F32)

        def finish(u):
            den = both_heads(u.pop("den"))
            pv = u.pop("pv")
            rows = pl.ds(u["q0"], ATTN_TQ)
            o_ref[u["sub"], rows, u["cols"]] = (jnp.where(first, pv[:ATTN_TQ], pv[ATTN_TQ:])
                                                * (1.0 / den)).astype(BF16)
            lse_ref[u["sub"], rows, u["cols"]] = (both_heads(u.pop("m"))
                                                  + jnp.log(den) * (1.0 / np.log(2.0)))

        _emit_skewed(units, (scores, row_max, probs, values, finish))

    if nb == blocks:
        run(range(n_sub), 0)
    else:
        def body(j, carry):
            run(range(n_sub), j * blocks)
            return carry

        lax.fori_loop(0, nb // blocks, body, 0)


def _attention(qkv, bias, name):
    n_seq, seq, _ = qkv.shape
    assert seq >= 2 * ATTN_TQ and seq % ATTN_TQ == 0
    if seq >= 4096:
        n_sub, n_pairs = 1, 2
    else:
        n_pairs = ATTN_WIDTH // LANES
        n_sub = max(1, min(n_seq, 1024 // seq))
    blocks = max(2, min(seq // ATTN_TQ, 32 // (n_sub * n_pairs)))
    assert (seq // ATTN_TQ) % blocks == 0
    cw = n_pairs * LANES
    pb = ATTN_WIDTH // cw
    kern = functools.partial(_attn_kernel, seq=seq, n_sub=n_sub, n_pairs=n_pairs, blocks=blocks)
    blk = (n_sub, seq, cw)
    return pl.pallas_call(
        kern,
        out_shape=(jax.ShapeDtypeStruct((n_seq, seq, ATTN_WIDTH), BF16),
                   jax.ShapeDtypeStruct((n_seq, seq, ATTN_WIDTH), F32)),
        grid=(n_seq // n_sub, pb),
        in_specs=[pl.BlockSpec(blk, lambda r, p: (r, 0, p)),
                  pl.BlockSpec(blk, lambda r, p: (r, 0, pb + p)),
                  pl.BlockSpec(blk, lambda r, p: (r, 0, 2 * pb + p)),
                  pl.BlockSpec((2 * n_pairs, 3, ATTN_TQ, ATTN_TK), lambda r, p: (p, 0, 0, 0))],
        out_specs=(pl.BlockSpec(blk, lambda r, p: (r, 0, p)),
                   pl.BlockSpec(blk, lambda r, p: (r, 0, p))),
        compiler_params=_cparams(("parallel", "parallel")),
        name=name,
    )(qkv, qkv, qkv, bias)


def _hgrn_mats():
    t = np.arange(HG_TILE)[:, None]
    u = np.arange(HG_TILE)[None, :]
    same = (t // HG_CHUNK) == (u // HG_CHUNK)
    return (jnp.asarray(same & (u <= t), dtype=BF16), jnp.asarray(same & (u >= t), dtype=BF16))


def _hgrn_kernel(xn_ref, wq_ref, wzf_ref, wzb_ref, wi_ref, wg_ref, wx_ref, lbf_ref, lbb_ref,
                 nw_ref, mf_ref, mb_ref, o_ref, x_out_ref, wcat_ref, gate_ref, acc_ref, qd_ref,
                 kv_ref, dec_ref, st_ref, *, seq):
    n_tiles = seq // HG_TILE
    n_chunks = seq // HG_CHUNK
    cpt = HG_TILE // HG_CHUNK
    mid = HG_CHUNK // 2
    nt = (((1,), (1,)), ((), ()))
    w_refs = (wq_ref, wzf_ref, wzb_ref, wi_ref, wg_ref, wx_ref)

    def per_chunk_rows(rows):
        return jnp.concatenate([jnp.broadcast_to(r, (HG_CHUNK, HG_DK)) for r in rows], axis=0)

    scans = ((lbf_ref, mf_ref, False), (lbb_ref, mb_ref, True))

    for k, w_ref in enumerate(w_refs):
        wcat_ref[:, k * HG_DK:(k + 1) * HG_DK] = w_ref[...]

    def a_project(u):
        z = jnp.dot(xn_ref[u["rows"], :], wcat_ref[...], preferred_element_type=F32)
        u["q"] = z[:, 0:HG_DK]
        u["z"] = [z[:, HG_DK:2 * HG_DK], z[:, 2 * HG_DK:3 * HG_DK]]
        u["v"] = z[:, 3 * HG_DK:4 * HG_DK].astype(BF16)
        gate_ref[u["rows"], :] = z[:, 4 * HG_DK:5 * HG_DK]
        x_out_ref[u["rows"], :] = jax.nn.sigmoid(z[:, 5 * HG_DK:]).astype(BF16)

    def a_gate(u):
        u["kk"], u["ghl"] = [], []
        for (lb_ref, _, _), z in zip(scans, u.pop("z")):
            kk = (1.0 - lb_ref[...]) / (1.0 + jnp.exp(z))
            g = jnp.log(1.0 - kk) * (1.0 / np.log(2.0))
            g_hi = g.astype(BF16)
            g_lo = (g - g_hi.astype(F32)).astype(BF16)
            u["kk"].append(kk)
            u["ghl"].append(jnp.concatenate([g_hi, g_lo], axis=1))

    def a_cumsum(u):
        u["sums"] = [jnp.dot(m_ref[...], ghl, preferred_element_type=F32)
                     for (_, m_ref, _), ghl in zip(scans, u.pop("ghl"))]

    def a_decay(u):
        q = u.pop("q")
        q_c, k_c, q_d, k_s, decs = [], [], [], [], []
        for (_, _, reverse), sums, kk in zip(scans, u.pop("sums"), u.pop("kk")):
            beta = sums[:, :HG_DK] + sums[:, HG_DK:]
            tot_row = 0 if reverse else HG_CHUNK - 1
            cen_row = mid if reverse else mid - 1
            tot = [beta[c * HG_CHUNK + tot_row:c * HG_CHUNK + tot_row + 1] for c in range(cpt)]
            cen = [beta[c * HG_CHUNK + cen_row:c * HG_CHUNK + cen_row + 1] for c in range(cpt)]
            d = beta - per_chunk_rows(cen)
            e_q = jnp.exp2(d)
            qc = (q * e_q).astype(BF16)
            kc = (kk * (1.0 / e_q)).astype(BF16)
            q_d.append(qc * per_chunk_rows([jnp.exp2(c_).astype(BF16) for c_ in cen]))
            k_s.append(kc * per_chunk_rows([jnp.exp2(t_ - c_).astype(BF16)
                                            for t_, c_ in zip(tot, cen)]))
            q_c.append(qc)
            k_c.append(kc)
            decs.append([jnp.exp2(t_) for t_ in tot])
        qd_ref[u["rows"], :] = jnp.concatenate(q_d, axis=1)
        u["ks"] = jnp.concatenate(k_s, axis=1)
        u["qc"], u["kc"] = q_c, k_c
        for c in range(cpt):
            dec_ref[u["t"] * cpt + c] = jnp.concatenate(
                [jnp.broadcast_to(decs[0][c], (SUBLANES, HG_DK)),
                 jnp.broadcast_to(decs[1][c], (SUBLANES, HG_DK))],
                axis=1)

    chunk_rows = [slice(c * HG_CHUNK, (c + 1) * HG_CHUNK) for c in range(cpt)]
    ti = lax.broadcasted_iota(jnp.int32, (HG_CHUNK, HG_CHUNK), 0)
    ui = lax.broadcasted_iota(jnp.int32, (HG_CHUNK, HG_CHUNK), 1)
    past, future = ui <= ti, ui >= ti

    def a_scores(u):
        u["a"] = [[lax.dot_general(qc[cr], kc[cr], nt, preferred_element_type=F32)
                   for qc, kc in zip(u["qc"], u["kc"])] for cr in chunk_rows]
        del u["qc"], u["kc"]

    def a_mask(u):
        u["a"] = [(jnp.where(past, a_f, 0.0) + jnp.where(future, a_b, 0.0)).astype(BF16)
                  for a_f, a_b in u.pop("a")]

    def a_values(u):
        v = u.pop("v")
        acc_ref[u["rows"], :] = jnp.concatenate(
            [jnp.dot(a, v[cr], preferred_element_type=F32) for a, cr in zip(u.pop("a"), chunk_rows)],
            axis=0)
        ks = u.pop("ks")
        for c, cr in enumerate(chunk_rows):
            kv_ref[u["t"] * cpt + c] = lax.dot_general(v[cr], ks[cr], (((0,), (0,)), ((), ())),
                                                       preferred_element_type=F32)

    a_stages = (a_project, a_gate, a_cumsum, a_decay, a_scores, a_mask, a_values)

    def phase_a(j, carry):
        units = []
        for k in range(HG_TILES_PER_TRIP):
            t = j * HG_TILES_PER_TRIP + k
            units.append(dict(t=t, rows=pl.ds(pl.multiple_of(t * HG_TILE, HG_TILE), HG_TILE)))
        _emit_skewed(units, a_stages)
        return carry

    lax.fori_loop(0, n_tiles // HG_TILES_PER_TRIP, phase_a, 0)

    fw = slice(0, HG_DK)
    bw = slice(HG_DK, 2 * HG_DK)

    def phase_b(n, carry):
        sf, sb = carry
        m = n_chunks - 1 - n
        st_ref[n, :, fw] = sf.astype(BF16)
        sf = dec_ref[n][0:1, fw] * sf + kv_ref[n, :, fw]
        st_ref[m, :, bw] = sb.astype(BF16)
        sb = dec_ref[m][0:1, bw] * sb + kv_ref[m, :, bw]
        return sf, sb

    s0 = jnp.zeros((HG_DK, HG_DK), F32)
    lax.fori_loop(0, n_chunks, phase_b, (s0, s0), unroll=4)

    def c_inter(u):
        u["inter"] = [lax.dot_general(qd_ref[pl.ds(u["r0"] + c * HG_CHUNK, HG_CHUNK), :],
                                      st_ref[u["t"] * cpt + c], nt, preferred_element_type=F32)
                      for c in range(cpt)]

    def c_sum(u):
        u["o"] = acc_ref[u["rows"], :] + jnp.concatenate(u.pop("inter"), axis=0)
        u["ms"] = jnp.mean(u["o"] * u["o"], axis=-1, keepdims=True)

    def c_out(u):
        gate = gate_ref[u["rows"], :]
        o_ref[u["rows"], :] = (u.pop("o") * lax.rsqrt(u.pop("ms") + EPS) * nw_ref[...]
                               * (gate * jax.nn.sigmoid(gate))).astype(BF16)

    def phase_c(j, carry):
        units = []
        for k in range(HG_TILES_PER_TRIP):
            t = j * HG_TILES_PER_TRIP + k
            r0 = pl.multiple_of(t * HG_TILE, HG_TILE)
            units.append(dict(t=t, r0=r0, rows=pl.ds(r0, HG_TILE)))
        _emit_skewed(units, (c_inter, c_sum, c_out))
        return carry

    lax.fori_loop(0, n_tiles // HG_TILES_PER_TRIP, phase_c, 0)


def _hgrn(xn3, w, lb_f, lb_b, nw, col_q, col_zf, col_zb, col_i, col_g, col_x):
    b, seq, d = xn3.shape
    mf, mb = _hgrn_mats()
    n_chunks = seq // HG_CHUNK

    def wspec(col):
        return pl.BlockSpec((d, HG_DK), lambda bi, h, col=col: (0, col + h))

    vec = pl.BlockSpec((1, HG_DK), lambda bi, h: (0, h))
    const = pl.BlockSpec((HG_TILE, HG_TILE), lambda bi, h: (0, 0))
    return pl.pallas_call(
        functools.partial(_hgrn_kernel, seq=seq),
        out_shape=[jax.ShapeDtypeStruct((b, seq, HG_HEADS * HG_DK), BF16)] * 2,
        grid=(b, HG_HEADS),
        in_specs=[pl.BlockSpec((None, seq, d), lambda bi, h: (bi, 0, 0)),
                  wspec(col_q), wspec(col_zf), wspec(col_zb), wspec(col_i), wspec(col_g),
                  wspec(col_x),
                  vec, vec, pl.BlockSpec((1, HG_DK), lambda bi, h: (0, 0)), const, const],
        out_specs=[pl.BlockSpec((None, seq, HG_DK), lambda bi, h: (bi, 0, h))] * 2,
        scratch_shapes=[pltpu.VMEM((d, 6 * HG_DK), BF16),
                        pltpu.VMEM((seq, HG_DK), F32),
                        pltpu.VMEM((seq, HG_DK), F32),
                        pltpu.VMEM((seq, 2 * HG_DK), BF16),
                        pltpu.VMEM((n_chunks, HG_DK, 2 * HG_DK), F32),
                        pltpu.VMEM((n_chunks, SUBLANES, 2 * HG_DK), F32),
                        pltpu.VMEM((n_chunks, HG_DK, 2 * HG_DK), BF16)],
        compiler_params=_cparams(("parallel", "parallel")),
        name="hgrn2",
    )(xn3, w, w, w, w, w, w, lb_f, lb_b, nw, mf, mb)


def _final_kernel(o0_ref, l0_ref, o1_ref, l1_ref, o2_ref, l2_ref, ga_ref, ob_ref,
                  za0_ref, za1_ref, zb0_ref, zb1_ref,
                  x_ref, wa_ref, wb_ref, wo_ref, out_ref, so1_ref, sl1_ref, so2_ref, sl2_ref,
                  *, tm, dilations):
    for src, dst, dil in ((o1_ref, so1_ref, dilations[0]), (l1_ref, sl1_ref, dilations[0]),
                          (o2_ref, so2_ref, dilations[1]), (l2_ref, sl2_ref, dilations[1])):
        for r in range(dil):
            val = src[r].astype(F32)
            for c in range(ATTN_WIDTH // LANES):
                dst[c, pl.ds(r, tm // dil, stride=dil), :] = val[:, c * LANES:(c + 1) * LANES]

    def natural(ref):
        return jnp.concatenate([ref[c] for c in range(ATTN_WIDTH // LANES)], axis=1)

    l0, l1, l2 = l0_ref[...], natural(sl1_ref), natural(sl2_ref)
    mx = jnp.maximum(jnp.maximum(l0, l1), l2)
    w0, w1, w2 = jnp.exp2(l0 - mx), jnp.exp2(l1 - mx), jnp.exp2(l2 - mx)
    num = w0 * o0_ref[...].astype(F32) + w1 * natural(so1_ref) + w2 * natural(so2_ref)
    ga = ga_ref[...].astype(F32)
    a = (num / (w0 + w1 + w2) * (ga * jax.nn.sigmoid(ga))).astype(BF16)

    y_a = jnp.dot(a, wa_ref[...], preferred_element_type=F32)
    y_b = jnp.dot(ob_ref[...], wb_ref[...], preferred_element_type=F32)
    gate_a = jnp.concatenate([za0_ref[...], za1_ref[...]], axis=1).astype(F32)
    zb = jnp.concatenate([zb0_ref[...], zb1_ref[...]], axis=1).astype(F32)
    merged = (gate_a * y_a + jax.nn.sigmoid(zb) * y_b).astype(BF16)
    out_ref[...] = x_ref[...] + jnp.dot(merged, wo_ref[...], preferred_element_type=F32)


def _final(o_g, lse_g, z_gate, z_ga, o_b, x3, wa, wb, wo, dilations, col_ga, col_zb):
    b, seq, d = x3.shape
    tm = 512
    aw = ATTN_WIDTH

    def rows(width, col=0):
        return pl.BlockSpec((None, tm, width), lambda bi, i, col=col: (bi, i, col))

    def perm(dil):
        return pl.BlockSpec((None, dil, tm // dil, aw), lambda bi, i: (bi, 0, i, 0))

    def full(shape):
        return pl.BlockSpec(shape, lambda bi, i: (0, 0))

    d1, d2 = dilations
    return pl.pallas_call(
        functools.partial(_final_kernel, tm=tm, dilations=dilations),
        out_shape=jax.ShapeDtypeStruct((b, seq, d), F32),
        grid=(b, seq // tm),
        in_specs=[rows(aw), rows(aw), perm(d1), perm(d1), perm(d2), perm(d2),
                  rows(aw, col_ga), rows(d), rows(aw, 0), rows(aw, 1),
                  rows(aw, col_zb), rows(aw, col_zb + 1), rows(d),
                  full(wa.shape), full(wb.shape), full(wo.shape)],
        out_specs=rows(d),
        scratch_shapes=[pltpu.VMEM((aw // LANES, tm, LANES), F32)] * 4,
        compiler_params=_cparams(("parallel", "parallel")),
        name="merge_out",
    )(o_g[0], lse_g[0], o_g[1], lse_g[1], o_g[2], lse_g[2], z_gate, o_b,
      z_ga, z_ga, z_gate, z_gate, x3, wa, wb, wo)


def kernel(x, norm_w, w_in, q_norm_w, k_norm_w, rel_bias, lb_fwd, lb_bwd, hg_norm_w,
           w_proj_a, w_proj_b, w_out):
    b, seq, d = x.shape
    n = b * seq
    layer = 0
    w = w_in[layer].astype(BF16)
    qkv_cols = 3 * len(ATTN_GROUPS) * ATTN_WIDTH
    dilations = tuple(dil for _, dil in ATTN_GROUPS)
    assert dilations[0] == 1

    xn_all = _norm(x.astype(F32), norm_w[layer].reshape(1, d).astype(F32), dilations[1:])
    xn = xn_all[0].reshape(n, d)

    hw = HG_HEADS * HG_DK
    c_ga = qkv_cols // ATTN_WIDTH
    c_gate_b = c_ga + 1 + 6 * hw // ATTN_WIDTH
    z_gate = _proj_gates(xn, w, c_ga, c_gate_b, hw // ATTN_WIDTH).reshape(b, seq, -1)

    o_g, lse_g = [], []
    for g, dilation in enumerate(dilations):
        sub_len = seq // dilation
        q_scale = HEAD_DIM ** -0.5 * LOG2E
        nw = jnp.stack([jnp.tile(q_norm_w[layer, g].astype(F32), HEADS_PER_GROUP) * q_scale,
                        jnp.tile(k_norm_w[layer, g].astype(F32), HEADS_PER_GROUP)]
                       ).reshape(2, 1, ATTN_WIDTH)
        qkv = _proj_qkv(xn_all[g].reshape(n, d), w, g, nw, f"proj_qkv_d{dilation}")
        heads = slice(g * HEADS_PER_GROUP, (g + 1) * HEADS_PER_GROUP)
        bias = _attn_bias(rel_bias.astype(F32)[:, heads] * LOG2E, dilation)
        o, lse = _attention(qkv.reshape(b * dilation, sub_len, 3 * ATTN_WIDTH), bias,
                            f"attn_d{dilation}")
        shape = (b, seq, ATTN_WIDTH) if dilation == 1 else (b, dilation, sub_len, ATTN_WIDTH)
        o_g.append(o.reshape(shape))
        lse_g.append(lse.reshape(shape))

    lb_f = jnp.cumsum(jax.nn.softmax(lb_fwd.astype(F32), axis=0), axis=0)[layer].reshape(1, hw)
    lb_b = jnp.cumsum(jax.nn.softmax(lb_bwd.astype(F32), axis=0), axis=0)[layer].reshape(1, hw)
    hb = hw // LANES
    c_qb = (qkv_cols + ATTN_WIDTH) // LANES
    o_b, z_ga = _hgrn(xn_all[0], w, lb_f, lb_b, hg_norm_w[layer].reshape(1, HG_DK).astype(F32),
                      col_q=c_qb, col_zf=c_qb + hb, col_zb=c_qb + 2 * hb, col_i=c_qb + 3 * hb,
                      col_g=c_qb + 4 * hb, col_x=c_qb + 5 * hb)

    out = _final(o_g, lse_g, z_gate, z_ga, o_b, x.astype(F32),
                 w_proj_a[layer].astype(BF16), w_proj_b[layer].astype(BF16),
                 w_out[layer].astype(BF16), dilations[1:], col_ga=0, col_zb=1)
    return out.astype(x.dtype)
```

```python
import functools

import numpy as np
import jax
import jax.numpy as jnp
from jax import lax
from jax.experimental import pallas as pl
from jax.experimental.pallas import tpu as pltpu

F32 = jnp.float32
BF16 = jnp.bfloat16

EPS = 1e-6
LOG2E = float(np.log2(np.e))
NEG_INF = -1e30
ATTN_GROUPS = ((128, 1), (512, 4), (2048, 16))
HEAD_DIM = 64
HEADS_PER_GROUP = 8
ATTN_WIDTH = HEADS_PER_GROUP * HEAD_DIM
NUM_BUCKETS = 32
REL_MAX_DISTANCE = 1024
HG_HEADS = 8
HG_DK = 128

LANES = 128
SUBLANES = 8
PROJ_SUB = 512
VMEM_LIMIT = 52 * 1024 * 1024

ATTN_TQ = 128
ATTN_SIDE = 64
ATTN_TK = ATTN_TQ + 2 * ATTN_SIDE

HG_CHUNK = 64
HG_TILE = 256


def _cparams(sem):
    return pltpu.CompilerParams(dimension_semantics=sem, vmem_limit_bytes=VMEM_LIMIT)


def _emit_skewed(units, stages):
    for step in range(len(units) + len(stages) - 1):
        for k, stage in enumerate(stages):
            if 0 <= step - k < len(units):
                stage(units[step - k])


def _norm_kernel(x_ref, w_ref, o_ref, *rest, tm, dilations):
    perm_refs, y_ref = rest[:-1], rest[-1]
    x = x_ref[...]
    ms = jnp.mean(x * x, axis=-1, keepdims=True)
    y = x * lax.rsqrt(ms + EPS) * w_ref[...]
    o_ref[...] = y.astype(BF16)
    n_lane_blocks = y.shape[1] // LANES
    for c in range(n_lane_blocks):
        y_ref[c] = y[:, c * LANES:(c + 1) * LANES]
    for p_ref, dil in zip(perm_refs, dilations):
        for r in range(dil):
            for c in range(n_lane_blocks):
                p_ref[r, :, c * LANES:(c + 1) * LANES] = (
                    y_ref[c, pl.ds(r, tm // dil, stride=dil), :].astype(BF16))


def _norm(x3, w, dilations):
    b, seq, d = x3.shape
    tm = 1024
    out_shape = [jax.ShapeDtypeStruct((b, seq, d), BF16)]
    out_specs = [pl.BlockSpec((None, tm, d), lambda bi, i: (bi, i, 0))]
    for dil in dilations:
        out_shape.append(jax.ShapeDtypeStruct((b, dil, seq // dil, d), BF16))
        out_specs.append(pl.BlockSpec((None, dil, tm // dil, d), lambda bi, i: (bi, 0, i, 0)))
    return pl.pallas_call(
        functools.partial(_norm_kernel, tm=tm, dilations=dilations),
        out_shape=out_shape,
        grid=(b, seq // tm),
        in_specs=[pl.BlockSpec((None, tm, d), lambda bi, i: (bi, i, 0)),
                  pl.BlockSpec((1, d), lambda bi, i: (0, 0))],
        out_specs=out_specs,
        scratch_shapes=[pltpu.VMEM((d // LANES, tm, LANES), F32)],
        compiler_params=_cparams(("parallel", "parallel")),
        name="rmsnorm",
    )(x3, w)


def _proj_kernel(x_ref, w_ref, o_ref):
    o_ref[...] = jnp.dot(x_ref[...], w_ref[...], preferred_element_type=F32).astype(o_ref.dtype)


def _proj_gates(xn, w, col_a, col_b, n_b):
    n, d = xn.shape
    tm, tn = 4096, ATTN_WIDTH
    return pl.pallas_call(
        _proj_kernel,
        out_shape=jax.ShapeDtypeStruct((n, (1 + n_b) * tn), BF16),
        grid=(n // tm, 1 + n_b),
        in_specs=[pl.BlockSpec((tm, d), lambda i, j: (i, 0)),
                  pl.BlockSpec((d, tn), lambda i, j: (0, jnp.where(j == 0, col_a, col_b - 1 + j)))],
        out_specs=pl.BlockSpec((tm, tn), lambda i, j: (i, j)),
        compiler_params=_cparams(("parallel", "arbitrary")),
        name="proj_gates",
    )(xn, w)


def _proj_qkv_kernel(x_ref, w_ref, nw_ref, seg_ref, o_ref):
    j = pl.program_id(1)

    @pl.when(j < 2)
    def _():
        units = [dict(rows=slice(r * PROJ_SUB, (r + 1) * PROJ_SUB))
                 for r in range(x_ref.shape[0] // PROJ_SUB)]

        def project(u):
            u["z"] = jnp.dot(x_ref[u["rows"], :], w_ref[...], preferred_element_type=F32)

        def square(u):
            u["zz"] = (u["z"] * u["z"]).astype(BF16)

        def mean_square(u):
            u["ms"] = jnp.dot(u.pop("zz"), seg_ref[...], preferred_element_type=F32)

        def normalise(u):
            o_ref[u["rows"], :] = (u.pop("z") * lax.rsqrt(u.pop("ms") + EPS) * nw_ref[...]
                                   ).astype(BF16)

        _emit_skewed(units, (project, square, mean_square, normalise))

    @pl.when(j == 2)
    def _():
        o_ref[...] = jnp.dot(x_ref[...], w_ref[...], preferred_element_type=F32).astype(BF16)


def _proj_qkv(xn2, w, group, nw, name):
    n, d = xn2.shape
    tm, tn = 4096, ATTN_WIDTH
    head = np.arange(tn) // HEAD_DIM
    seg = jnp.asarray((head[:, None] == head[None, :]) / HEAD_DIM, dtype=BF16)
    return pl.pallas_call(
        _proj_qkv_kernel,
        out_shape=jax.ShapeDtypeStruct((n, 3 * tn), BF16),
        grid=(n // tm, 3),
        in_specs=[pl.BlockSpec((tm, d), lambda i, j: (i, 0)),
                  pl.BlockSpec((d, tn), lambda i, j: (0, 3 * group + j)),
                  pl.BlockSpec((None, 1, tn), lambda i, j: (jnp.minimum(j, 1), 0, 0)),
                  pl.BlockSpec((tn, tn), lambda i, j: (0, 0))],
        out_specs=pl.BlockSpec((tm, tn), lambda i, j: (i, j)),
        compiler_params=_cparams(("parallel", "arbitrary")),
        name=name,
    )(xn2, w, nw, seg)


def _t5_bucket(rel):
    half = NUM_BUCKETS // 2
    max_exact = half // 2
    n = np.abs(rel)
    large = max_exact + (np.log(np.maximum(n, 1) / max_exact)
                         / np.log(REL_MAX_DISTANCE / max_exact) * (half - max_exact)).astype(np.int32)
    large = np.minimum(large, half - 1)
    return np.where(rel > 0, half, 0) + np.where(n < max_exact, n, large)


def _attn_bias(bias_tab, dilation):
    t = np.arange(ATTN_TQ)[:, None]
    j = np.arange(ATTN_TK)[None, :]
    rel = np.stack([j - off - t for off in (0, ATTN_SIDE, 2 * ATTN_SIDE)])
    valid = (np.abs(rel) <= ATTN_SIDE).reshape(-1)
    bucket = _t5_bucket(rel * dilation).reshape(-1).astype(np.int32)
    onehot = (jnp.asarray(bucket)[None, :] == jnp.arange(NUM_BUCKETS, dtype=jnp.int32)[:, None])
    vals = jnp.dot(bias_tab.T, onehot.astype(F32), precision=lax.Precision.HIGHEST)
    vals = jnp.where(jnp.asarray(valid)[None, :], vals, NEG_INF)
    return vals.reshape(HEADS_PER_GROUP, 3, ATTN_TQ, ATTN_TK)


def _attn_kernel(q_ref, k_ref, v_ref, bias_ref, o_ref, lse_ref, *, seq, n_sub, n_pairs, blocks):
    nb = seq // ATTN_TQ
    lane = lax.broadcasted_iota(jnp.int32, (ATTN_TQ, LANES), 1)
    first = lane < HEAD_DIM
    nt = (((1,), (1,)), ((), ()))

    def both_heads(col):
        return jnp.where(first, jnp.broadcast_to(col[:ATTN_TQ], (ATTN_TQ, LANES)),
                         jnp.broadcast_to(col[ATTN_TQ:], (ATTN_TQ, LANES)))

    def run(subs, i0):
        units = []
        for sub in subs:
            for bo in range(blocks):
                i = i0 + bo
                q0 = pl.multiple_of(i * ATTN_TQ, ATTN_TQ)
                ks = pl.multiple_of(jnp.clip(i * ATTN_TQ - ATTN_SIDE, 0, seq - ATTN_TK), ATTN_SIDE)
                var = jnp.where(i == 0, 0, jnp.where(i == nb - 1, 2, 1))
                for pr in range(n_pairs):
                    units.append(dict(sub=sub, q0=q0, ks=ks, var=var, pr=pr,
                                      cols=slice(pr * LANES, (pr + 1) * LANES)))

        def scores(u):
            q2 = q_ref[u["sub"], pl.ds(u["q0"], ATTN_TQ), u["cols"]]
            k2 = k_ref[u["sub"], pl.ds(u["ks"], ATTN_TK), u["cols"]]
            zero = jnp.zeros_like(q2)
            qq = jnp.concatenate([jnp.where(first, q2, zero), jnp.where(first, zero, q2)], axis=0)
            u["s"] = lax.dot_general(qq, k2, nt, preferred_element_type=F32)

        def row_max(u):
            pr, var = u["pr"], u["var"]
            u["s"] = u["s"] + jnp.concatenate([bias_ref[2 * pr, var], bias_ref[2 * pr + 1, var]],
                                              axis=0)
            u["m"] = jnp.max(u["s"], axis=-1, keepdims=True)

        def probs(u):
            p = jnp.exp2(u.pop("s") - u["m"])
            u["den"] = jnp.sum(p, axis=-1, keepdims=True)
            u["p"] = p.astype(BF16)

        def values(u):
            v2 = v_ref[u["sub"], pl.ds(u["ks"], ATTN_TK), u["cols"]]
            u["pv"] = jnp.dot(u.pop("p"), v2, preferred_element_type=F32)

        def finish(u):
            den = both_heads(u.pop("den"))
            pv = u.pop("pv")
            rows = pl.ds(u["q0"], ATTN_TQ)
            o_ref[u["sub"], rows, u["cols"]] = (jnp.where(first, pv[:ATTN_TQ], pv[ATTN_TQ:])
                                                * (1.0 / den)).astype(BF16)
            lse_ref[u["sub"], rows, u["cols"]] = (both_heads(u.pop("m"))
                                                  + jnp.log(den) * (1.0 / np.log(2.0)))

        _emit_skewed(units, (scores, row_max, probs, values, finish))

    if nb == blocks:
        run(range(n_sub), 0)
    else:
        def body(j, carry):
            run(range(n_sub), j * blocks)
            return carry

        lax.fori_loop(0, nb // blocks, body, 0)


def _attention(qkv, bias, name):
    n_seq, seq, _ = qkv.shape
    assert seq >= 2 * ATTN_TQ and seq % ATTN_TQ == 0
    if seq >= 4096:
        n_sub, n_pairs = 1, 2
    else:
        n_pairs = ATTN_WIDTH // LANES
        n_sub = max(1, min(n_seq, 1024 // seq))
    blocks = max(2, min(seq // ATTN_TQ, 32 // (n_sub * n_pairs)))
    assert (seq // ATTN_TQ) % blocks == 0
    cw = n_pairs * LANES
    pb = ATTN_WIDTH // cw
    kern = functools.partial(_attn_kernel, seq=seq, n_sub=n_sub, n_pairs=n_pairs, blocks=blocks)
    blk = (n_sub, seq, cw)
    return pl.pallas_call(
        kern,
        out_shape=(jax.ShapeDtypeStruct((n_seq, seq, ATTN_WIDTH), BF16),
                   jax.ShapeDtypeStruct((n_seq, seq, ATTN_WIDTH), F32)),
        grid=(n_seq // n_sub, pb),
        in_specs=[pl.BlockSpec(blk, lambda r, p: (r, 0, p)),
                  pl.BlockSpec(blk, lambda r, p: (r, 0, pb + p)),
                  pl.BlockSpec(blk, lambda r, p: (r, 0, 2 * pb + p)),
                  pl.BlockSpec((2 * n_pairs, 3, ATTN_TQ, ATTN_TK), lambda r, p: (p, 0, 0, 0))],
        out_specs=(pl.BlockSpec(blk, lambda r, p: (r, 0, p)),
                   pl.BlockSpec(blk, lambda r, p: (r, 0, p))),
        compiler_params=_cparams(("parallel", "parallel")),
        name=name,
    )(qkv, qkv, qkv, bias)


def _hgrn_mats():
    t = np.arange(HG_TILE)[:, None]
    u = np.arange(HG_TILE)[None, :]
    same = (t // HG_CHUNK) == (u // HG_CHUNK)
    return (jnp.asarray(same & (u <= t), dtype=BF16), jnp.asarray(same & (u >= t), dtype=BF16))


def _hgrn_kernel(xn_ref, wq_ref, wzf_ref, wzb_ref, wi_ref, wg_ref, wx_ref, lbf_ref, lbb_ref,
                 nw_ref, mf_ref, mb_ref, o_ref, x_out_ref, wcat_ref, gate_ref, acc_ref, qd_ref,
                 kv_ref, dec_ref, st_ref, *, seq):
    n_tiles = seq // HG_TILE
    cpt = HG_TILE // HG_CHUNK
    mid = HG_CHUNK // 2
    nt = (((1,), (1,)), ((), ()))
    w_refs = (wq_ref, wzf_ref, wzb_ref, wi_ref, wg_ref, wx_ref)

    def per_chunk_rows(rows):
        return jnp.concatenate([jnp.broadcast_to(r, (HG_CHUNK, HG_DK)) for r in rows], axis=0)

    scans = ((lbf_ref, mf_ref, False), (lbb_ref, mb_ref, True))

    for k, w_ref in enumerate(w_refs):
        wcat_ref[:, k * HG_DK:(k + 1) * HG_DK] = w_ref[...]

    def a_project(u):
        z = jnp.dot(xn_ref[u["rows"], :], wcat_ref[...], preferred_element_type=F32)
        u["q"] = z[:, 0:HG_DK]
        u["z"] = [z[:, HG_DK:2 * HG_DK], z[:, 2 * HG_DK:3 * HG_DK]]
        u["v"] = z[:, 3 * HG_DK:4 * HG_DK].astype(BF16)
        gate_ref[u["rows"], :] = z[:, 4 * HG_DK:5 * HG_DK]
        x_out_ref[u["rows"], :] = jax.nn.sigmoid(z[:, 5 * HG_DK:]).astype(BF16)

    def a_gate(u):
        u["kk"], u["ghl"] = [], []
        for (lb_ref, _, _), z in zip(scans, u.pop("z")):
            kk = (1.0 - lb_ref[...]) / (1.0 + jnp.exp(z))
            g = jnp.log(1.0 - kk) * (1.0 / np.log(2.0))
            g_hi = g.astype(BF16)
            g_lo = (g - g_hi.astype(F32)).astype(BF16)
            u["kk"].append(kk)
            u["ghl"].append(jnp.concatenate([g_hi, g_lo], axis=1))

    def a_cumsum(u):
        u["sums"] = [jnp.dot(m_ref[...], ghl, preferred_element_type=F32)
                     for (_, m_ref, _), ghl in zip(scans, u.pop("ghl"))]

    def a_decay(u):
        q = u.pop("q")
        q_c, k_c, q_d, k_s, decs = [], [], [], [], []
        for (_, _, reverse), sums, kk in zip(scans, u.pop("sums"), u.pop("kk")):
            beta = sums[:, :HG_DK] + sums[:, HG_DK:]
            tot_row = 0 if reverse else HG_CHUNK - 1
            cen_row = mid if reverse else mid - 1
            tot = [beta[c * HG_CHUNK + tot_row:c * HG_CHUNK + tot_row + 1] for c in range(cpt)]
            cen = [beta[c * HG_CHUNK + cen_row:c * HG_CHUNK + cen_row + 1] for c in range(cpt)]
            d = beta - per_chunk_rows(cen)
            e_q = jnp.exp2(d)
            qc = (q * e_q).astype(BF16)
            kc = (kk * (1.0 / e_q)).astype(BF16)
            q_d.append(qc * per_chunk_rows([jnp.exp2(c_).astype(BF16) for c_ in cen]))
            k_s.append(kc * per_chunk_rows([jnp.exp2(t_ - c_).astype(BF16)
                                            for t_, c_ in zip(tot, cen)]))
            q_c.append(qc)
            k_c.append(kc)
            decs.append([jnp.exp2(t_) for t_ in tot])
        qd_ref[u["rows"], :] = jnp.concatenate(q_d, axis=1)
        u["ks"] = jnp.concatenate(k_s, axis=1)
        u["qc"], u["kc"] = q_c, k_c
        for c in range(cpt):
            dec_ref[u["t"] * cpt + c] = jnp.concatenate(
                [jnp.broadcast_to(decs[0][c], (SUBLANES, HG_DK)),
                 jnp.broadcast_to(decs[1][c], (SUBLANES, HG_DK))],
                axis=1)

    chunk_rows = [slice(c * HG_CHUNK, (c + 1) * HG_CHUNK) for c in range(cpt)]
    ti = lax.broadcasted_iota(jnp.int32, (HG_CHUNK, HG_CHUNK), 0)
    ui = lax.broadcasted_iota(jnp.int32, (HG_CHUNK, HG_CHUNK), 1)
    past, future = ui <= ti, ui >= ti

    def a_scores(u):
        u["a"] = [[lax.dot_general(qc[cr], kc[cr], nt, preferred_element_type=F32)
                   for qc, kc in zip(u["qc"], u["kc"])] for cr in chunk_rows]
        del u["qc"], u["kc"]

    def a_mask(u):
        u["a"] = [(jnp.where(past, a_f, 0.0) + jnp.where(future, a_b, 0.0)).astype(BF16)
                  for a_f, a_b in u.pop("a")]

    def a_values(u):
        v = u.pop("v")
        acc_ref[u["rows"], :] = jnp.concatenate(
            [jnp.dot(a, v[cr], preferred_element_type=F32) for a, cr in zip(u.pop("a"), chunk_rows)],
            axis=0)
        ks = u.pop("ks")
        for c, cr in enumerate(chunk_rows):
            kv_ref[u["t"] * cpt + c] = lax.dot_general(v[cr], ks[cr], (((0,), (0,)), ((), ())),
                                                       preferred_element_type=F32)

    fw = slice(0, HG_DK)
    bw = slice(HG_DK, 2 * HG_DK)
    state = dict(f=jnp.zeros((HG_DK, HG_DK), F32), b=jnp.zeros((HG_DK, HG_DK), F32))

    def a_state(u):
        for c in range(cpt):
            n = u["t"] * cpt + c
            st_ref[n, :, fw] = state["f"].astype(BF16)
            state["f"] = dec_ref[n][0:1, fw] * state["f"] + kv_ref[n, :, fw]

    tiles = [dict(t=t, r0=t * HG_TILE, rows=slice(t * HG_TILE, (t + 1) * HG_TILE))
             for t in range(n_tiles)]
    _emit_skewed([dict(u) for u in tiles],
                 (a_project, a_gate, a_cumsum, a_decay, a_scores, a_mask, a_values, a_state))

    def c_state(u):
        for c in reversed(range(cpt)):
            m = u["t"] * cpt + c
            st_ref[m, :, bw] = state["b"].astype(BF16)
            state["b"] = dec_ref[m][0:1, bw] * state["b"] + kv_ref[m, :, bw]

    def c_inter(u):
        u["inter"] = [lax.dot_general(qd_ref[u["r0"] + c * HG_CHUNK:u["r0"] + (c + 1) * HG_CHUNK, :],
                                      st_ref[u["t"] * cpt + c], nt, preferred_element_type=F32)
                      for c in range(cpt)]

    def c_sum(u):
        u["o"] = acc_ref[u["rows"], :] + jnp.concatenate(u.pop("inter"), axis=0)
        u["ms"] = jnp.mean(u["o"] * u["o"], axis=-1, keepdims=True)

    def c_out(u):
        gate = gate_ref[u["rows"], :]
        o_ref[u["rows"], :] = (u.pop("o") * lax.rsqrt(u.pop("ms") + EPS) * nw_ref[...]
                               * (gate * jax.nn.sigmoid(gate))).astype(BF16)

    _emit_skewed([dict(u) for u in reversed(tiles)], (c_state, c_inter, c_sum, c_out))


def _hgrn(xn3, w, lb_f, lb_b, nw, col_q, col_zf, col_zb, col_i, col_g, col_x):
    b, seq, d = xn3.shape
    mf, mb = _hgrn_mats()
    n_chunks = seq // HG_CHUNK

    def wspec(col):
        return pl.BlockSpec((d, HG_DK), lambda bi, h, col=col: (0, col + h))

    vec = pl.BlockSpec((1, HG_DK), lambda bi, h: (0, h))
    const = pl.BlockSpec((HG_TILE, HG_TILE), lambda bi, h: (0, 0))
    return pl.pallas_call(
        functools.partial(_hgrn_kernel, seq=seq),
        out_shape=[jax.ShapeDtypeStruct((b, seq, HG_HEADS * HG_DK), BF16)] * 2,
        grid=(b, HG_HEADS),
        in_specs=[pl.BlockSpec((None, seq, d), lambda bi, h: (bi, 0, 0)),
                  wspec(col_q), wspec(col_zf), wspec(col_zb), wspec(col_i), wspec(col_g),
                  wspec(col_x),
                  vec, vec, pl.BlockSpec((1, HG_DK), lambda bi, h: (0, 0)), const, const],
        out_specs=[pl.BlockSpec((None, seq, HG_DK), lambda bi, h: (bi, 0, h))] * 2,
        scratch_shapes=[pltpu.VMEM((d, 6 * HG_DK), BF16),
                        pltpu.VMEM((seq, HG_DK), F32),
                        pltpu.VMEM((seq, HG_DK), F32),
                        pltpu.VMEM((seq, 2 * HG_DK), BF16),
                        pltpu.VMEM((n_chunks, HG_DK, 2 * HG_DK), F32),
                        pltpu.VMEM((n_chunks, SUBLANES, 2 * HG_DK), F32),
                        pltpu.VMEM((n_chunks, HG_DK, 2 * HG_DK), BF16)],
        compiler_params=_cparams(("parallel", "parallel")),
        name="hgrn2",
    )(xn3, w, w, w, w, w, w, lb_f, lb_b, nw, mf, mb)


def _final_kernel(o0_ref, l0_ref, o1_ref, l1_ref, o2_ref, l2_ref, ga_ref, ob_ref,
                  za0_ref, za1_ref, zb0_ref, zb1_ref,
                  x_ref, wa_ref, wb_ref, wo_ref, out_ref, so1_ref, sl1_ref, so2_ref, sl2_ref,
                  *, tm, dilations):
    for src, dst, dil in ((o1_ref, so1_ref, dilations[0]), (l1_ref, sl1_ref, dilations[0]),
                          (o2_ref, so2_ref, dilations[1]), (l2_ref, sl2_ref, dilations[1])):
        for r in range(dil):
            val = src[r].astype(F32)
            for c in range(ATTN_WIDTH // LANES):
                dst[c, pl.ds(r, tm // dil, stride=dil), :] = val[:, c * LANES:(c + 1) * LANES]

    def natural(ref):
        return jnp.concatenate([ref[c] for c in range(ATTN_WIDTH // LANES)], axis=1)

    l0, l1, l2 = l0_ref[...], natural(sl1_ref), natural(sl2_ref)
    mx = jnp.maximum(jnp.maximum(l0, l1), l2)
    w0, w1, w2 = jnp.exp2(l0 - mx), jnp.exp2(l1 - mx), jnp.exp2(l2 - mx)
    num = w0 * o0_ref[...].astype(F32) + w1 * natural(so1_ref) + w2 * natural(so2_ref)
    ga = ga_ref[...].astype(F32)
    a = (num / (w0 + w1 + w2) * (ga * jax.nn.sigmoid(ga))).astype(BF16)

    y_a = jnp.dot(a, wa_ref[...], preferred_element_type=F32)
    y_b = jnp.dot(ob_ref[...], wb_ref[...], preferred_element_type=F32)
    gate_a = jnp.concatenate([za0_ref[...], za1_ref[...]], axis=1).astype(F32)
    zb = jnp.concatenate([zb0_ref[...], zb1_ref[...]], axis=1).astype(F32)
    merged = (gate_a * y_a + jax.nn.sigmoid(zb) * y_b).astype(BF16)
    out_ref[...] = x_ref[...] + jnp.dot(merged, wo_ref[...], preferred_element_type=F32)


def _final(o_g, lse_g, z_gate, z_ga, o_b, x3, wa, wb, wo, dilations, col_ga, col_zb):
    b, seq, d = x3.shape
    tm = 512
    aw = ATTN_WIDTH

    def rows(width, col=0):
        return pl.BlockSpec((None, tm, width), lambda bi, i, col=col: (bi, i, col))

    def perm(dil):
        return pl.BlockSpec((None, dil, tm // dil, aw), lambda bi, i: (bi, 0, i, 0))

    def full(shape):
        return pl.BlockSpec(shape, lambda bi, i: (0, 0))

    d1, d2 = dilations
    return pl.pallas_call(
        functools.partial(_final_kernel, tm=tm, dilations=dilations),
        out_shape=jax.ShapeDtypeStruct((b, seq, d), F32),
        grid=(b, seq // tm),
        in_specs=[rows(aw), rows(aw), perm(d1), perm(d1), perm(d2), perm(d2),
                  rows(aw, col_ga), rows(d), rows(aw, 0), rows(aw, 1),
                  rows(aw, col_zb), rows(aw, col_zb + 1), rows(d),
                  full(wa.shape), full(wb.shape), full(wo.shape)],
        out_specs=rows(d),
        scratch_shapes=[pltpu.VMEM((aw // LANES, tm, LANES), F32)] * 4,
        compiler_params=_cparams(("parallel", "parallel")),
        name="merge_out",
    )(o_g[0], lse_g[0], o_g[1], lse_g[1], o_g[2], lse_g[2], z_gate, o_b,
      z_ga, z_ga, z_gate, z_gate, x3, wa, wb, wo)


def kernel(x, norm_w, w_in, q_norm_w, k_norm_w, rel_bias, lb_fwd, lb_bwd, hg_norm_w,
           w_proj_a, w_proj_b, w_out):
    b, seq, d = x.shape
    n = b * seq
    layer = 0
    w = w_in[layer].astype(BF16)
    qkv_cols = 3 * len(ATTN_GROUPS) * ATTN_WIDTH
    dilations = tuple(dil for _, dil in ATTN_GROUPS)
    assert dilations[0] == 1

    xn_all = _norm(x.astype(F32), norm_w[layer].reshape(1, d).astype(F32), dilations[1:])
    xn = xn_all[0].reshape(n, d)

    hw = HG_HEADS * HG_DK
    c_ga = qkv_cols // ATTN_WIDTH
    c_gate_b = c_ga + 1 + 6 * hw // ATTN_WIDTH
    z_gate = _proj_gates(xn, w, c_ga, c_gate_b, hw // ATTN_WIDTH).reshape(b, seq, -1)

    o_g, lse_g = [], []
    for g, dilation in enumerate(dilations):
        sub_len = seq // dilation
        q_scale = HEAD_DIM ** -0.5 * LOG2E
        nw = jnp.stack([jnp.tile(q_norm_w[layer, g].astype(F32), HEADS_PER_GROUP) * q_scale,
                        jnp.tile(k_norm_w[layer, g].astype(F32), HEADS_PER_GROUP)]
                       ).reshape(2, 1, ATTN_WIDTH)
        qkv = _proj_qkv(xn_all[g].reshape(n, d), w, g, nw, f"proj_qkv_d{dilation}")
        heads = slice(g * HEADS_PER_GROUP, (g + 1) * HEADS_PER_GROUP)
        bias = _attn_bias(rel_bias.astype(F32)[:, heads] * LOG2E, dilation)
        o, lse = _attention(qkv.reshape(b * dilation, sub_len, 3 * ATTN_WIDTH), bias,
                            f"attn_d{dilation}")
        shape = (b, seq, ATTN_WIDTH) if dilation == 1 else (b, dilation, sub_len, ATTN_WIDTH)
        o_g.append(o.reshape(shape))
        lse_g.append(lse.reshape(shape))

    lb_f = jnp.cumsum(jax.nn.softmax(lb_fwd.astype(F32), axis=0), axis=0)[layer].reshape(1, hw)
    lb_b = jnp.cumsum(jax.nn.softmax(lb_bwd.astype(F32), axis=0), axis=0)[layer].reshape(1, hw)
    hb = hw // LANES
    c_qb = (qkv_cols + ATTN_WIDTH) // LANES
    o_b, z_ga = _hgrn(xn_all[0], w, lb_f, lb_b, hg_norm_w[layer].reshape(1, HG_DK).astype(F32),
                      col_q=c_qb, col_zf=c_qb + hb, col_zb=c_qb + 2 * hb, col_i=c_qb + 3 * hb,
                      col_g=c_qb + 4 * hb, col_x=c_qb + 5 * hb)

    out = _final(o_g, lse_g, z_gate, z_ga, o_b, x.astype(F32),
                 w_proj_a[layer].astype(BF16), w_proj_b[layer].astype(BF16),
                 w_out[layer].astype(BF16), dilations[1:], col_ga=0, col_zb=1)
    return out.astype(x.dtype)
```

```python
import functools

import numpy as np
import jax
import jax.numpy as jnp
from jax import lax
from jax.experimental import pallas as pl
from jax.experimental.pallas import tpu as pltpu

F32 = jnp.float32
BF16 = jnp.bfloat16

EPS = 1e-6
LOG2E = float(np.log2(np.e))
NEG_INF = -1e30
ATTN_GROUPS = ((128, 1), (512, 4), (2048, 16))
HEAD_DIM = 64
HEADS_PER_GROUP = 8
ATTN_WIDTH = HEADS_PER_GROUP * HEAD_DIM
NUM_BUCKETS = 32
REL_MAX_DISTANCE = 1024
HG_HEADS = 8
HG_DK = 128

LANES = 128
SUBLANES = 8
PROJ_SUB = 512
VMEM_LIMIT = 52 * 1024 * 1024

ATTN_TQ = 128
ATTN_SIDE = 64
ATTN_TK = ATTN_TQ + 2 * ATTN_SIDE

HG_CHUNK = 64
HG_TILE = 256


def _cparams(sem):
    return pltpu.CompilerParams(dimension_semantics=sem, vmem_limit_bytes=VMEM_LIMIT)


def _emit_skewed(units, stages):
    for step in range(len(units) + len(stages) - 1):
        for k, stage in enumerate(stages):
            if 0 <= step - k < len(units):
                stage(units[step - k])


def _norm_kernel(x_ref, w_ref, o_ref, *rest, tm, dilations):
    perm_refs, y_ref = rest[:-1], rest[-1]
    x = x_ref[...]
    ms = jnp.mean(x * x, axis=-1, keepdims=True)
    y = x * lax.rsqrt(ms + EPS) * w_ref[...]
    o_ref[...] = y.astype(BF16)
    n_lane_blocks = y.shape[1] // LANES
    for c in range(n_lane_blocks):
        y_ref[c] = y[:, c * LANES:(c + 1) * LANES]
    for p_ref, dil in zip(perm_refs, dilations):
        for r in range(dil):
            for c in range(n_lane_blocks):
                p_ref[r, :, c * LANES:(c + 1) * LANES] = (
                    y_ref[c, pl.ds(r, tm // dil, stride=dil), :].astype(BF16))


def _norm(x3, w, dilations):
    b, seq, d = x3.shape
    tm = 1024
    out_shape = [jax.ShapeDtypeStruct((b, seq, d), BF16)]
    out_specs = [pl.BlockSpec((None, tm, d), lambda bi, i: (bi, i, 0))]
    for dil in dilations:
        out_shape.append(jax.ShapeDtypeStruct((b, dil, seq // dil, d), BF16))
        out_specs.append(pl.BlockSpec((None, dil, tm // dil, d), lambda bi, i: (bi, 0, i, 0)))
    return pl.pallas_call(
        functools.partial(_norm_kernel, tm=tm, dilations=dilations),
        out_shape=out_shape,
        grid=(b, seq // tm),
        in_specs=[pl.BlockSpec((None, tm, d), lambda bi, i: (bi, i, 0)),
                  pl.BlockSpec((1, d), lambda bi, i: (0, 0))],
        out_specs=out_specs,
        scratch_shapes=[pltpu.VMEM((d // LANES, tm, LANES), F32)],
        compiler_params=_cparams(("parallel", "parallel")),
        name="rmsnorm",
    )(x3, w)


def _proj_kernel(x_ref, w_ref, o_ref):
    o_ref[...] = jnp.dot(x_ref[...], w_ref[...], preferred_element_type=F32).astype(o_ref.dtype)


def _proj_gates(xn, w, col_a, col_b, n_b):
    n, d = xn.shape
    tm, tn = 4096, ATTN_WIDTH
    return pl.pallas_call(
        _proj_kernel,
        out_shape=jax.ShapeDtypeStruct((n, (1 + n_b) * tn), BF16),
        grid=(n // tm, 1 + n_b),
        in_specs=[pl.BlockSpec((tm, d), lambda i, j: (i, 0)),
                  pl.BlockSpec((d, tn), lambda i, j: (0, jnp.where(j == 0, col_a, col_b - 1 + j)))],
        out_specs=pl.BlockSpec((tm, tn), lambda i, j: (i, j)),
        compiler_params=_cparams(("parallel", "arbitrary")),
        name="proj_gates",
    )(xn, w)


def _proj_qkv_kernel(x_ref, w_ref, nw_ref, seg_ref, o_ref):
    j = pl.program_id(1)

    @pl.when(j < 2)
    def _():
        units = [dict(rows=slice(r * PROJ_SUB, (r + 1) * PROJ_SUB))
                 for r in range(x_ref.shape[0] // PROJ_SUB)]

        def project(u):
            u["z"] = jnp.dot(x_ref[u["rows"], :], w_ref[...], preferred_element_type=F32)

        def square(u):
            u["zz"] = (u["z"] * u["z"]).astype(BF16)

        def mean_square(u):
            u["ms"] = jnp.dot(u.pop("zz"), seg_ref[...], preferred_element_type=F32)

        def normalise(u):
            o_ref[u["rows"], :] = (u.pop("z") * lax.rsqrt(u.pop("ms") + EPS) * nw_ref[...]
                                   ).astype(BF16)

        _emit_skewed(units, (project, square, mean_square, normalise))

    @pl.when(j == 2)
    def _():
        o_ref[...] = jnp.dot(x_ref[...], w_ref[...], preferred_element_type=F32).astype(BF16)


def _proj_qkv(xn2, w, group, nw, name):
    n, d = xn2.shape
    tm, tn = 4096, ATTN_WIDTH
    head = np.arange(tn) // HEAD_DIM
    seg = jnp.asarray((head[:, None] == head[None, :]) / HEAD_DIM, dtype=BF16)
    return pl.pallas_call(
        _proj_qkv_kernel,
        out_shape=jax.ShapeDtypeStruct((n, 3 * tn), BF16),
        grid=(n // tm, 3),
        in_specs=[pl.BlockSpec((tm, d), lambda i, j: (i, 0)),
                  pl.BlockSpec((d, tn), lambda i, j: (0, 3 * group + j)),
                  pl.BlockSpec((None, 1, tn), lambda i, j: (jnp.minimum(j, 1), 0, 0)),
                  pl.BlockSpec((tn, tn), lambda i, j: (0, 0))],
        out_specs=pl.BlockSpec((tm, tn), lambda i, j: (i, j)),
        compiler_params=_cparams(("parallel", "arbitrary")),
        name=name,
    )(xn2, w, nw, seg)


def _t5_bucket(rel):
    half = NUM_BUCKETS // 2
    max_exact = half // 2
    n = np.abs(rel)
    large = max_exact + (np.log(np.maximum(n, 1) / max_exact)
                         / np.log(REL_MAX_DISTANCE / max_exact) * (half - max_exact)).astype(np.int32)
    large = np.minimum(large, half - 1)
    return np.where(rel > 0, half, 0) + np.where(n < max_exact, n, large)


def _attn_bias(bias_tab, dilation):
    t = np.arange(ATTN_TQ)[:, None]
    j = np.arange(ATTN_TK)[None, :]
    rel = np.stack([j - off - t for off in (0, ATTN_SIDE, 2 * ATTN_SIDE)])
    valid = (np.abs(rel) <= ATTN_SIDE).reshape(-1)
    bucket = _t5_bucket(rel * dilation).reshape(-1).astype(np.int32)
    onehot = (jnp.asarray(bucket)[None, :] == jnp.arange(NUM_BUCKETS, dtype=jnp.int32)[:, None])
    vals = jnp.dot(bias_tab.T, onehot.astype(F32), precision=lax.Precision.HIGHEST)
    vals = jnp.where(jnp.asarray(valid)[None, :], vals, NEG_INF)
    return vals.reshape(HEADS_PER_GROUP, 3, ATTN_TQ, ATTN_TK)


def _attn_kernel(q_ref, k_ref, v_ref, bias_ref, o_ref, lse_ref, *, seq, n_sub, n_pairs, blocks):
    nb = seq // ATTN_TQ
    lane = lax.broadcasted_iota(jnp.int32, (ATTN_TQ, LANES), 1)
    first = lane < HEAD_DIM
    nt = (((1,), (1,)), ((), ()))

    def both_heads(col):
        return jnp.where(first, jnp.broadcast_to(col[:ATTN_TQ], (ATTN_TQ, LANES)),
                         jnp.broadcast_to(col[ATTN_TQ:], (ATTN_TQ, LANES)))

    def run(subs, i0):
        units = []
        for sub in subs:
            for bo in range(blocks):
                i = i0 + bo
                q0 = pl.multiple_of(i * ATTN_TQ, ATTN_TQ)
                ks = pl.multiple_of(jnp.clip(i * ATTN_TQ - ATTN_SIDE, 0, seq - ATTN_TK), ATTN_SIDE)
                var = jnp.where(i == 0, 0, jnp.where(i == nb - 1, 2, 1))
                for pr in range(n_pairs):
                    units.append(dict(sub=sub, q0=q0, ks=ks, var=var, pr=pr,
                                      cols=slice(pr * LANES, (pr + 1) * LANES)))

        def scores(u):
            q2 = q_ref[u["sub"], pl.ds(u["q0"], ATTN_TQ), u["cols"]]
            k2 = k_ref[u["sub"], pl.ds(u["ks"], ATTN_TK), u["cols"]]
            zero = jnp.zeros_like(q2)
            qq = jnp.concatenate([jnp.where(first, q2, zero), jnp.where(first, zero, q2)], axis=0)
            u["s"] = lax.dot_general(qq, k2, nt, preferred_element_type=F32)

        def row_max(u):
            pr, var = u["pr"], u["var"]
            u["s"] = u["s"] + jnp.concatenate([bias_ref[2 * pr, var], bias_ref[2 * pr + 1, var]],
                                              axis=0)
            u["m"] = jnp.max(u["s"], axis=-1, keepdims=True)

        def probs(u):
            p = jnp.exp2(u.pop("s") - u["m"])
            u["den"] = jnp.sum(p, axis=-1, keepdims=True)
            u["p"] = p.astype(BF16)

        def values(u):
            v2 = v_ref[u["sub"], pl.ds(u["ks"], ATTN_TK), u["cols"]]
            u["pv"] = jnp.dot(u.pop("p"), v2, preferred_element_type=F32)

        def finish(u):
            den = both_heads(u.pop("den"))
            pv = u.pop("pv")
            rows = pl.ds(u["q0"], ATTN_TQ)
            o_ref[u["sub"], rows, u["cols"]] = (jnp.where(first, pv[:ATTN_TQ], pv[ATTN_TQ:])
                                                * (1.0 / den)).astype(BF16)
            lse_ref[u["sub"], rows, u["cols"]] = (both_heads(u.pop("m"))
                                                  + jnp.log(den) * (1.0 / np.log(2.0)))

        _emit_skewed(units, (scores, row_max, probs, values, finish))

    if nb == blocks:
        run(range(n_sub), 0)
    else:
        def body(j, carry):
            run(range(n_sub), j * blocks)
            return carry

        lax.fori_loop(0, nb // blocks, body, 0)


def _attention(qkv, bias, name):
    n_seq, seq, _ = qkv.shape
    assert seq >= 2 * ATTN_TQ and seq % ATTN_TQ == 0
    if seq >= 4096:
        n_sub, n_pairs = 1, 2
    else:
        n_pairs = ATTN_WIDTH // LANES
        n_sub = max(1, min(n_seq, 1024 // seq))
    blocks = max(2, min(seq // ATTN_TQ, 32 // (n_sub * n_pairs)))
    assert (seq // ATTN_TQ) % blocks == 0
    cw = n_pairs * LANES
    pb = ATTN_WIDTH // cw
    kern = functools.partial(_attn_kernel, seq=seq, n_sub=n_sub, n_pairs=n_pairs, blocks=blocks)
    blk = (n_sub, seq, cw)
    return pl.pallas_call(
        kern,
        out_shape=(jax.ShapeDtypeStruct((n_seq, seq, ATTN_WIDTH), BF16),
                   jax.ShapeDtypeStruct((n_seq, seq, ATTN_WIDTH), F32)),
        grid=(n_seq // n_sub, pb),
        in_specs=[pl.BlockSpec(blk, lambda r, p: (r, 0, p)),
                  pl.BlockSpec(blk, lambda r, p: (r, 0, pb + p)),
                  pl.BlockSpec(blk, lambda r, p: (r, 0, 2 * pb + p)),
                  pl.BlockSpec((2 * n_pairs, 3, ATTN_TQ, ATTN_TK), lambda r, p: (p, 0, 0, 0))],
        out_specs=(pl.BlockSpec(blk, lambda r, p: (r, 0, p)),
                   pl.BlockSpec(blk, lambda r, p: (r, 0, p))),
        compiler_params=_cparams(("parallel", "parallel")),
        name=name,
    )(qkv, qkv, qkv, bias)


def _hgrn_mats():
    t = np.arange(HG_TILE)[:, None]
    u = np.arange(HG_TILE)[None, :]
    same = (t // HG_CHUNK) == (u // HG_CHUNK)
    return (jnp.asarray(same & (u <= t), dtype=BF16), jnp.asarray(same & (u >= t), dtype=BF16))


def _hgrn_kernel(xn_ref, wq_ref, wzf_ref, wzb_ref, wi_ref, wg_ref, wx_ref, lbf_ref, lbb_ref,
                 nw_ref, mf_ref, mb_ref, o_ref, x_out_ref, wcat_ref, gate_ref, acc_ref, qd_ref,
                 kv_ref, dec_ref, st_ref, *, seq):
    n_tiles = seq // HG_TILE
    cpt = HG_TILE // HG_CHUNK
    mid = HG_CHUNK // 2
    nt = (((1,), (1,)), ((), ()))
    w_refs = (wq_ref, wzf_ref, wzb_ref, wi_ref, wg_ref, wx_ref)

    def per_chunk_rows(rows):
        return jnp.concatenate([jnp.broadcast_to(r, (HG_CHUNK, HG_DK)) for r in rows], axis=0)

    scans = ((lbf_ref, mf_ref, False), (lbb_ref, mb_ref, True))

    for k, w_ref in enumerate(w_refs):
        wcat_ref[:, k * HG_DK:(k + 1) * HG_DK] = w_ref[...]

    def a_project(u):
        z = jnp.dot(xn_ref[u["rows"], :], wcat_ref[...], preferred_element_type=F32)
        u["q"] = z[:, 0:HG_DK]
        u["z"] = [z[:, HG_DK:2 * HG_DK], z[:, 2 * HG_DK:3 * HG_DK]]
        u["v"] = z[:, 3 * HG_DK:4 * HG_DK].astype(BF16)
        g = z[:, 4 * HG_DK:5 * HG_DK]
        gate_ref[u["rows"], :] = g * jax.nn.sigmoid(g) * nw_ref[...]
        x_out_ref[u["rows"], :] = jax.nn.sigmoid(z[:, 5 * HG_DK:]).astype(BF16)

    def a_gate(u):
        u["kk"], u["ghl"] = [], []
        for (lb_ref, _, _), z in zip(scans, u.pop("z")):
            kk = (1.0 - lb_ref[...]) / (1.0 + jnp.exp(z))
            g = jnp.log(1.0 - kk) * (1.0 / np.log(2.0))
            g_hi = g.astype(BF16)
            g_lo = (g - g_hi.astype(F32)).astype(BF16)
            u["kk"].append(kk)
            u["ghl"].append(jnp.concatenate([g_hi, g_lo], axis=1))

    def a_cumsum(u):
        u["sums"] = [jnp.dot(m_ref[...], ghl, preferred_element_type=F32)
                     for (_, m_ref, _), ghl in zip(scans, u.pop("ghl"))]

    def a_decay(u):
        q = u.pop("q")
        q_c, k_c, q_d, k_s, decs = [], [], [], [], []
        for (_, _, reverse), sums, kk in zip(scans, u.pop("sums"), u.pop("kk")):
            beta = sums[:, :HG_DK] + sums[:, HG_DK:]
            tot_row = 0 if reverse else HG_CHUNK - 1
            cen_row = mid if reverse else mid - 1
            tot = [beta[c * HG_CHUNK + tot_row:c * HG_CHUNK + tot_row + 1] for c in range(cpt)]
            cen = [beta[c * HG_CHUNK + cen_row:c * HG_CHUNK + cen_row + 1] for c in range(cpt)]
            d = beta - per_chunk_rows(cen)
            e_q = jnp.exp2(d)
            qc = (q * e_q).astype(BF16)
            kc = (kk * (1.0 / e_q)).astype(BF16)
            q_d.append(qc * per_chunk_rows([jnp.exp2(c_).astype(BF16) for c_ in cen]))
            k_s.append(kc * per_chunk_rows([jnp.exp2(t_ - c_).astype(BF16)
                                            for t_, c_ in zip(tot, cen)]))
            q_c.append(qc)
            k_c.append(kc)
            decs.append([jnp.exp2(t_) for t_ in tot])
        qd_ref[u["rows"], :] = jnp.concatenate(q_d, axis=1)
        u["ks"] = jnp.concatenate(k_s, axis=1)
        u["qc"], u["kc"] = q_c, k_c
        for c in range(cpt):
            dec_ref[u["t"] * cpt + c] = jnp.concatenate(
                [jnp.broadcast_to(decs[0][c], (SUBLANES, HG_DK)),
                 jnp.broadcast_to(decs[1][c], (SUBLANES, HG_DK))],
                axis=1)

    chunk_rows = [slice(c * HG_CHUNK, (c + 1) * HG_CHUNK) for c in range(cpt)]
    ti = lax.broadcasted_iota(jnp.int32, (HG_CHUNK, HG_CHUNK), 0)
    ui = lax.broadcasted_iota(jnp.int32, (HG_CHUNK, HG_CHUNK), 1)
    past, future = ui <= ti, ui >= ti

    def a_scores(u):
        u["a"] = [[lax.dot_general(qc[cr], kc[cr], nt, preferred_element_type=F32)
                   for qc, kc in zip(u["qc"], u["kc"])] for cr in chunk_rows]
        del u["qc"], u["kc"]

    def a_mask(u):
        u["a"] = [(jnp.where(past, a_f, 0.0) + jnp.where(future, a_b, 0.0)).astype(BF16)
                  for a_f, a_b in u.pop("a")]

    def a_values(u):
        v = u.pop("v")
        acc_ref[u["rows"], :] = jnp.concatenate(
            [jnp.dot(a, v[cr], preferred_element_type=F32) for a, cr in zip(u.pop("a"), chunk_rows)],
            axis=0)
        ks = u.pop("ks")
        for c, cr in enumerate(chunk_rows):
            kv_ref[u["t"] * cpt + c] = lax.dot_general(v[cr], ks[cr], (((0,), (0,)), ((), ())),
                                                       preferred_element_type=F32)

    fw = slice(0, HG_DK)
    bw = slice(HG_DK, 2 * HG_DK)
    state = dict(f=jnp.zeros((HG_DK, HG_DK), F32), b=jnp.zeros((HG_DK, HG_DK), F32))

    def a_state(u):
        for c in range(cpt):
            n = u["t"] * cpt + c
            st_ref[n, :, fw] = state["f"].astype(BF16)
            state["f"] = dec_ref[n][0:1, fw] * state["f"] + kv_ref[n, :, fw]

    tiles = [dict(t=t, r0=t * HG_TILE, rows=slice(t * HG_TILE, (t + 1) * HG_TILE))
             for t in range(n_tiles)]
    _emit_skewed([dict(u) for u in tiles],
                 (a_project, a_gate, a_cumsum, a_decay, a_scores, a_mask, a_values, a_state))

    def c_state(u):
        for c in reversed(range(cpt)):
            m = u["t"] * cpt + c
            st_ref[m, :, bw] = state["b"].astype(BF16)
            state["b"] = dec_ref[m][0:1, bw] * state["b"] + kv_ref[m, :, bw]

    def c_inter(u):
        u["inter"] = [lax.dot_general(qd_ref[u["r0"] + c * HG_CHUNK:u["r0"] + (c + 1) * HG_CHUNK, :],
                                      st_ref[u["t"] * cpt + c], nt, preferred_element_type=F32)
                      for c in range(cpt)]

    def c_sum(u):
        u["o"] = acc_ref[u["rows"], :] + jnp.concatenate(u.pop("inter"), axis=0)
        u["ms"] = jnp.mean(u["o"] * u["o"], axis=-1, keepdims=True)

    def c_out(u):
        o_ref[u["rows"], :] = (u.pop("o") * lax.rsqrt(u.pop("ms") + EPS)
                               * gate_ref[u["rows"], :]).astype(BF16)

    @pl.when(pl.program_id(0) >= 0)
    def _():
        _emit_skewed([dict(u) for u in reversed(tiles)], (c_state, c_inter, c_sum, c_out))


def _hgrn(xn3, w, lb_f, lb_b, nw, col_q, col_zf, col_zb, col_i, col_g, col_x):
    b, seq, d = xn3.shape
    mf, mb = _hgrn_mats()
    n_chunks = seq // HG_CHUNK

    def wspec(col):
        return pl.BlockSpec((d, HG_DK), lambda bi, h, col=col: (0, col + h))

    vec = pl.BlockSpec((1, HG_DK), lambda bi, h: (0, h))
    const = pl.BlockSpec((HG_TILE, HG_TILE), lambda bi, h: (0, 0))
    return pl.pallas_call(
        functools.partial(_hgrn_kernel, seq=seq),
        out_shape=[jax.ShapeDtypeStruct((b, seq, HG_HEADS * HG_DK), BF16)] * 2,
        grid=(b, HG_HEADS),
        in_specs=[pl.BlockSpec((None, seq, d), lambda bi, h: (bi, 0, 0)),
                  wspec(col_q), wspec(col_zf), wspec(col_zb), wspec(col_i), wspec(col_g),
                  wspec(col_x),
                  vec, vec, pl.BlockSpec((1, HG_DK), lambda bi, h: (0, 0)), const, const],
        out_specs=[pl.BlockSpec((None, seq, HG_DK), lambda bi, h: (bi, 0, h))] * 2,
        scratch_shapes=[pltpu.VMEM((d, 6 * HG_DK), BF16),
                        pltpu.VMEM((seq, HG_DK), F32),
                        pltpu.VMEM((seq, HG_DK), F32),
                        pltpu.VMEM((seq, 2 * HG_DK), BF16),
                        pltpu.VMEM((n_chunks, HG_DK, 2 * HG_DK), F32),
                        pltpu.VMEM((n_chunks, SUBLANES, 2 * HG_DK), F32),
                        pltpu.VMEM((n_chunks, HG_DK, 2 * HG_DK), BF16)],
        compiler_params=_cparams(("parallel", "parallel")),
        name="hgrn2",
    )(xn3, w, w, w, w, w, w, lb_f, lb_b, nw, mf, mb)


def _final_kernel(o0_ref, l0_ref, o1_ref, l1_ref, o2_ref, l2_ref, ga_ref, ob_ref,
                  za0_ref, za1_ref, zb0_ref, zb1_ref,
                  x_ref, wa_ref, wb_ref, wo_ref, out_ref, so1_ref, sl1_ref, so2_ref, sl2_ref,
                  *, tm, dilations):
    for src, dst, dil in ((o1_ref, so1_ref, dilations[0]), (l1_ref, sl1_ref, dilations[0]),
                          (o2_ref, so2_ref, dilations[1]), (l2_ref, sl2_ref, dilations[1])):
        for r in range(dil):
            val = src[r].astype(F32)
            for c in range(ATTN_WIDTH // LANES):
                dst[c, pl.ds(r, tm // dil, stride=dil), :] = val[:, c * LANES:(c + 1) * LANES]

    def natural(ref):
        return jnp.concatenate([ref[c] for c in range(ATTN_WIDTH // LANES)], axis=1)

    l0, l1, l2 = l0_ref[...], natural(sl1_ref), natural(sl2_ref)
    mx = jnp.maximum(jnp.maximum(l0, l1), l2)
    w0, w1, w2 = jnp.exp2(l0 - mx), jnp.exp2(l1 - mx), jnp.exp2(l2 - mx)
    num = w0 * o0_ref[...].astype(F32) + w1 * natural(so1_ref) + w2 * natural(so2_ref)
    ga = ga_ref[...].astype(F32)
    a = (num / (w0 + w1 + w2) * (ga * jax.nn.sigmoid(ga))).astype(BF16)

    y_a = jnp.dot(a, wa_ref[...], preferred_element_type=F32)
    y_b = jnp.dot(ob_ref[...], wb_ref[...], preferred_element_type=F32)
    gate_a = jnp.concatenate([za0_ref[...], za1_ref[...]], axis=1).astype(F32)
    zb = jnp.concatenate([zb0_ref[...], zb1_ref[...]], axis=1).astype(F32)
    merged = (gate_a * y_a + jax.nn.sigmoid(zb) * y_b).astype(BF16)
    out_ref[...] = x_ref[...] + jnp.dot(merged, wo_ref[...], preferred_element_type=F32)


def _final(o_g, lse_g, z_gate, z_ga, o_b, x3, wa, wb, wo, dilations, col_ga, col_zb):
    b, seq, d = x3.shape
    tm = 512
    aw = ATTN_WIDTH

    def rows(width, col=0):
        return pl.BlockSpec((None, tm, width), lambda bi, i, col=col: (bi, i, col))

    def perm(dil):
        return pl.BlockSpec((None, dil, tm // dil, aw), lambda bi, i: (bi, 0, i, 0))

    def full(shape):
        return pl.BlockSpec(shape, lambda bi, i: (0, 0))

    d1, d2 = dilations
    return pl.pallas_call(
        functools.partial(_final_kernel, tm=tm, dilations=dilations),
        out_shape=jax.ShapeDtypeStruct((b, seq, d), F32),
        grid=(b, seq // tm),
        in_specs=[rows(aw), rows(aw), perm(d1), perm(d1), perm(d2), perm(d2),
                  rows(aw, col_ga), rows(d), rows(aw, 0), rows(aw, 1),
                  rows(aw, col_zb), rows(aw, col_zb + 1), rows(d),
                  full(wa.shape), full(wb.shape), full(wo.shape)],
        out_specs=rows(d),
        scratch_shapes=[pltpu.VMEM((aw // LANES, tm, LANES), F32)] * 4,
        compiler_params=_cparams(("parallel", "parallel")),
        name="merge_out",
    )(o_g[0], lse_g[0], o_g[1], lse_g[1], o_g[2], lse_g[2], z_gate, o_b,
      z_ga, z_ga, z_gate, z_gate, x3, wa, wb, wo)


def kernel(x, norm_w, w_in, q_norm_w, k_norm_w, rel_bias, lb_fwd, lb_bwd, hg_norm_w,
           w_proj_a, w_proj_b, w_out):
    b, seq, d = x.shape
    n = b * seq
    layer = 0
    w = w_in[layer].astype(BF16)
    qkv_cols = 3 * len(ATTN_GROUPS) * ATTN_WIDTH
    dilations = tuple(dil for _, dil in ATTN_GROUPS)
    assert dilations[0] == 1

    xn_all = _norm(x.astype(F32), norm_w[layer].reshape(1, d).astype(F32), dilations[1:])
    xn = xn_all[0].reshape(n, d)

    hw = HG_HEADS * HG_DK
    c_ga = qkv_cols // ATTN_WIDTH
    c_gate_b = c_ga + 1 + 6 * hw // ATTN_WIDTH
    z_gate = _proj_gates(xn, w, c_ga, c_gate_b, hw // ATTN_WIDTH).reshape(b, seq, -1)

    o_g, lse_g = [], []
    for g, dilation in enumerate(dilations):
        sub_len = seq // dilation
        q_scale = HEAD_DIM ** -0.5 * LOG2E
        nw = jnp.stack([jnp.tile(q_norm_w[layer, g].astype(F32), HEADS_PER_GROUP) * q_scale,
                        jnp.tile(k_norm_w[layer, g].astype(F32), HEADS_PER_GROUP)]
                       ).reshape(2, 1, ATTN_WIDTH)
        qkv = _proj_qkv(xn_all[g].reshape(n, d), w, g, nw, f"proj_qkv_d{dilation}")
        heads = slice(g * HEADS_PER_GROUP, (g + 1) * HEADS_PER_GROUP)
        bias = _attn_bias(rel_bias.astype(F32)[:, heads] * LOG2E, dilation)
        o, lse = _attention(qkv.reshape(b * dilation, sub_len, 3 * ATTN_WIDTH), bias,
                            f"attn_d{dilation}")
        shape = (b, seq, ATTN_WIDTH) if dilation == 1 else (b, dilation, sub_len, ATTN_WIDTH)
        o_g.append(o.reshape(shape))
        lse_g.append(lse.reshape(shape))

    lb_f = jnp.cumsum(jax.nn.softmax(lb_fwd.astype(F32), axis=0), axis=0)[layer].reshape(1, hw)
    lb_b = jnp.cumsum(jax.nn.softmax(lb_bwd.astype(F32), axis=0), axis=0)[layer].reshape(1, hw)
    hb = hw // LANES
    c_qb = (qkv_cols + ATTN_WIDTH) // LANES
    o_b, z_ga = _hgrn(xn_all[0], w, lb_f, lb_b, hg_norm_w[layer].reshape(1, HG_DK).astype(F32),
                      col_q=c_qb, col_zf=c_qb + hb, col_zb=c_qb + 2 * hb, col_i=c_qb + 3 * hb,
                      col_g=c_qb + 4 * hb, col_x=c_qb + 5 * hb)

    out = _final(o_g, lse_g, z_gate, z_ga, o_b, x.astype(F32),
                 w_proj_a[layer].astype(BF16), w_proj_b[layer].astype(BF16),
                 w_out[layer].astype(BF16), dilations[1:], col_ga=0, col_zb=1)
    return out.astype(x.dtype)
```
